```python
import math
import jax, jax.numpy as jnp
from jax import lax
import numpy as np

D_MODEL = 4096
BATCH = 1
SEQ = 16384
DEPTH = 2
DEC_BATCH = 16
DEC_SEQ = 32
PAST_LEN = 1024

CHUNK = 64
HEAD_DIM = 128
N_HEADS = D_MODEL // HEAD_DIM
N_KV_HEADS = N_HEADS // 4
GROUP = N_HEADS // N_KV_HEADS
IDX_HEADS = 32
IDX_DIM = 128
TOPK_MAX = 256
N_BUCKETS = 32
REL_MAX_DIST = 128
D_RNN = ((4 * D_MODEL // 3) // 256) * 256
N_RG_BLOCKS = 16
RG_BLOCK = D_RNN // N_RG_BLOCKS
RG_C = 8.0
CONV_W = 4
D_FF = ((8 * D_MODEL // 3 + 255) // 256) * 256
EPS = 1e-6
_IN_SIZES = (N_HEADS * HEAD_DIM, N_KV_HEADS * HEAD_DIM, N_KV_HEADS * HEAD_DIM,
             IDX_HEADS * IDX_DIM, IDX_DIM, IDX_HEADS, D_RNN, D_RNN, D_MODEL, D_MODEL)
N_IN = sum(_IN_SIZES)

kernel_name = "dsa_rglru_hybrid_streaming_step"


def _rmsnorm(x, g):
    xf = x.astype(jnp.float32)
    y = xf * lax.rsqrt(jnp.mean(xf * xf, axis=-1, keepdims=True) + EPS) * g.astype(jnp.float32)
    return y.astype(x.dtype)


def _split_in(proj):
    parts = []
    start = 0
    for n in _IN_SIZES:
        parts.append(proj[..., start:start + n])
        start += n
    return parts


def _t5_bucket(rel):
    nb = N_BUCKETS // 2
    max_exact = nb // 2
    side = jnp.where(rel > 0, nb, 0)
    n = jnp.abs(rel)
    nf = jnp.maximum(n, 1).astype(jnp.float32)
    large = max_exact + (jnp.log(nf / max_exact) / math.log(REL_MAX_DIST / max_exact)
                         * (nb - max_exact)).astype(jnp.int32)
    large = jnp.minimum(large, nb - 1)
    return side + jnp.where(n < max_exact, n, large)


def _gather_rows(rows, idx):
    return jax.vmap(lambda r, ix: r[ix])(rows, idx)


def _sparse_attention(q, k_rows, v_rows, q_idx, w_idx, kidx_rows, q_pos, key_limit, rel_bias):
    B, Tq, H, dh = q.shape
    L = k_rows.shape[1]
    n_sel = min(TOPK_MAX, L // 4)
    f32 = jnp.float32
    s = jnp.einsum('bthd,bsd->bths', q_idx.astype(f32), kidx_rows.astype(f32)) * (IDX_DIM ** -0.5)
    score = jnp.einsum('bths,bth->bts', jax.nn.relu(s), w_idx.astype(f32))
    key_pos = jnp.arange(L, dtype=jnp.int32)
    admissible = key_pos[None, :] < key_limit[:, None]
    score = jnp.where(admissible[None], score, -jnp.inf)
    _, idx = lax.top_k(score, n_sel)
    kg = _gather_rows(k_rows, idx).astype(f32)
    vg = _gather_rows(v_rows, idx).astype(f32)
    valid = idx < key_limit[None, :, None]
    qg = q.reshape(B, Tq, N_KV_HEADS, GROUP, dh).astype(f32)
    logits = jnp.einsum('btkgd,btskd->btkgs', qg, kg) * (dh ** -0.5)
    bucket = _t5_bucket(idx - q_pos[None, :, None])
    bias = rel_bias.astype(f32)[bucket]
    bias = bias.reshape(B, Tq, n_sel, N_KV_HEADS, GROUP).transpose(0, 1, 3, 4, 2)
    logits = jnp.where(valid[:, :, None, None, :], logits + bias, -jnp.inf)
    p = jax.nn.softmax(logits, axis=-1)
    o = jnp.einsum('btkgs,btskd->btkgd', p, vg)
    return o.reshape(B, Tq, H * dh).astype(q.dtype)


def _prompt_attention(q, k, v, q_idx, w_idx, kidx, rel_bias):
    B, T = q.shape[:2]
    n_chunks = T // CHUNK

    def to_blocks(a):
        return a.reshape((B, n_chunks, CHUNK) + a.shape[2:]).swapaxes(0, 1)

    def block(args):
        c, qb, qib, wib = args
        q_pos = c * CHUNK + jnp.arange(CHUNK, dtype=jnp.int32)
        limit = jnp.full((CHUNK,), (c + 1) * CHUNK, dtype=jnp.int32)
        return _sparse_attention(qb, k, v, qib, wib, kidx, q_pos, limit, rel_bias)

    out = lax.map(block, (jnp.arange(n_chunks, dtype=jnp.int32), to_blocks(q),
                          to_blocks(q_idx), to_blocks(w_idx)))
    return out.swapaxes(0, 1).reshape(B, T, -1)


def _causal_conv(xr, buf, w, b):
    T = xr.shape[1]
    xp = jnp.concatenate([buf.astype(xr.dtype), xr], axis=1)
    y = b
    for j in range(CONV_W):
        y = y + xp[:, j:j + T] * w[j]
    return y, xp[:, -(CONV_W - 1):]


def _rglru(xc, h0, wa, ba, wx, bx, lam):
    B, T, _ = xc.shape
    f32 = jnp.float32
    xb = xc.reshape(B, T, N_RG_BLOCKS, RG_BLOCK)
    r = jax.nn.sigmoid(jnp.einsum('btni,nij->btnj', xb, wa).reshape(B, T, D_RNN) + ba)
    i = jax.nn.sigmoid(jnp.einsum('btni,nij->btnj', xb, wx).reshape(B, T, D_RNN) + bx)
    log_a = -RG_C * r.astype(f32) * jax.nn.softplus(-lam.astype(f32))
    a = jnp.exp(log_a)
    u = jnp.sqrt(-jnp.expm1(2.0 * log_a)) * (i * xc).astype(f32)

    def step(h, au):
        a_t, u_t = au
        h = a_t * h + u_t
        return h, h

    h_last, hs = lax.scan(step, h0.astype(f32), (a.swapaxes(0, 1), u.swapaxes(0, 1)))
    return hs.swapaxes(0, 1).astype(xc.dtype), h_last.astype(xc.dtype)


def _layer(x, hist, norm_mix, w_in, q_norm, k_norm, rel_bias, conv_w, conv_b, rg_wa, rg_ba,
           rg_wx, rg_bx, rg_lambda, w_out_attn, w_out_rg, w_o, norm_ffn, ffn_w1, ffn_w3, ffn_w2):
    B, T, _ = x.shape
    h = _rmsnorm(x, norm_mix)
    q, k, v, qi, ki, wi, xr, gr, ga, gb = _split_in(h @ w_in)
    q = _rmsnorm(q.reshape(B, T, N_HEADS, HEAD_DIM), q_norm)
    k = _rmsnorm(k.reshape(B, T, N_KV_HEADS, HEAD_DIM), k_norm)
    v = v.reshape(B, T, N_KV_HEADS, HEAD_DIM)
    qi = qi.reshape(B, T, IDX_HEADS, IDX_DIM)
    wi = wi * (IDX_HEADS ** -0.5)
    if hist is None:
        o_a = _prompt_attention(q, k, v, qi, wi, ki, rel_bias)
        conv_buf = jnp.zeros((B, CONV_W - 1, D_RNN), x.dtype)
        h0 = jnp.zeros((B, D_RNN), x.dtype)
    else:
        k_past, v_past, ki_past, conv_buf, h0 = hist
        past = k_past.shape[1]
        k_all = jnp.concatenate([k_past.astype(k.dtype), k], axis=1)
        v_all = jnp.concatenate([v_past.astype(v.dtype), v], axis=1)
        ki_all = jnp.concatenate([ki_past.astype(ki.dtype), ki], axis=1)
        q_pos = past + jnp.arange(T, dtype=jnp.int32)
        limit = jnp.full((T,), past + T, dtype=jnp.int32)
        o_a = _sparse_attention(q, k_all, v_all, qi, wi, ki_all, q_pos, limit, rel_bias)
    xc, conv_new = _causal_conv(xr, conv_buf, conv_w, conv_b)
    y_rg, h_new = _rglru(xc, h0, rg_wa, rg_ba, rg_wx, rg_bx, rg_lambda)
    o_b = y_rg * jax.nn.gelu(gr)
    merged = jax.nn.sigmoid(ga) * (o_a @ w_out_attn) + jax.nn.sigmoid(gb) * (o_b @ w_out_rg)
    x = x + merged @ w_o
    hf = _rmsnorm(x, norm_ffn)
    x = x + (jax.nn.silu(hf @ ffn_w1) * (hf @ ffn_w3)) @ ffn_w2
    return x, (k, v, ki, conv_new, h_new)


def setup_inputs(seed: int = 0) -> dict:
    key = jax.random.key(seed)
    ks = jax.random.split(key, 28)
    f32 = jnp.float32

    def nrm(k, shape, scale):
        return jax.random.normal(k, shape, f32) * scale

    u = jax.random.uniform(ks[18], (DEPTH, D_RNN), f32, 0.9, 0.999)
    a = u ** (1.0 / RG_C)
    return {
        "x_prompt": nrm(ks[0], (BATCH, SEQ, D_MODEL), 1.0),
        "x_sample": nrm(ks[1], (DEC_BATCH, DEC_SEQ, D_MODEL), 1.0),
        "cache_k": nrm(ks[2], (DEPTH, DEC_BATCH, PAST_LEN, N_KV_HEADS, HEAD_DIM), 1.0),
        "cache_v": nrm(ks[3], (DEPTH, DEC_BATCH, PAST_LEN, N_KV_HEADS, HEAD_DIM), 1.0),
        "cache_kidx": nrm(ks[4], (DEPTH, DEC_BATCH, PAST_LEN, IDX_DIM), 1.0),
        "state_conv": nrm(ks[5], (DEPTH, DEC_BATCH, CONV_W - 1, D_RNN), 0.5),
        "state_rglru": nrm(ks[6], (DEPTH, DEC_BATCH, D_RNN), 0.5),
        "norm_mix": 1.0 + nrm(ks[7], (DEPTH, D_MODEL), 0.02),
        "w_in": nrm(ks[8], (DEPTH, D_MODEL, N_IN), D_MODEL ** -0.5),
        "q_norm": 1.0 + nrm(ks[9], (DEPTH, HEAD_DIM), 0.02),
        "k_norm": 1.0 + nrm(ks[10], (DEPTH, HEAD_DIM), 0.02),
        "rel_bias": nrm(ks[11], (N_BUCKETS, N_HEADS), 0.5),
        "conv_w": nrm(ks[12], (DEPTH, CONV_W, D_RNN), CONV_W ** -0.5),
        "conv_b": nrm(ks[13], (DEPTH, D_RNN), 0.02),
        "rg_wa": nrm(ks[14], (DEPTH, N_RG_BLOCKS, RG_BLOCK, RG_BLOCK), RG_BLOCK ** -0.5),
        "rg_ba": nrm(ks[15], (DEPTH, D_RNN), 0.02),
        "rg_wx": nrm(ks[16], (DEPTH, N_RG_BLOCKS, RG_BLOCK, RG_BLOCK), RG_BLOCK ** -0.5),
        "rg_bx": nrm(ks[17], (DEPTH, D_RNN), 0.02),
        "rg_lambda": jnp.log(a) - jnp.log1p(-a),
        "w_out_attn": nrm(ks[19], (DEPTH, N_HEADS * HEAD_DIM, D_MODEL), (N_HEADS * HEAD_DIM) ** -0.5),
        "w_out_rg": nrm(ks[20], (DEPTH, D_RNN, D_MODEL), D_RNN ** -0.5),
        "w_o": nrm(ks[21], (DEPTH, D_MODEL, D_MODEL), D_MODEL ** -0.5),
        "norm_ffn": 1.0 + nrm(ks[22], (DEPTH, D_MODEL), 0.02),
        "ffn_w1": nrm(ks[23], (DEPTH, D_MODEL, D_FF), D_MODEL ** -0.5),
        "ffn_w3": nrm(ks[24], (DEPTH, D_MODEL, D_FF), D_MODEL ** -0.5),
        "ffn_w2": nrm(ks[25], (DEPTH, D_FF, D_MODEL), D_FF ** -0.5),
    }


def reference(x_prompt, x_sample, cache_k, cache_v, cache_kidx, state_conv, state_rglru,
              norm_mix, w_in, q_norm, k_norm, rel_bias, conv_w, conv_b, rg_wa, rg_ba, rg_wx,
              rg_bx, rg_lambda, w_out_attn, w_out_rg, w_o, norm_ffn, ffn_w1, ffn_w3, ffn_w2):
    xp = x_prompt
    xs = x_sample
    pk, pv, pki, pconv, ph = [], [], [], [], []
    sk, sv, ski, sconv, sh = [], [], [], [], []
    for l in range(DEPTH):
        params = (norm_mix[l], w_in[l], q_norm[l], k_norm[l], rel_bias, conv_w[l], conv_b[l],
                  rg_wa[l], rg_ba[l], rg_wx[l], rg_bx[l], rg_lambda[l], w_out_attn[l],
                  w_out_rg[l], w_o[l], norm_ffn[l], ffn_w1[l], ffn_w3[l], ffn_w2[l])
        xp, (k1, v1, ki1, c1, h1) = _layer(xp, None, *params)
        hist = (cache_k[l], cache_v[l], cache_kidx[l], state_conv[l], state_rglru[l])
        xs, (k2, v2, ki2, c2, h2) = _layer(xs, hist, *params)
        pk.append(k1); pv.append(v1); pki.append(ki1); pconv.append(c1); ph.append(h1)
        sk.append(k2); sv.append(v2); ski.append(ki2); sconv.append(c2); sh.append(h2)
    return (xp, xs,
            jnp.stack(pk), jnp.stack(pv), jnp.stack(pki), jnp.stack(pconv), jnp.stack(ph),
            jnp.stack(sk), jnp.stack(sv), jnp.stack(ski), jnp.stack(sconv), jnp.stack(sh))
```

```python
import functools
import math

import numpy as np
import jax
import jax.numpy as jnp
from jax import lax
from jax.experimental import pallas as pl
from jax.experimental.pallas import tpu as pltpu

F32 = jnp.float32
BF16 = jnp.bfloat16

CHUNK = 64
HEAD_DIM = 128
GROUP = 4
TOPK_MAX = 256
N_BUCKETS = 32
REL_MAX_DIST = 128
RG_C = 8.0
CONV_W = 4
EPS = 1e-6

LANES = 128
SUBLANES = 8
KEY_TILE = 256
VMEM_CAP = 56 << 20
NEG = -1e30


def _params(sem, vmem_bytes):
    limit = min(max(int(vmem_bytes) + (6 << 20), 24 << 20), VMEM_CAP)
    return pltpu.CompilerParams(dimension_semantics=sem, vmem_limit_bytes=limit)


def _rmsnorm_kernel(x_ref, g_ref, o_ref):
    x = x_ref[...]
    ms = jnp.mean(x * x, axis=-1, keepdims=True)
    o_ref[...] = (x * lax.rsqrt(ms + EPS) * g_ref[...]).astype(o_ref.dtype)


def _rmsnorm(x, g, tm):
    M, D = x.shape
    tm = min(tm, 256)
    return pl.pallas_call(
        _rmsnorm_kernel,
        grid=(M // tm,),
        in_specs=[pl.BlockSpec((tm, D), lambda i: (i, 0)),
                  pl.BlockSpec((1, D), lambda i: (0, 0))],
        out_specs=pl.BlockSpec((tm, D), lambda i: (i, 0)),
        out_shape=jax.ShapeDtypeStruct((M, D), BF16),
        compiler_params=_params(("parallel",), 2 * tm * D * 6 + 2 * tm * D * 4),
        name="rmsnorm",
    )(x, g.reshape(1, D))


def _mm_kernel(*refs, n_w, n_ext, n_vec, n_out, nk, epilogue):
    x_ref = refs[0]
    w_refs = refs[1:1 + n_w]
    p = 1 + n_w
    ext_refs = refs[p:p + n_ext]
    p += n_ext
    vec_refs = refs[p:p + n_vec]
    p += n_vec
    out_refs = refs[p:p + n_out]
    acc_refs = refs[p + n_out:]

    def finish(accs):
        res = epilogue(accs, [e[...] for e in ext_refs], [v[...] for v in vec_refs])
        for o, r in zip(out_refs, res):
            o[...] = r.astype(o.dtype)

    if nk == 1:
        finish([jnp.dot(x_ref[...], w[...], preferred_element_type=F32) for w in w_refs])
        return

    k = pl.program_id(2)

    @pl.when(k == 0)
    def _():
        for a in acc_refs:
            a[...] = jnp.zeros_like(a)

    for a, w in zip(acc_refs, w_refs):
        a[...] += jnp.dot(x_ref[...], w[...], preferred_element_type=F32)

    @pl.when(k == nk - 1)
    def _():
        finish([a[...] for a in acc_refs])


def _matmul(x, ws, epilogue, out_dtypes, *, tm, tn, tk=None, n=None, exts=(), vecs=(),
            x_map=None, w_map=None, name="matmul"):
    M = x.shape[0]
    n = ws[0].shape[1] if n is None else n
    tk = x.shape[1] if tk is None else tk
    nk = (x.shape[1] // tk) if x_map is None else 1
    assert M % tm == 0 and n % tn == 0
    x_map = x_map or (lambda j, i, k: (i, k))
    w_map = w_map or (lambda j, i, k: (k, j))
    mn_map = lambda j, i, k: (i, j)
    in_specs = [pl.BlockSpec((tm, tk), x_map)]
    in_specs += [pl.BlockSpec((tk, tn), w_map) for _ in ws]
    in_specs += [pl.BlockSpec((tm, tn), mn_map) for _ in exts]
    in_specs += [pl.BlockSpec((1, tn), lambda j, i, k: (0, j)) for _ in vecs]
    out_specs = [pl.BlockSpec((tm, tn), mn_map) for _ in out_dtypes]
    out_shape = [jax.ShapeDtypeStruct((M, n), dt) for dt in out_dtypes]
    scratch = [pltpu.VMEM((tm, tn), F32) for _ in ws] if nk > 1 else []
    vmem = (2 * (tm * tk * 2 + len(ws) * tk * tn * 2 + (len(exts) + len(out_dtypes)) * tm * tn * 4)
            + 3 * len(ws) * tm * tn * 4)
    kern = functools.partial(_mm_kernel, n_w=len(ws), n_ext=len(exts), n_vec=len(vecs),
                             n_out=len(out_dtypes), nk=nk, epilogue=epilogue)
    outs = pl.pallas_call(
        kern,
        grid=(n // tn, M // tm, nk),
        in_specs=in_specs,
        out_specs=out_specs,
        out_shape=out_shape,
        scratch_shapes=scratch,
        compiler_params=_params(("parallel", "parallel", "arbitrary"), vmem),
        name=name,
    )(x, *ws, *exts, *[v.reshape(1, -1) for v in vecs])
    return outs


def _ep_identity(accs, exts, vecs):
    return (accs[0],)


def _ep_headnorm(scale):
    def ep(accs, exts, vecs):
        a, g = accs[0], vecs[0]
        outs = []
        for j in range(a.shape[1] // HEAD_DIM):
            aj = a[:, j * HEAD_DIM:(j + 1) * HEAD_DIM]
            ms = jnp.mean(aj * aj, axis=-1, keepdims=True)
            outs.append(aj * lax.rsqrt(ms + EPS) * g[:, j * HEAD_DIM:(j + 1) * HEAD_DIM])
        y = jnp.concatenate(outs, axis=1)
        return (y * scale if scale != 1.0 else y,)
    return ep


def _ep_gate(accs, exts, vecs):
    return (jax.nn.sigmoid(exts[0]) * accs[0],)


def _ep_gate_add(accs, exts, vecs):
    return (exts[1] + jax.nn.sigmoid(exts[0]) * accs[0],)


def _ep_residual(accs, exts, vecs):
    return (exts[0] + accs[0],)


def _ep_swiglu(accs, exts, vecs):
    return (jax.nn.silu(accs[0]) * accs[1],)


def _softplus(x):
    return jnp.maximum(x, 0.0) + jnp.log1p(jnp.exp(-jnp.abs(x)))


def _ep_rglru_gates(accs, exts, vecs):
    xc = exts[0]
    r = jax.nn.sigmoid(accs[0] + vecs[0])
    i = jax.nn.sigmoid(accs[1] + vecs[1])
    log_a = -RG_C * r * _softplus(-vecs[2])
    a = jnp.exp(log_a)
    u = jnp.sqrt(1.0 - a * a) * (i * xc)
    return (a, u)


def _select_kernel(qi_ref, wi_ref, ki_ref, mask_ref, q2_ref, w2_ref, key_ref, *,
                   TQ, TK, NT, HI, lend_a, lend_b, n_sel, wscale):
    nl = TK // LANES
    lend = lend_a * pl.program_id(0) + lend_b
    nt = (lend + TK - 1) // TK
    ksel = jnp.minimum(n_sel, lend).astype(F32)

    wi = wi_ref[...] * wscale
    for h in range(HI):
        q2_ref[h * TQ:(h + 1) * TQ, :] = qi_ref[:, h * LANES:(h + 1) * LANES]
        w2_ref[h * TQ:(h + 1) * TQ, :] = jnp.broadcast_to(wi[:, h:h + 1], (TQ, LANES))

    lane = lax.broadcasted_iota(jnp.int32, (TQ, TK), 1)
    int_min = jnp.int32(-2 ** 31)

    def to_key(s):
        b = pltpu.bitcast(s, jnp.int32)
        return b ^ ((b >> 31) & jnp.int32(0x7FFFFFFF))

    def score_tile(j, carry):
        smin, smax = carry
        start = pl.multiple_of(j * TK, TK)
        kt = ki_ref[pl.ds(start, TK), :]
        s = lax.dot_general(q2_ref[...], kt, (((1,), (1,)), ((), ())),
                            preferred_element_type=F32)
        cols = []
        for l in range(nl):
            acc = jnp.zeros((TQ, LANES), F32)
            for h in range(HI):
                acc = acc + (jnp.maximum(s[h * TQ:(h + 1) * TQ, l * LANES:(l + 1) * LANES], 0.0)
                             * w2_ref[h * TQ:(h + 1) * TQ, :])
            cols.append(acc)
        sc = jnp.concatenate(cols, axis=1)
        valid = (lane + j * TK) < lend
        key_ref[j] = jnp.where(valid, to_key(sc), int_min)
        lo_s = jnp.where(valid, sc, jnp.inf)
        hi_s = jnp.where(valid, sc, -jnp.inf)
        for l in range(nl):
            smin = jnp.minimum(smin, lo_s[:, l * LANES:(l + 1) * LANES])
            smax = jnp.maximum(smax, hi_s[:, l * LANES:(l + 1) * LANES])
        return smin, smax

    smin, smax = lax.fori_loop(
        0, nt, score_tile,
        (jnp.full((TQ, LANES), jnp.inf, F32), jnp.full((TQ, LANES), -jnp.inf, F32)))
    lo = to_key(jnp.broadcast_to(jnp.min(smin, axis=1, keepdims=True), (TQ, LANES)))
    hi = to_key(jnp.broadcast_to(jnp.max(smax, axis=1, keepdims=True), (TQ, LANES)))

    def count_ge(mid):
        def body(j, c):
            kk = key_ref[j]
            for l in range(nl):
                c = c + jnp.where(kk[:, l * LANES:(l + 1) * LANES] >= mid, 1.0, 0.0)
            return c
        c = lax.fori_loop(0, nt, body, jnp.zeros((TQ, LANES), F32))
        return jnp.broadcast_to(jnp.sum(c, axis=1, keepdims=True), (TQ, LANES))

    def open_rows(lo, hi):
        return jnp.max(jnp.where(lo < hi, 1.0, 0.0))

    def cond(st):
        return st[2] > 0.5

    def body(st):
        lo, hi, _ = st
        active = lo < hi
        mid = (lo >> 1) + (hi >> 1) + ((lo | hi) & 1)
        c = count_ge(mid)
        ge = c >= ksel
        lo_n = jnp.where(ge, mid, lo)
        hi_n = jnp.where(c == ksel, mid, jnp.where(ge, hi, mid - 1))
        lo = jnp.where(active, lo_n, lo)
        hi = jnp.where(active, hi_n, hi)
        return lo, hi, open_rows(lo, hi)

    tau, _, _ = lax.while_loop(cond, body, (lo, hi, open_rows(lo, hi)))
    tau_t = jnp.concatenate([tau] * nl, axis=1)

    def write(j, _):
        mask_ref[0, j] = jnp.where(key_ref[j] >= tau_t, 1, 0).astype(jnp.int8)
        return 0

    lax.fori_loop(0, nt, write, 0)

    def clear(j, _):
        mask_ref[0, j] = jnp.zeros((TQ, TK), jnp.int8)
        return 0

    lax.fori_loop(nt, NT, clear, 0)


def _select(qi, wi, ki, *, TQ, NT, lend_a, lend_b, n_sel, batched):
    TK = KEY_TILE
    Mq = qi.shape[0]
    HI = wi.shape[1]
    steps = Mq // TQ
    if batched:
        ki_spec = pl.BlockSpec((None, NT * TK, LANES), lambda i: (i, 0, 0))
    else:
        ki_spec = pl.BlockSpec((NT * TK, LANES), lambda i: (0, 0))
    kern = functools.partial(_select_kernel, TQ=TQ, TK=TK, NT=NT, HI=HI, lend_a=lend_a,
                             lend_b=lend_b, n_sel=n_sel,
                             wscale=float(HI ** -0.5 * LANES ** -0.5))
    vmem = (2 * (TQ * HI * LANES * 2 + NT * TK * LANES * 2 + NT * TQ * TK)
            + HI * TQ * LANES * 6 + NT * TQ * TK * 4 + 3 * HI * TQ * TK * 4)
    return pl.pallas_call(
        kern,
        grid=(steps,),
        in_specs=[pl.BlockSpec((TQ, HI * LANES), lambda i: (i, 0)),
                  pl.BlockSpec((TQ, HI), lambda i: (i, 0)),
                  ki_spec],
        out_specs=pl.BlockSpec((1, NT, TQ, TK), lambda i: (i, 0, 0, 0)),
        out_shape=jax.ShapeDtypeStruct((steps, NT, TQ, TK), jnp.int8),
        scratch_shapes=[pltpu.VMEM((HI * TQ, LANES), BF16),
                        pltpu.VMEM((HI * TQ, LANES), F32),
                        pltpu.VMEM((NT, TQ, TK), jnp.int32)],
        compiler_params=_params(("parallel",), vmem),
        name="dsa_select",
    )(qi, wi, ki)


def _attn_kernel(q_ref, k_ref, v_ref, mask_ref, nb_ref, o_ref, q2_ref, m_ref, l_ref, acc_ref, *,
                 TQ, TK, G, lend_a, lend_b, step_axis):
    nl = TK // LANES
    R = G * TQ
    lend = lend_a * pl.program_id(step_axis) + lend_b
    jl = (lend - 1) // TK

    for g in range(G):
        q2_ref[g * TQ:(g + 1) * TQ, :] = q_ref[:, g * HEAD_DIM:(g + 1) * HEAD_DIM]
    m_ref[...] = jnp.full((R, LANES), NEG, F32)
    l_ref[...] = jnp.zeros((R, LANES), F32)
    acc_ref[...] = jnp.zeros((R, HEAD_DIM), F32)

    def tile(j, half):
        start = pl.multiple_of(j * TK, TK)
        kt = k_ref[pl.ds(start, TK), :]
        vt = v_ref[pl.ds(start, TK), :]
        s = lax.dot_general(q2_ref[...], kt, (((1,), (1,)), ((), ())),
                            preferred_element_type=F32)
        madd = jnp.where(mask_ref[0, j].astype(jnp.int32) != 0, 0.0, NEG)
        rows = []
        for g in range(G):
            sg = s[g * TQ:(g + 1) * TQ, :]
            if half is not None:
                sg = sg + nb_ref[0, g, :, half * TK:(half + 1) * TK]
            rows.append(sg + madd)
        s = jnp.concatenate(rows, axis=0)
        m_old = m_ref[...]
        m_cur = s[:, :LANES]
        for l in range(1, nl):
            m_cur = jnp.maximum(m_cur, s[:, l * LANES:(l + 1) * LANES])
        m_new = jnp.maximum(m_old, jnp.broadcast_to(jnp.max(m_cur, axis=1, keepdims=True), (R, LANES)))
        alpha = jnp.exp(m_old - m_new)
        p = jnp.exp(s - jnp.concatenate([m_new] * nl, axis=1))
        psum = p[:, :LANES]
        for l in range(1, nl):
            psum = psum + p[:, l * LANES:(l + 1) * LANES]
        l_ref[...] = alpha * l_ref[...] + jnp.broadcast_to(
            jnp.sum(psum, axis=1, keepdims=True), (R, LANES))
        acc_ref[...] = alpha * acc_ref[...] + jnp.dot(
            p.astype(BF16), vt, preferred_element_type=F32)
        m_ref[...] = m_new

    def far(j, _):
        tile(j, None)
        return 0

    lax.fori_loop(0, jnp.maximum(jl - 1, 0), far, 0)

    @pl.when(jl >= 1)
    def _():
        tile(jl - 1, 0)

    tile(jl, 1)

    o = acc_ref[...] / l_ref[...]
    for g in range(G):
        o_ref[:, g * HEAD_DIM:(g + 1) * HEAD_DIM] = o[g * TQ:(g + 1) * TQ, :].astype(o_ref.dtype)


def _attention(q, k, v, mask, nb, *, TQ, lend_a, lend_b, batched):
    TK = KEY_TILE
    G = GROUP
    Mq, HD = q.shape
    KV = HD // (G * HEAD_DIM)
    steps = Mq // TQ
    NT = mask.shape[1]
    L = k.shape[-2]
    P = nb.shape[0]
    if batched:
        grid = (steps, KV)
        q_map = lambda b, h: (b, h)
        kv_spec = pl.BlockSpec((None, L, HEAD_DIM), lambda b, h: (b, 0, h))
        mask_map = lambda b, h: (b, 0, 0, 0)
        nb_map = lambda b, h: (0, h, 0, 0)
        step_axis = 0
    else:
        grid = (KV, steps)
        q_map = lambda h, c: (c, h)
        kv_spec = pl.BlockSpec((L, HEAD_DIM), lambda h, c: (0, h))
        mask_map = lambda h, c: (c, 0, 0, 0)
        nb_map = lambda h, c: (c % P, h, 0, 0)
        step_axis = 1
    kern = functools.partial(_attn_kernel, TQ=TQ, TK=TK, G=G, lend_a=lend_a, lend_b=lend_b,
                             step_axis=step_axis)
    R = G * TQ
    vmem = (2 * (2 * TQ * G * HEAD_DIM * 2 + 2 * L * HEAD_DIM * 2 + NT * TQ * TK + G * TQ * 2 * TK * 4)
            + R * LANES * 14 + 6 * R * TK * 4)
    return pl.pallas_call(
        kern,
        grid=grid,
        in_specs=[pl.BlockSpec((TQ, G * HEAD_DIM), q_map),
                  kv_spec, kv_spec,
                  pl.BlockSpec((1, NT, TQ, TK), mask_map),
                  pl.BlockSpec((1, G, TQ, 2 * TK), nb_map)],
        out_specs=pl.BlockSpec((TQ, G * HEAD_DIM), q_map),
        out_shape=jax.ShapeDtypeStruct((Mq, HD), BF16),
        scratch_shapes=[pltpu.VMEM((R, HEAD_DIM), BF16),
                        pltpu.VMEM((R, LANES), F32),
                        pltpu.VMEM((R, LANES), F32),
                        pltpu.VMEM((R, HEAD_DIM), F32)],
        compiler_params=_params(("parallel", "arbitrary"), vmem),
        name="dsa_attention",
    )(q, k, v, mask, nb)


def _t5_bucket_np(rel):
    nb = N_BUCKETS // 2
    max_exact = nb // 2
    side = np.where(rel > 0, nb, 0)
    n = np.abs(rel)
    nf = np.maximum(n, 1).astype(np.float32)
    large = max_exact + (np.log(nf / np.float32(max_exact))
                         / np.float32(math.log(REL_MAX_DIST / max_exact))
                         * np.float32(nb - max_exact)).astype(np.int32)
    large = np.minimum(large, nb - 1)
    return side + np.where(n < max_exact, n, large)


def _near_bias(rel_bias, TQ, TK, phases):
    t = np.arange(TQ)[:, None]
    j = np.arange(2 * TK)[None, :]
    buckets = np.stack([_t5_bucket_np(j - TK - ph + TQ - t) for ph in phases])
    far = int(_t5_bucket_np(np.array(TQ - TK - 2)))
    assert far == int(_t5_bucket_np(np.array(-10 ** 6)))
    rb = rel_bias.astype(F32)
    tab = rb[buckets] - rb[far][None, None, None, :]
    return tab.transpose(0, 3, 1, 2)


def _conv_kernel(x_ref, halo_ref, w_ref, b_ref, xc_ref, xcb_ref, ext_ref, *, tt):
    ext_ref[0:SUBLANES, :] = halo_ref[...]
    ext_ref[SUBLANES:, :] = x_ref[...]
    y = jnp.broadcast_to(b_ref[...], x_ref.shape)
    for j in range(CONV_W):
        off = SUBLANES - (CONV_W - 1) + j
        y = y + ext_ref[off:off + tt, :] * w_ref[j:j + 1, :]
    xc_ref[...] = y
    xcb_ref[...] = y.astype(BF16)


def _conv(xp, w, b, tt):
    B, Tp, C = xp.shape
    T = Tp - tt
    hb = tt // SUBLANES
    return pl.pallas_call(
        functools.partial(_conv_kernel, tt=tt),
        grid=(B, T // tt),
        in_specs=[pl.BlockSpec((None, tt, C), lambda b, i: (b, i + 1, 0)),
                  pl.BlockSpec((None, SUBLANES, C), lambda b, i: (b, hb * (i + 1) - 1, 0)),
                  pl.BlockSpec((CONV_W, C), lambda b, i: (0, 0)),
                  pl.BlockSpec((1, C), lambda b, i: (0, 0))],
        out_specs=[pl.BlockSpec((None, tt, C), lambda b, i: (b, i, 0)),
                   pl.BlockSpec((None, tt, C), lambda b, i: (b, i, 0))],
        out_shape=[jax.ShapeDtypeStruct((B, T, C), F32),
                   jax.ShapeDtypeStruct((B, T, C), BF16)],
        scratch_shapes=[pltpu.VMEM((tt + SUBLANES, C), F32)],
        compiler_params=_params(("parallel", "parallel"), 2 * tt * C * 10 + tt * C * 12),
        name="causal_conv",
    )(xp, xp, w, b.reshape(1, C))


def _scan_kernel(a_ref, u_ref, g_ref, h0_ref, y_ref, hlast_ref, h_ref, hs_ref, *, tt):
    i = pl.program_id(1)

    @pl.when(i == 0)
    def _():
        h_ref[...] = h0_ref[...]

    def step(t, h):
        h = a_ref[pl.ds(t, 1), :] * h + u_ref[pl.ds(t, 1), :]
        hs_ref[pl.ds(t, 1), :] = h
        return h

    h = lax.fori_loop(0, tt, step, h_ref[...])
    h_ref[...] = h
    hlast_ref[...] = h
    y_ref[...] = (hs_ref[...] * jax.nn.gelu(g_ref[...])).astype(y_ref.dtype)


def _scan(a, u, g, h0, tt):
    B, T, C = a.shape
    blk = pl.BlockSpec((None, tt, C), lambda b, i: (b, i, 0))
    vec = pl.BlockSpec((None, 1, C), lambda b, i: (b, 0, 0))
    return pl.pallas_call(
        functools.partial(_scan_kernel, tt=tt),
        grid=(B, T // tt),
        in_specs=[blk, blk, blk, vec],
        out_specs=[blk, vec],
        out_shape=[jax.ShapeDtypeStruct((B, T, C), BF16),
                   jax.ShapeDtypeStruct((B, 1, C), F32)],
        scratch_shapes=[pltpu.VMEM((1, C), F32), pltpu.VMEM((tt, C), F32)],
        compiler_params=_params(("parallel", "arbitrary"), 2 * tt * C * 14 + tt * C * 12),
        name="rglru_scan",
    )(a, u, g, h0)


def _blockdiag_tiles(w, tn):
    nblk, rb, _ = w.shape
    C = nblk * rb
    sb = math.lcm(rb, tn)
    assert C % sb == 0 and sb % LANES == 0
    dense = jax.scipy.linalg.block_diag(*[w[i] for i in range(nblk)])
    tiles = [dense[(j * tn // sb) * sb:(j * tn // sb + 1) * sb, j * tn:(j + 1) * tn]
             for j in range(C // tn)]
    return jnp.concatenate(tiles, axis=0).astype(BF16), sb


def _pick(n, cands):
    for c in cands:
        if n % c == 0:
            return c
    raise ValueError(f"no tile for {n}")


def _layer(x, hist, p, rel_bias, dims):
    Tp, Bs, Ts, past = dims
    M, D = x.shape
    Ms = Bs * Ts
    k_past, v_past, ki_past, conv_buf, h0 = hist
    KV = k_past.shape[2]
    HQ = KV * GROUP * HEAD_DIM
    KVD = KV * HEAD_DIM
    DI = ki_past.shape[-1]
    C = p["conv_w"].shape[-1]
    n_in = p["w_in"].shape[1]
    HI = (n_in - HQ - 2 * KVD - DI - 2 * C - 2 * D) // (DI + 1)
    assert DI == LANES
    sizes = (HQ, KVD, KVD, HI * DI, DI, HI, C, C, D, D)
    offs = np.concatenate([[0], np.cumsum(sizes)])
    assert offs[-1] == n_in
    w_in = p["w_in"]

    def wslice(i, j=None):
        j = i if j is None else j
        return w_in[:, offs[i]:offs[j + 1]].astype(BF16)

    tm = _pick(M, (512, 256, 128, 64))
    TK = KEY_TILE

    h = _rmsnorm(x, p["norm_mix"], tm)

    def proj(w, ep, dt, tn, vecs=(), name="in_proj"):
        return _matmul(h, [w], ep, [dt], tm=tm, tn=tn, vecs=vecs, name=name)[0]

    q = proj(wslice(0), _ep_headnorm(HEAD_DIM ** -0.5), BF16, _pick(HQ, (512, 256, 128)),
             vecs=[jnp.tile(p["q_norm"], HQ // HEAD_DIM)], name="in_proj_q")
    k = proj(wslice(1), _ep_headnorm(1.0), F32, _pick(KVD, (512, 256, 128)),
             vecs=[jnp.tile(p["k_norm"], KV)], name="in_proj_k")
    v = proj(wslice(2), _ep_identity, F32, _pick(KVD, (512, 256, 128)), name="in_proj_v")
    qi = proj(wslice(3), _ep_identity, BF16, _pick(HI * DI, (512, 256, 128)), name="in_proj_qi")
    kw_w = jnp.pad(wslice(4, 5), ((0, 0), (0, 2 * LANES - DI - HI)))
    kiwi = proj(kw_w, _ep_identity, F32, 2 * LANES, name="in_proj_ki")
    ki, wi = kiwi[:, :DI], kiwi[:, DI:DI + HI]
    tn_c = _pick(C, (768, 384, 128))
    xr = proj(wslice(6), _ep_identity, F32, tn_c, name="in_proj_xr")
    gr = proj(wslice(7), _ep_identity, F32, tn_c, name="in_proj_gr")
    tn_d = _pick(D, (512, 256, 128))
    ga = proj(wslice(8), _ep_identity, F32, tn_d, name="in_proj_ga")
    gb = proj(wslice(9), _ep_identity, F32, tn_d, name="in_proj_gb")

    k_bf, v_bf, ki_bf = k.astype(BF16), v.astype(BF16), ki.astype(BF16)

    nt_p = -(-Tp // TK)
    pad_p = nt_p * TK - Tp
    padrows = lambda a, n: jnp.pad(a, ((0, n), (0, 0))) if n else a
    n_sel_p = min(TOPK_MAX, Tp // 4)
    mask_p = _select(qi[:Tp], wi[:Tp], padrows(ki_bf[:Tp], pad_p), TQ=CHUNK, NT=nt_p,
                     lend_a=CHUNK, lend_b=CHUNK, n_sel=n_sel_p, batched=False)
    phases_p = [CHUNK * (i + 1) for i in range(TK // CHUNK)]
    nb_p = _near_bias(rel_bias, CHUNK, TK, phases_p)
    o_p = _attention(q[:Tp], padrows(k_bf[:Tp], pad_p), padrows(v_bf[:Tp], pad_p), mask_p, nb_p,
                     TQ=CHUNK, lend_a=CHUNK, lend_b=CHUNK, batched=False)

    Ls = past + Ts
    nt_s = -(-Ls // TK)
    pad_s = nt_s * TK - Ls

    def with_cache(cache, new):
        new = new[Tp:].reshape(Bs, Ts, -1)
        parts = [cache.reshape(Bs, past, -1).astype(BF16), new]
        if pad_s:
            parts.append(jnp.zeros((Bs, pad_s, new.shape[-1]), BF16))
        return jnp.concatenate(parts, axis=1)

    n_sel_s = min(TOPK_MAX, Ls // 4)
    mask_s = _select(qi[Tp:], wi[Tp:], with_cache(ki_past, ki_bf), TQ=Ts, NT=nt_s,
                     lend_a=0, lend_b=Ls, n_sel=n_sel_s, batched=True)
    nb_s = _near_bias(rel_bias, Ts, TK, [Ls - ((Ls - 1) // TK) * TK])
    o_s = _attention(q[Tp:], with_cache(k_past, k_bf), with_cache(v_past, v_bf), mask_s, nb_s,
                     TQ=Ts, lend_a=0, lend_b=Ls, batched=True)
    o_a = jnp.concatenate([o_p, o_s], axis=0)

    wa_t, sb = _blockdiag_tiles(p["rg_wa"], 384)
    wx_t, _ = _blockdiag_tiles(p["rg_wx"], 384)
    per_sb = sb // 384

    def griffin(xr_b, gr_b, buf, h_init):
        B, T, _ = xr_b.shape
        tt = _pick(T, (128, 64, 32, 16, 8))
        front = jnp.zeros((B, tt - (CONV_W - 1), C), F32)
        xp = jnp.concatenate([front, buf.astype(F32), xr_b], axis=1)
        xc, xcb = _conv(xp, p["conv_w"], p["conv_b"], tt)
        rows = B * T
        tmr = _pick(rows, (512, 256, 128, 64, 32))
        a, u = _matmul(xcb.reshape(rows, C), [wa_t, wx_t], _ep_rglru_gates, [F32, F32],
                       tm=tmr, tn=384, tk=sb, n=C, exts=[xc.reshape(rows, C)],
                       vecs=[p["rg_ba"], p["rg_bx"], p["rg_lambda"]],
                       x_map=lambda j, i, k: (i, j // per_sb), w_map=lambda j, i, k: (j, 0),
                       name="rglru_gates")
        y, h_last = _scan(a.reshape(B, T, C), u.reshape(B, T, C), gr_b, h_init.reshape(B, 1, C), tt)
        return y.reshape(rows, C), xp[:, -(CONV_W - 1):], h_last.reshape(B, C)

    y_p, conv_p, h_p = griffin(xr[:Tp].reshape(1, Tp, C), gr[:Tp].reshape(1, Tp, C),
                               jnp.zeros((1, CONV_W - 1, C), F32), jnp.zeros((1, C), F32))
    y_s, conv_s, h_s = griffin(xr[Tp:].reshape(Bs, Ts, C), gr[Tp:].reshape(Bs, Ts, C), conv_buf, h0)
    o_b = jnp.concatenate([y_p, y_s], axis=0)

    part = _matmul(o_a, [p["w_out_attn"].astype(BF16)], _ep_gate, [F32], tm=tm, tn=tn_d,
                   exts=[ga], name="out_attn")[0]
    tk_c = _pick(C, (2688, 1792, 1344, 896, 128))
    merged = _matmul(o_b, [p["w_out_rg"].astype(BF16)], _ep_gate_add, [BF16], tm=tm, tn=tn_d,
                     tk=tk_c, exts=[gb, part], name="out_rg")[0]
    x1 = _matmul(merged, [p["w_o"].astype(BF16)], _ep_residual, [F32], tm=tm, tn=tn_d,
                 exts=[x], name="w_o")[0]
    hf = _rmsnorm(x1, p["norm_ffn"], tm)
    FF = p["ffn_w1"].shape[1]
    tn_f = _pick(FF, (512, 256, 128))
    act = _matmul(hf, [p["ffn_w1"].astype(BF16), p["ffn_w3"].astype(BF16)], _ep_swiglu, [BF16],
                  tm=tm, tn=tn_f, name="ffn_up")[0]
    tk_f = _pick(FF, (5504, 2816, 2048, 1024, 512, 256, 128))
    x2 = _matmul(act, [p["ffn_w2"].astype(BF16)], _ep_residual, [F32], tm=tm, tn=tn_d,
                 tk=tk_f, exts=[x1], name="ffn_down")[0]

    new_p = (k[:Tp].reshape(1, Tp, KV, HEAD_DIM), v[:Tp].reshape(1, Tp, KV, HEAD_DIM),
             ki[:Tp].reshape(1, Tp, DI), conv_p, h_p)
    new_s = (k[Tp:].reshape(Bs, Ts, KV, HEAD_DIM), v[Tp:].reshape(Bs, Ts, KV, HEAD_DIM),
             ki[Tp:].reshape(Bs, Ts, DI), conv_s, h_s)
    return x2, new_p, new_s


def kernel(x_prompt, x_sample, cache_k, cache_v, cache_kidx, state_conv, state_rglru, norm_mix, w_in, q_norm, k_norm, rel_bias, conv_w, conv_b, rg_wa, rg_ba, rg_wx, rg_bx, rg_lambda, w_out_attn, w_out_rg, w_o, norm_ffn, ffn_w1, ffn_w3, ffn_w2):
    Bp, Tp, D = x_prompt.shape
    Bs, Ts, _ = x_sample.shape
    assert Bp == 1 and Tp % CHUNK == 0
    depth = w_in.shape[0]
    past = cache_k.shape[2]
    x = jnp.concatenate([x_prompt.reshape(Tp, D), x_sample.reshape(Bs * Ts, D)], axis=0)
    outs_p, outs_s = [], []
    for l in range(depth):
        p = dict(norm_mix=norm_mix[l], w_in=w_in[l], q_norm=q_norm[l], k_norm=k_norm[l],
                 conv_w=conv_w[l], conv_b=conv_b[l], rg_wa=rg_wa[l], rg_ba=rg_ba[l],
                 rg_wx=rg_wx[l], rg_bx=rg_bx[l], rg_lambda=rg_lambda[l],
                 w_out_attn=w_out_attn[l], w_out_rg=w_out_rg[l], w_o=w_o[l],
                 norm_ffn=norm_ffn[l], ffn_w1=ffn_w1[l], ffn_w3=ffn_w3[l], ffn_w2=ffn_w2[l])
        hist = (cache_k[l], cache_v[l], cache_kidx[l], state_conv[l], state_rglru[l])
        x, new_p, new_s = _layer(x, hist, p, rel_bias, (Tp, Bs, Ts, past))
        outs_p.append(new_p)
        outs_s.append(new_s)
    stack = lambda outs, i: jnp.stack([o[i] for o in outs])
    return (x[:Tp].reshape(1, Tp, D), x[Tp:].reshape(Bs, Ts, D),
            *[stack(outs_p, i) for i in range(5)],
            *[stack(outs_s, i) for i in range(5)])
```

```python
import functools
import math

import numpy as np
import jax
import jax.numpy as jnp
from jax import lax
from jax.experimental import pallas as pl
from jax.experimental.pallas import tpu as pltpu

F32 = jnp.float32
BF16 = jnp.bfloat16

CHUNK = 64
HEAD_DIM = 128
GROUP = 4
TOPK_MAX = 256
N_BUCKETS = 32
REL_MAX_DIST = 128
RG_C = 8.0
CONV_W = 4
EPS = 1e-6

LANES = 128
SUBLANES = 8
KEY_TILE = 256
Q_STEP = 2 * CHUNK
FAR_TILE = 512
FRONT_PAD = FAR_TILE
LOG2E = math.log2(math.e)
VMEM_CAP = 56 << 20
NEG = -1e30


def _params(sem, vmem_bytes):
    limit = min(max(int(vmem_bytes) + (6 << 20), 24 << 20), VMEM_CAP)
    return pltpu.CompilerParams(dimension_semantics=sem, vmem_limit_bytes=limit)


def _rmsnorm_kernel(x_ref, g_ref, o_ref):
    x = x_ref[...]
    ms = jnp.mean(x * x, axis=-1, keepdims=True)
    o_ref[...] = (x * lax.rsqrt(ms + EPS) * g_ref[...]).astype(o_ref.dtype)


def _rmsnorm(x, g, tm):
    M, D = x.shape
    tm = min(tm, 256)
    return pl.pallas_call(
        _rmsnorm_kernel,
        grid=(M // tm,),
        in_specs=[pl.BlockSpec((tm, D), lambda i: (i, 0)),
                  pl.BlockSpec((1, D), lambda i: (0, 0))],
        out_specs=pl.BlockSpec((tm, D), lambda i: (i, 0)),
        out_shape=jax.ShapeDtypeStruct((M, D), BF16),
        compiler_params=_params(("parallel",), 2 * tm * D * 6 + 2 * tm * D * 4),
        name="rmsnorm",
    )(x, g.reshape(1, D))


def _mm_kernel(*refs, n_w, n_ext, n_vec, n_out, nk, epilogue):
    x_ref = refs[0]
    w_refs = refs[1:1 + n_w]
    p = 1 + n_w
    ext_refs = refs[p:p + n_ext]
    p += n_ext
    vec_refs = refs[p:p + n_vec]
    p += n_vec
    out_refs = refs[p:p + n_out]
    acc_refs = refs[p + n_out:]

    def finish(accs):
        res = epilogue(accs, [e[...] for e in ext_refs], [v[...] for v in vec_refs])
        for o, r in zip(out_refs, res):
            o[...] = r.astype(o.dtype)

    if nk == 1:
        finish([jnp.dot(x_ref[...], w[...], preferred_element_type=F32) for w in w_refs])
        return

    k = pl.program_id(2)

    @pl.when(k == 0)
    def _():
        for a in acc_refs:
            a[...] = jnp.zeros_like(a)

    for a, w in zip(acc_refs, w_refs):
        a[...] += jnp.dot(x_ref[...], w[...], preferred_element_type=F32)

    @pl.when(k == nk - 1)
    def _():
        finish([a[...] for a in acc_refs])


def _matmul(x, ws, epilogue, out_dtypes, *, tm, tn, tk=None, n=None, exts=(), vecs=(),
            x_map=None, w_map=None, name="matmul"):
    M = x.shape[0]
    n = ws[0].shape[1] if n is None else n
    tk = x.shape[1] if tk is None else tk
    nk = (x.shape[1] // tk) if x_map is None else 1
    assert M % tm == 0 and n % tn == 0
    x_map = x_map or (lambda j, i, k: (i, k))
    w_map = w_map or (lambda j, i, k: (k, j))
    mn_map = lambda j, i, k: (i, j)
    in_specs = [pl.BlockSpec((tm, tk), x_map)]
    in_specs += [pl.BlockSpec((tk, tn), w_map) for _ in ws]
    in_specs += [pl.BlockSpec((tm, tn), mn_map) for _ in exts]
    in_specs += [pl.BlockSpec((1, tn), lambda j, i, k: (0, j)) for _ in vecs]
    out_specs = [pl.BlockSpec((tm, tn), mn_map) for _ in out_dtypes]
    out_shape = [jax.ShapeDtypeStruct((M, n), dt) for dt in out_dtypes]
    scratch = [pltpu.VMEM((tm, tn), F32) for _ in ws] if nk > 1 else []
    vmem = (2 * (tm * tk * 2 + len(ws) * tk * tn * 2 + (len(exts) + len(out_dtypes)) * tm * tn * 4)
            + 3 * len(ws) * tm * tn * 4)
    kern = functools.partial(_mm_kernel, n_w=len(ws), n_ext=len(exts), n_vec=len(vecs),
                             n_out=len(out_dtypes), nk=nk, epilogue=epilogue)
    outs = pl.pallas_call(
        kern,
        grid=(n // tn, M // tm, nk),
        in_specs=in_specs,
        out_specs=out_specs,
        out_shape=out_shape,
        scratch_shapes=scratch,
        compiler_params=_params(("parallel", "parallel", "arbitrary"), vmem),
        name=name,
    )(x, *ws, *exts, *[v.reshape(1, -1) for v in vecs])
    return outs


def _ep_identity(accs, exts, vecs):
    return (accs[0],)


def _ep_headnorm(scale):
    def ep(accs, exts, vecs):
        a, g = accs[0], vecs[0]
        outs = []
        for j in range(a.shape[1] // HEAD_DIM):
            aj = a[:, j * HEAD_DIM:(j + 1) * HEAD_DIM]
            ms = jnp.mean(aj * aj, axis=-1, keepdims=True)
            outs.append(aj * lax.rsqrt(ms + EPS) * g[:, j * HEAD_DIM:(j + 1) * HEAD_DIM])
        y = jnp.concatenate(outs, axis=1)
        return (y * scale if scale != 1.0 else y,)
    return ep


def _ep_gate(accs, exts, vecs):
    return (jax.nn.sigmoid(exts[0]) * accs[0],)


def _ep_gate_add(accs, exts, vecs):
    return (exts[1] + jax.nn.sigmoid(exts[0]) * accs[0],)


def _ep_residual(accs, exts, vecs):
    return (exts[0] + accs[0],)


def _ep_swiglu(accs, exts, vecs):
    return (jax.nn.silu(accs[0]) * accs[1],)


def _softplus(x):
    return jnp.maximum(x, 0.0) + jnp.log1p(jnp.exp(-jnp.abs(x)))


def _ep_rglru_gates(accs, exts, vecs):
    xc = exts[0]
    r = jax.nn.sigmoid(accs[0] + vecs[0])
    i = jax.nn.sigmoid(accs[1] + vecs[1])
    log_a = -RG_C * r * _softplus(-vecs[2])
    a = jnp.exp(log_a)
    u = jnp.sqrt(1.0 - a * a) * (i * xc)
    return (a, u)


def _select_kernel(qi_ref, wi_ref, ki_ref, mask_ref, q2_ref, w2_ref, key_ref, *,
                   TQ, TK, NT, HI, lend_a, lend_b, n_sel, wscale):
    nl = TK // LANES
    lend = lend_a * pl.program_id(0) + lend_b
    nt = (lend + TK - 1) // TK
    ksel = jnp.minimum(n_sel, lend).astype(F32)

    wi = wi_ref[...] * wscale
    for h in range(HI):
        q2_ref[h * TQ:(h + 1) * TQ, :] = qi_ref[:, h * LANES:(h + 1) * LANES]
        w2_ref[h * TQ:(h + 1) * TQ, :] = jnp.broadcast_to(wi[:, h:h + 1], (TQ, LANES))

    lane = lax.broadcasted_iota(jnp.int32, (TQ, TK), 1)
    int_min = jnp.int32(-2 ** 31)

    def to_key(s):
        b = pltpu.bitcast(s, jnp.int32)
        return b ^ ((b >> 31) & jnp.int32(0x7FFFFFFF))

    def score_tile(j, carry):
        smin, smax = carry
        start = pl.multiple_of(j * TK, TK)
        kt = ki_ref[pl.ds(start, TK), :]
        s = lax.dot_general(q2_ref[...], kt, (((1,), (1,)), ((), ())),
                            preferred_element_type=F32)
        cols = []
        for l in range(nl):
            acc = jnp.zeros((TQ, LANES), F32)
            for h in range(HI):
                acc = acc + (jnp.maximum(s[h * TQ:(h + 1) * TQ, l * LANES:(l + 1) * LANES], 0.0)
                             * w2_ref[h * TQ:(h + 1) * TQ, :])
            cols.append(acc)
        sc = jnp.concatenate(cols, axis=1)
        valid = (lane + j * TK) < lend
        key_ref[j] = jnp.where(valid, to_key(sc), int_min)
        lo_s = jnp.where(valid, sc, jnp.inf)
        hi_s = jnp.where(valid, sc, -jnp.inf)
        for l in range(nl):
            smin = jnp.minimum(smin, lo_s[:, l * LANES:(l + 1) * LANES])
            smax = jnp.maximum(smax, hi_s[:, l * LANES:(l + 1) * LANES])
        return smin, smax

    smin, smax = lax.fori_loop(
        0, nt, score_tile,
        (jnp.full((TQ, LANES), jnp.inf, F32), jnp.full((TQ, LANES), -jnp.inf, F32)))
    lo = to_key(jnp.broadcast_to(jnp.min(smin, axis=1, keepdims=True), (TQ, LANES)))
    hi = to_key(jnp.broadcast_to(jnp.max(smax, axis=1, keepdims=True), (TQ, LANES)))

    def count_ge(mid):
        def body(j, c):
            kk = key_ref[j]
            for l in range(nl):
                c = c + jnp.where(kk[:, l * LANES:(l + 1) * LANES] >= mid, 1.0, 0.0)
            return c
        c = lax.fori_loop(0, nt, body, jnp.zeros((TQ, LANES), F32))
        return jnp.broadcast_to(jnp.sum(c, axis=1, keepdims=True), (TQ, LANES))

    def open_rows(lo, hi):
        return jnp.max(jnp.where(lo < hi, 1.0, 0.0))

    def cond(st):
        return st[2] > 0.5

    def body(st):
        lo, hi, _ = st
        active = lo < hi
        mid = (lo >> 1) + (hi >> 1) + ((lo | hi) & 1)
        c = count_ge(mid)
        ge = c >= ksel
        lo_n = jnp.where(ge, mid, lo)
        hi_n = jnp.where(c == ksel, mid, jnp.where(ge, hi, mid - 1))
        lo = jnp.where(active, lo_n, lo)
        hi = jnp.where(active, hi_n, hi)
        return lo, hi, open_rows(lo, hi)

    tau, _, _ = lax.while_loop(cond, body, (lo, hi, open_rows(lo, hi)))
    tau_t = jnp.concatenate([tau] * nl, axis=1)

    def write(j, _):
        mask_ref[0, j] = jnp.where(key_ref[j] >= tau_t, 1, 0).astype(jnp.int8)
        return 0

    lax.fori_loop(0, nt, write, 0)

    def clear(j, _):
        mask_ref[0, j] = jnp.zeros((TQ, TK), jnp.int8)
        return 0

    lax.fori_loop(nt, NT, clear, 0)


def _select(qi, wi, ki, *, TQ, NT, lend_a, lend_b, n_sel, batched):
    TK = KEY_TILE
    Mq = qi.shape[0]
    HI = wi.shape[1]
    steps = Mq // TQ
    if batched:
        ki_spec = pl.BlockSpec((None, NT * TK, LANES), lambda i: (i, 0, 0))
    else:
        ki_spec = pl.BlockSpec((NT * TK, LANES), lambda i: (0, 0))
    kern = functools.partial(_select_kernel, TQ=TQ, TK=TK, NT=NT, HI=HI, lend_a=lend_a,
                             lend_b=lend_b, n_sel=n_sel,
                             wscale=float(HI ** -0.5 * LANES ** -0.5))
    vmem = (2 * (TQ * HI * LANES * 2 + NT * TK * LANES * 2 + NT * TQ * TK)
            + HI * TQ * LANES * 6 + NT * TQ * TK * 4 + 3 * HI * TQ * TK * 4)
    return pl.pallas_call(
        kern,
        grid=(steps,),
        in_specs=[pl.BlockSpec((TQ, HI * LANES), lambda i: (i, 0)),
                  pl.BlockSpec((TQ, HI), lambda i: (i, 0)),
                  ki_spec],
        out_specs=pl.BlockSpec((1, NT, TQ, TK), lambda i: (i, 0, 0, 0)),
        out_shape=jax.ShapeDtypeStruct((steps, NT, TQ, TK), jnp.int8),
        scratch_shapes=[pltpu.VMEM((HI * TQ, LANES), BF16),
                        pltpu.VMEM((HI * TQ, LANES), F32),
                        pltpu.VMEM((NT, TQ, TK), jnp.int32)],
        compiler_params=_params(("parallel",), vmem),
        name="dsa_select",
    )(qi, wi, ki)


def _select_prompt_kernel(qi_ref, wi_ref, ki_ref, mask_ref, q2_ref, w2_ref, key_ref, acc_ref, *,
                          HI, HG, NT, n_sel, wscale):
    TQ, TK = Q_STEP, FAR_TILE
    nl = TK // LANES
    pad_tiles = FRONT_PAD // LANES
    RB = 64
    c2 = pl.program_id(0)
    row = lax.broadcasted_iota(jnp.int32, (TQ, LANES), 0)
    lend = jnp.where(row < CHUNK, c2 * TQ + CHUNK, c2 * TQ + TQ)
    nt = (c2 * TQ + TQ + TK - 1) // TK
    ksel = jnp.minimum(n_sel, lend).astype(F32)

    wi = wi_ref[...] * wscale
    for h in range(HI):
        q2_ref[h * TQ:(h + 1) * TQ, :] = qi_ref[:, h * LANES:(h + 1) * LANES]
        w2_ref[h * TQ:(h + 1) * TQ, :] = jnp.broadcast_to(wi[:, h:h + 1], (TQ, LANES))

    lane = lax.broadcasted_iota(jnp.int32, (TQ, TK), 1)
    lend_t = jnp.concatenate([lend] * nl, axis=1)
    int_min = jnp.int32(-2 ** 31)

    def to_key(s):
        b = pltpu.bitcast(s, jnp.int32)
        return b ^ ((b >> 31) & jnp.int32(0x7FFFFFFF))

    def score_tile(j, carry):
        smin, smax = carry
        start = pl.multiple_of(j * TK, TK)
        kt = ki_ref[pl.ds(start, TK), :]
        for hg in range(HI // HG):
            s = lax.dot_general(q2_ref[hg * HG * TQ:(hg + 1) * HG * TQ, :], kt,
                                (((1,), (1,)), ((), ())), preferred_element_type=F32)
            for r in range(TQ // RB):
                cs = [None] * nl
                for h in range(HG):
                    r0 = h * TQ + r * RB
                    w = w2_ref[(hg * HG) * TQ + r0:(hg * HG) * TQ + r0 + RB, :]
                    for l in range(nl):
                        term = jnp.maximum(s[r0:r0 + RB, l * LANES:(l + 1) * LANES], 0.0) * w
                        cs[l] = term if cs[l] is None else cs[l] + term
                for l in range(nl):
                    if hg == 0:
                        acc_ref[r * RB:(r + 1) * RB, l * LANES:(l + 1) * LANES] = cs[l]
                    else:
                        acc_ref[r * RB:(r + 1) * RB, l * LANES:(l + 1) * LANES] += cs[l]
        sc = acc_ref[...]
        valid = (lane + j * TK) < lend_t
        key_ref[j] = jnp.where(valid, to_key(sc), int_min)
        lo_s = jnp.where(valid, sc, jnp.inf)
        hi_s = jnp.where(valid, sc, -jnp.inf)
        for l in range(nl):
            smin = jnp.minimum(smin, lo_s[:, l * LANES:(l + 1) * LANES])
            smax = jnp.maximum(smax, hi_s[:, l * LANES:(l + 1) * LANES])
        return smin, smax

    smin, smax = lax.fori_loop(
        0, nt, score_tile,
        (jnp.full((TQ, LANES), jnp.inf, F32), jnp.full((TQ, LANES), -jnp.inf, F32)))
    lo = to_key(jnp.broadcast_to(jnp.min(smin, axis=1, keepdims=True), (TQ, LANES)))
    hi = to_key(jnp.broadcast_to(jnp.max(smax, axis=1, keepdims=True), (TQ, LANES)))

    def count_ge(mid):
        def body(j, c):
            kk = key_ref[j]
            for l in range(nl):
                c = c + jnp.where(kk[:, l * LANES:(l + 1) * LANES] >= mid, 1.0, 0.0)
            return c
        c = lax.fori_loop(0, nt, body, jnp.zeros((TQ, LANES), F32))
        return jnp.broadcast_to(jnp.sum(c, axis=1, keepdims=True), (TQ, LANES))

    def open_rows(lo, hi):
        return jnp.max(jnp.where(lo < hi, 1.0, 0.0))

    def cond(st):
        return st[2] > 0.5

    def body(st):
        lo, hi, _ = st
        active = lo < hi
        mid = (lo >> 1) + (hi >> 1) + ((lo | hi) & 1)
        c = count_ge(mid)
        ge = c >= ksel
        lo_n = jnp.where(ge, mid, lo)
        hi_n = jnp.where(c == ksel, mid, jnp.where(ge, hi, mid - 1))
        lo = jnp.where(active, lo_n, lo)
        hi = jnp.where(active, hi_n, hi)
        return lo, hi, open_rows(lo, hi)

    tau, _, _ = lax.while_loop(cond, body, (lo, hi, open_rows(lo, hi)))

    neg_tile = jnp.full((TQ, LANES), NEG, BF16)
    for i in range(pad_tiles):
        mask_ref[0, i] = neg_tile

    def write(j, _):
        kk = key_ref[j]
        for i in range(nl):
            mask_ref[0, pad_tiles + nl * j + i] = jnp.where(
                kk[:, i * LANES:(i + 1) * LANES] >= tau, 0.0, NEG).astype(BF16)
        return 0

    lax.fori_loop(0, nt, write, 0)

    def clear(j, _):
        for i in range(nl):
            mask_ref[0, pad_tiles + nl * j + i] = neg_tile
        return 0

    lax.fori_loop(nt, NT, clear, 0)


def _select_prompt(qi, wi, ki, n_sel, T):
    TQ, TK = Q_STEP, FAR_TILE
    HI = wi.shape[1]
    HG = 4 if HI % 4 == 0 else 1
    NT = ki.shape[0] // TK
    steps = T // TQ
    ntile = (FRONT_PAD + NT * TK) // LANES
    kern = functools.partial(_select_prompt_kernel, HI=HI, HG=HG, NT=NT, n_sel=n_sel,
                             wscale=float(HI ** -0.5 * LANES ** -0.5))
    vmem = (2 * (TQ * HI * LANES * 2 + NT * TK * LANES * 2 + ntile * TQ * LANES * 2)
            + HI * TQ * LANES * 6 + NT * TQ * TK * 4 + TQ * TK * 4 + 4 * HG * TQ * TK * 4)
    return pl.pallas_call(
        kern,
        grid=(steps,),
        in_specs=[pl.BlockSpec((TQ, HI * LANES), lambda i: (i, 0)),
                  pl.BlockSpec((TQ, HI), lambda i: (i, 0)),
                  pl.BlockSpec((NT * TK, LANES), lambda i: (0, 0))],
        out_specs=pl.BlockSpec((1, ntile, TQ, LANES), lambda i: (i, 0, 0, 0)),
        out_shape=jax.ShapeDtypeStruct((steps, ntile, TQ, LANES), BF16),
        scratch_shapes=[pltpu.VMEM((HI * TQ, LANES), BF16),
                        pltpu.VMEM((HI * TQ, LANES), F32),
                        pltpu.VMEM((NT, TQ, TK), jnp.int32),
                        pltpu.VMEM((TQ, TK), F32)],
        compiler_params=_params(("parallel",), vmem),
        name="dsa_select_prompt",
    )(qi, wi, ki)


def _attn_kernel(q_ref, k_ref, v_ref, mask_ref, nb_ref, o_ref, q2_ref, m_ref, l_ref, acc_ref, *,
                 TQ, TK, G, lend_a, lend_b, step_axis):
    nl = TK // LANES
    R = G * TQ
    lend = lend_a * pl.program_id(step_axis) + lend_b
    jl = (lend - 1) // TK

    for g in range(G):
        q2_ref[g * TQ:(g + 1) * TQ, :] = q_ref[:, g * HEAD_DIM:(g + 1) * HEAD_DIM]
    m_ref[...] = jnp.full((R, LANES), NEG, F32)
    l_ref[...] = jnp.zeros((R, LANES), F32)
    acc_ref[...] = jnp.zeros((R, HEAD_DIM), F32)

    def tile(j, half):
        start = pl.multiple_of(j * TK, TK)
        kt = k_ref[pl.ds(start, TK), :]
        vt = v_ref[pl.ds(start, TK), :]
        s = lax.dot_general(q2_ref[...], kt, (((1,), (1,)), ((), ())),
                            preferred_element_type=F32)
        madd = jnp.where(mask_ref[0, j].astype(jnp.int32) != 0, 0.0, NEG)
        rows = []
        for g in range(G):
            sg = s[g * TQ:(g + 1) * TQ, :]
            if half is not None:
                sg = sg + nb_ref[0, g, :, half * TK:(half + 1) * TK]
            rows.append(sg + madd)
        s = jnp.concatenate(rows, axis=0)
        m_old = m_ref[...]
        m_cur = s[:, :LANES]
        for l in range(1, nl):
            m_cur = jnp.maximum(m_cur, s[:, l * LANES:(l + 1) * LANES])
        m_new = jnp.maximum(m_old, jnp.broadcast_to(jnp.max(m_cur, axis=1, keepdims=True), (R, LANES)))
        alpha = jnp.exp2(m_old - m_new)
        p = jnp.exp2(s - jnp.concatenate([m_new] * nl, axis=1))
        psum = p[:, :LANES]
        for l in range(1, nl):
            psum = psum + p[:, l * LANES:(l + 1) * LANES]
        l_ref[...] = alpha * l_ref[...] + jnp.broadcast_to(
            jnp.sum(psum, axis=1, keepdims=True), (R, LANES))
        acc_ref[...] = alpha * acc_ref[...] + jnp.dot(
            p.astype(BF16), vt, preferred_element_type=F32)
        m_ref[...] = m_new

    def far(j, _):
        tile(j, None)
        return 0

    lax.fori_loop(0, jnp.maximum(jl - 1, 0), far, 0)

    @pl.when(jl >= 1)
    def _():
        tile(jl - 1, 0)

    tile(jl, 1)

    o = acc_ref[...] / l_ref[...]
    for g in range(G):
        o_ref[:, g * HEAD_DIM:(g + 1) * HEAD_DIM] = o[g * TQ:(g + 1) * TQ, :].astype(o_ref.dtype)


def _attention(q, k, v, mask, nb, *, TQ, lend_a, lend_b, batched):
    TK = KEY_TILE
    G = GROUP
    Mq, HD = q.shape
    KV = HD // (G * HEAD_DIM)
    steps = Mq // TQ
    NT = mask.shape[1]
    L = k.shape[-2]
    P = nb.shape[0]
    if batched:
        grid = (steps, KV)
        q_map = lambda b, h: (b, h)
        kv_spec = pl.BlockSpec((None, L, HEAD_DIM), lambda b, h: (b, 0, h))
        mask_map = lambda b, h: (b, 0, 0, 0)
        nb_map = lambda b, h: (0, h, 0, 0)
        step_axis = 0
    else:
        grid = (KV, steps)
        q_map = lambda h, c: (c, h)
        kv_spec = pl.BlockSpec((L, HEAD_DIM), lambda h, c: (0, h))
        mask_map = lambda h, c: (c, 0, 0, 0)
        nb_map = lambda h, c: (c % P, h, 0, 0)
        step_axis = 1
    kern = functools.partial(_attn_kernel, TQ=TQ, TK=TK, G=G, lend_a=lend_a, lend_b=lend_b,
                             step_axis=step_axis)
    R = G * TQ
    vmem = (2 * (2 * TQ * G * HEAD_DIM * 2 + 2 * L * HEAD_DIM * 2 + NT * TQ * TK + G * TQ * 2 * TK * 4)
            + R * LANES * 14 + 6 * R * TK * 4)
    return pl.pallas_call(
        kern,
        grid=grid,
        in_specs=[pl.BlockSpec((TQ, G * HEAD_DIM), q_map),
                  kv_spec, kv_spec,
                  pl.BlockSpec((1, NT, TQ, TK), mask_map),
                  pl.BlockSpec((1, G, TQ, 2 * TK), nb_map)],
        out_specs=pl.BlockSpec((TQ, G * HEAD_DIM), q_map),
        out_shape=jax.ShapeDtypeStruct((Mq, HD), BF16),
        scratch_shapes=[pltpu.VMEM((R, HEAD_DIM), BF16),
                        pltpu.VMEM((R, LANES), F32),
                        pltpu.VMEM((R, LANES), F32),
                        pltpu.VMEM((R, HEAD_DIM), F32)],
        compiler_params=_params(("parallel", "arbitrary"), vmem),
        name="dsa_attention",
    )(q, k, v, mask, nb)


def _attn_prompt_kernel(q_ref, k_ref, v_ref, mask_ref, nb_ref, o_ref,
                        q2_ref, m_ref, l_ref, acc_ref, sa_ref, sb_ref):
    TQ, TK, G = Q_STEP, FAR_TILE, GROUP
    R = G * TQ
    NW = 2 * TQ
    c2 = pl.program_id(1)
    far_len = TQ * (c2 - 1)
    nfar = jnp.maximum((far_len + TK - 1) // TK, 0)
    nt_dims = (((1,), (1,)), ((), ()))

    for g in range(G):
        q2_ref[g * TQ:(g + 1) * TQ, :] = q_ref[:, g * HEAD_DIM:(g + 1) * HEAD_DIM]
    m_ref[...] = jnp.full((R, LANES), NEG, F32)
    l_ref[...] = jnp.zeros((R, LANES), F32)
    acc_ref[...] = jnp.zeros((R, HEAD_DIM), F32)

    def far_start(j):
        return pl.multiple_of(jnp.maximum(far_len - TK * j, 0), LANES)

    def logits(j, dst):
        dst[...] = lax.dot_general(q2_ref[...], k_ref[pl.ds(far_start(j), TK), :], nt_dims,
                                   preferred_element_type=F32)

    def update(s, start, width, biased):
        nl = width // LANES
        t0 = start // LANES
        madd = jnp.concatenate([mask_ref[0, t0 + i].astype(F32) for i in range(nl)], axis=1)
        rows = []
        for g in range(G):
            sg = s[g * TQ:(g + 1) * TQ, :] + madd
            if biased:
                sg = sg + nb_ref[g]
            rows.append(sg)
        s = jnp.concatenate(rows, axis=0)
        m_old = m_ref[...]
        m_cur = s[:, :LANES]
        for i in range(1, nl):
            m_cur = jnp.maximum(m_cur, s[:, i * LANES:(i + 1) * LANES])
        m_new = jnp.maximum(m_old, jnp.broadcast_to(jnp.max(m_cur, axis=1, keepdims=True), (R, LANES)))
        alpha = jnp.exp2(m_old - m_new)
        p = jnp.exp2(s - jnp.concatenate([m_new] * nl, axis=1))
        psum = p[:, :LANES]
        for i in range(1, nl):
            psum = psum + p[:, i * LANES:(i + 1) * LANES]
        l_ref[...] = alpha * l_ref[...] + jnp.broadcast_to(
            jnp.sum(psum, axis=1, keepdims=True), (R, LANES))
        acc_ref[...] = alpha * acc_ref[...] + jnp.dot(
            p.astype(BF16), v_ref[pl.ds(start, width), :], preferred_element_type=F32)
        m_ref[...] = m_new

    logits(0, sa_ref)

    def pair(i, _):
        logits(2 * i + 1, sb_ref)
        update(sa_ref[...], far_start(2 * i), TK, False)
        logits(2 * i + 2, sa_ref)
        update(sb_ref[...], far_start(2 * i + 1), TK, False)
        return 0

    lax.fori_loop(0, (nfar + 1) // 2, pair, 0)

    near = pl.multiple_of(TQ * c2 + FRONT_PAD - TQ, LANES)
    s_near = lax.dot_general(q2_ref[...], k_ref[pl.ds(near, NW), :], nt_dims,
                             preferred_element_type=F32)
    update(s_near, near, NW, True)

    o = acc_ref[...] / l_ref[...]
    for g in range(G):
        o_ref[:, g * HEAD_DIM:(g + 1) * HEAD_DIM] = o[g * TQ:(g + 1) * TQ, :].astype(o_ref.dtype)


def _attention_prompt(q, k, v, mask, nb):
    TQ, TK, G = Q_STEP, FAR_TILE, GROUP
    HD = q.shape[1]
    KV = HD // (G * HEAD_DIM)
    steps = mask.shape[0]
    T = steps * TQ
    ntile = mask.shape[1]
    Lp = k.shape[0]
    assert Lp == ntile * LANES and Lp >= FRONT_PAD + T
    R = G * TQ
    vmem = (2 * (2 * TQ * G * HEAD_DIM * 2 + 2 * Lp * HEAD_DIM * 2 + ntile * TQ * LANES * 2
                 + G * TQ * 2 * TQ * 4)
            + R * LANES * 14 + 2 * R * TK * 4 + 5 * R * TK * 4)
    q_map = lambda h, c: (c, h)
    kv_spec = pl.BlockSpec((Lp, HEAD_DIM), lambda h, c: (0, h))
    return pl.pallas_call(
        _attn_prompt_kernel,
        grid=(KV, steps),
        in_specs=[pl.BlockSpec((TQ, G * HEAD_DIM), q_map),
                  kv_spec, kv_spec,
                  pl.BlockSpec((1, ntile, TQ, LANES), lambda h, c: (c, 0, 0, 0)),
                  pl.BlockSpec((G, TQ, 2 * TQ), lambda h, c: (h, 0, 0))],
        out_specs=pl.BlockSpec((TQ, G * HEAD_DIM), q_map),
        out_shape=jax.ShapeDtypeStruct((T, HD), BF16),
        scratch_shapes=[pltpu.VMEM((R, HEAD_DIM), BF16),
                        pltpu.VMEM((R, LANES), F32),
                        pltpu.VMEM((R, LANES), F32),
                        pltpu.VMEM((R, HEAD_DIM), F32),
                        pltpu.VMEM((R, TK), F32),
                        pltpu.VMEM((R, TK), F32)],
        compiler_params=_params(("parallel", "arbitrary"), vmem),
        name="dsa_attention_prompt",
    )(q, k, v, mask, nb)


def _t5_bucket_np(rel):
    nb = N_BUCKETS // 2
    max_exact = nb // 2
    side = np.where(rel > 0, nb, 0)
    n = np.abs(rel)
    nf = np.maximum(n, 1).astype(np.float32)
    large = max_exact + (np.log(nf / np.float32(max_exact))
                         / np.float32(math.log(REL_MAX_DIST / max_exact))
                         * np.float32(nb - max_exact)).astype(np.int32)
    large = np.minimum(large, nb - 1)
    return side + np.where(n < max_exact, n, large)


def _near_bias(rel_bias, rel, far_rel):
    far = int(_t5_bucket_np(np.array(far_rel)))
    assert far_rel < 0 and far == int(_t5_bucket_np(np.array(-10 ** 6)))
    rb = rel_bias.astype(F32) * LOG2E
    tab = rb[_t5_bucket_np(rel)] - rb[far][None, None, None, :]
    return tab.transpose(0, 3, 1, 2)


def _near_bias_stream(rel_bias, TQ, TK, phases):
    t = np.arange(TQ)[:, None]
    j = np.arange(2 * TK)[None, :]
    rel = np.stack([j - TK - ph + TQ - t for ph in phases])
    return _near_bias(rel_bias, rel, TQ - TK - 2)


def _near_bias_prompt(rel_bias):
    TQ = Q_STEP
    rel = np.arange(2 * TQ)[None, :] - TQ - np.arange(TQ)[:, None]
    return _near_bias(rel_bias, rel[None], -TQ - 1)[0]


def _conv_kernel(x_ref, halo_ref, buf_ref, w_ref, b_ref, xc_ref, xcb_ref, ext_ref, *, tt):
    first = pl.program_id(1) == 0
    ext_ref[0:SUBLANES, :] = jnp.where(first, buf_ref[...], halo_ref[...])
    ext_ref[SUBLANES:, :] = x_ref[...]
    y = jnp.broadcast_to(b_ref[...], x_ref.shape)
    for j in range(CONV_W):
        off = SUBLANES - (CONV_W - 1) + j
        y = y + ext_ref[off:off + tt, :] * w_ref[j:j + 1, :]
    xc_ref[...] = y
    xcb_ref[...] = y.astype(BF16)


def _conv(x, buf8, w, b, *, row0, B, T, tt):
    C = x.shape[1]
    assert row0 % tt == 0 and T % tt == 0 and tt % SUBLANES == 0
    nt = T // tt
    hb = tt // SUBLANES
    blk0 = row0 // tt
    main_map = lambda b, i: (blk0 + b * nt + i, 0)
    halo_map = lambda b, i: (jnp.maximum((blk0 + b * nt + i) * hb - 1, 0), 0)
    out_map = lambda b, i: (b * nt + i, 0)
    return pl.pallas_call(
        functools.partial(_conv_kernel, tt=tt),
        grid=(B, nt),
        in_specs=[pl.BlockSpec((tt, C), main_map),
                  pl.BlockSpec((SUBLANES, C), halo_map),
                  pl.BlockSpec((None, SUBLANES, C), lambda b, i: (b, 0, 0)),
                  pl.BlockSpec((CONV_W, C), lambda b, i: (0, 0)),
                  pl.BlockSpec((1, C), lambda b, i: (0, 0))],
        out_specs=[pl.BlockSpec((tt, C), out_map), pl.BlockSpec((tt, C), out_map)],
        out_shape=[jax.ShapeDtypeStruct((B * T, C), F32),
                   jax.ShapeDtypeStruct((B * T, C), BF16)],
        scratch_shapes=[pltpu.VMEM((tt + SUBLANES, C), F32)],
        compiler_params=_params(("parallel", "parallel"), 2 * tt * C * 10 + tt * C * 12),
        name="causal_conv",
    )(x, x, buf8, w, b.reshape(1, C))


def _scan_kernel(a_ref, u_ref, g_ref, h0_ref, y_ref, hlast_ref, h_ref, hs_ref, *, tt):
    i = pl.program_id(1)

    @pl.when(i == 0)
    def _():
        h_ref[...] = h0_ref[...]

    def step(t, h):
        h = a_ref[pl.ds(t, 1), :] * h + u_ref[pl.ds(t, 1), :]
        hs_ref[pl.ds(t, 1), :] = h
        return h

    h = lax.fori_loop(0, tt, step, h_ref[...])
    h_ref[...] = h
    hlast_ref[...] = h
    y_ref[...] = (hs_ref[...] * jax.nn.gelu(g_ref[...])).astype(y_ref.dtype)


def _scan(a, u, g, h0, *, row0, B, T, tt):
    C = a.shape[1]
    assert row0 % tt == 0 and T % tt == 0
    nt = T // tt
    blk0 = row0 // tt
    blk = pl.BlockSpec((tt, C), lambda b, i: (b * nt + i, 0))
    g_blk = pl.BlockSpec((tt, C), lambda b, i: (blk0 + b * nt + i, 0))
    vec = pl.BlockSpec((None, 1, C), lambda b, i: (b, 0, 0))
    return pl.pallas_call(
        functools.partial(_scan_kernel, tt=tt),
        grid=(B, nt),
        in_specs=[blk, blk, g_blk, vec],
        out_specs=[blk, vec],
        out_shape=[jax.ShapeDtypeStruct((B * T, C), BF16),
                   jax.ShapeDtypeStruct((B, 1, C), F32)],
        scratch_shapes=[pltpu.VMEM((1, C), F32), pltpu.VMEM((tt, C), F32)],
        compiler_params=_params(("parallel", "arbitrary"), 2 * tt * C * 14 + tt * C * 12),
        name="rglru_scan",
    )(a, u, g, h0)


def _blockdiag_tiles(w, tn):
    nblk, rb, _ = w.shape
    C = nblk * rb
    sb = math.lcm(rb, tn)
    assert C % sb == 0 and sb % LANES == 0
    dense = jax.scipy.linalg.block_diag(*[w[i] for i in range(nblk)])
    tiles = [dense[(j * tn // sb) * sb:(j * tn // sb + 1) * sb, j * tn:(j + 1) * tn]
             for j in range(C // tn)]
    return jnp.concatenate(tiles, axis=0).astype(BF16), sb


def _pick(n, cands):
    for c in cands:
        if n % c == 0:
            return c
    raise ValueError(f"no tile for {n}")


def _layer(x, hist, p, rel_bias, dims):
    Tp, Bs, Ts, past = dims
    M, D = x.shape
    Ms = Bs * Ts
    k_past, v_past, ki_past, conv_buf, h0 = hist
    KV = k_past.shape[2]
    HQ = KV * GROUP * HEAD_DIM
    KVD = KV * HEAD_DIM
    DI = ki_past.shape[-1]
    C = p["conv_w"].shape[-1]
    n_in = p["w_in"].shape[1]
    HI = (n_in - HQ - 2 * KVD - DI - 2 * C - 2 * D) // (DI + 1)
    assert DI == LANES
    sizes = (HQ, KVD, KVD, HI * DI, DI, HI, C, C, D, D)
    offs = np.concatenate([[0], np.cumsum(sizes)])
    assert offs[-1] == n_in
    w_in = p["w_in"]

    def wslice(i, j=None):
        j = i if j is None else j
        return w_in[:, offs[i]:offs[j + 1]].astype(BF16)

    tm = _pick(M, (512, 256, 128, 64))
    TK = KEY_TILE

    h = _rmsnorm(x, p["norm_mix"], tm)

    def proj(w, ep, dt, tn, vecs=(), name="in_proj"):
        return _matmul(h, [w], ep, [dt], tm=tm, tn=tn, vecs=vecs, name=name)[0]

    q = proj(wslice(0), _ep_headnorm(HEAD_DIM ** -0.5 * LOG2E), BF16, _pick(HQ, (512, 256, 128)),
             vecs=[jnp.tile(p["q_norm"], HQ // HEAD_DIM)], name="in_proj_q")
    k = proj(wslice(1), _ep_headnorm(1.0), F32, _pick(KVD, (512, 256, 128)),
             vecs=[jnp.tile(p["k_norm"], KV)], name="in_proj_k")
    v = proj(wslice(2), _ep_identity, F32, _pick(KVD, (512, 256, 128)), name="in_proj_v")
    qi = proj(wslice(3), _ep_identity, BF16, _pick(HI * DI, (512, 256, 128)), name="in_proj_qi")
    kw_w = jnp.pad(wslice(4, 5), ((0, 0), (0, 2 * LANES - DI - HI)))
    kiwi = proj(kw_w, _ep_identity, F32, 2 * LANES, name="in_proj_ki")
    ki, wi = kiwi[:, :DI], kiwi[:, DI:DI + HI]
    tn_c = _pick(C, (768, 384, 128))
    xr = proj(wslice(6), _ep_identity, F32, tn_c, name="in_proj_xr")
    gr = proj(wslice(7), _ep_identity, F32, tn_c, name="in_proj_gr")
    tn_d = _pick(D, (512, 256, 128))
    ga = proj(wslice(8), _ep_identity, F32, tn_d, name="in_proj_ga")
    gb = proj(wslice(9), _ep_identity, F32, tn_d, name="in_proj_gb")

    k_bf, v_bf, ki_bf = k.astype(BF16), v.astype(BF16), ki.astype(BF16)

    assert Tp % Q_STEP == 0
    back_p = -Tp % FAR_TILE
    n_sel_p = min(TOPK_MAX, Tp // 4)
    mask_p = _select_prompt(qi, wi, jnp.pad(ki_bf[:Tp], ((0, back_p), (0, 0))), n_sel_p, Tp)
    kv_pad = lambda a: jnp.pad(a[:Tp], ((FRONT_PAD, back_p), (0, 0)))
    o_p = _attention_prompt(q, kv_pad(k_bf), kv_pad(v_bf), mask_p, _near_bias_prompt(rel_bias))

    Ls = past + Ts
    nt_s = -(-Ls // TK)
    pad_s = nt_s * TK - Ls

    def with_cache(cache, new):
        new = new[Tp:].reshape(Bs, Ts, -1)
        parts = [cache.reshape(Bs, past, -1).astype(BF16), new]
        if pad_s:
            parts.append(jnp.zeros((Bs, pad_s, new.shape[-1]), BF16))
        return jnp.concatenate(parts, axis=1)

    n_sel_s = min(TOPK_MAX, Ls // 4)
    mask_s = _select(qi[Tp:], wi[Tp:], with_cache(ki_past, ki_bf), TQ=Ts, NT=nt_s,
                     lend_a=0, lend_b=Ls, n_sel=n_sel_s, batched=True)
    nb_s = _near_bias_stream(rel_bias, Ts, TK, [Ls - ((Ls - 1) // TK) * TK])
    o_s = _attention(q[Tp:], with_cache(k_past, k_bf), with_cache(v_past, v_bf), mask_s, nb_s,
                     TQ=Ts, lend_a=0, lend_b=Ls, batched=True)
    o_a = jnp.concatenate([o_p, o_s], axis=0)

    wa_t, sb = _blockdiag_tiles(p["rg_wa"], 384)
    wx_t, _ = _blockdiag_tiles(p["rg_wx"], 384)
    per_sb = sb // 384

    def griffin(row0, B, T, buf, h_init):
        tt = _pick(math.gcd(T, row0) if row0 else T, (128, 64, 32, 16, 8))
        buf8 = jnp.pad(buf.astype(F32), ((0, 0), (SUBLANES - (CONV_W - 1), 0), (0, 0)))
        xc, xcb = _conv(xr, buf8, p["conv_w"], p["conv_b"], row0=row0, B=B, T=T, tt=tt)
        rows = B * T
        tmr = _pick(rows, (512, 256, 128, 64, 32))
        a, u = _matmul(xcb, [wa_t, wx_t], _ep_rglru_gates, [F32, F32],
                       tm=tmr, tn=384, tk=sb, n=C, exts=[xc],
                       vecs=[p["rg_ba"], p["rg_bx"], p["rg_lambda"]],
                       x_map=lambda j, i, k: (i, j // per_sb), w_map=lambda j, i, k: (j, 0),
                       name="rglru_gates")
        y, h_last = _scan(a, u, gr, h_init.reshape(B, 1, C), row0=row0, B=B, T=T, tt=tt)
        tail = xr[row0:row0 + rows].reshape(B, T, C)[:, -(CONV_W - 1):]
        conv_new = jnp.concatenate([buf.astype(F32), tail], axis=1)[:, -(CONV_W - 1):]
        return y, conv_new, h_last.reshape(B, C)

    y_p, conv_p, h_p = griffin(0, 1, Tp, jnp.zeros((1, CONV_W - 1, C), F32), jnp.zeros((1, C), F32))
    y_s, conv_s, h_s = griffin(Tp, Bs, Ts, conv_buf, h0)
    o_b = jnp.concatenate([y_p, y_s], axis=0)

    part = _matmul(o_a, [p["w_out_attn"].astype(BF16)], _ep_gate, [F32], tm=tm, tn=tn_d,
                   exts=[ga], name="out_attn")[0]
    tk_c = _pick(C, (2688, 1792, 1344, 896, 128))
    merged = _matmul(o_b, [p["w_out_rg"].astype(BF16)], _ep_gate_add, [BF16], tm=tm, tn=tn_d,
                     tk=tk_c, exts=[gb, part], name="out_rg")[0]
    x1 = _matmul(merged, [p["w_o"].astype(BF16)], _ep_residual, [F32], tm=tm, tn=tn_d,
                 exts=[x], name="w_o")[0]
    hf = _rmsnorm(x1, p["norm_ffn"], tm)
    FF = p["ffn_w1"].shape[1]
    tn_f = _pick(FF, (512, 256, 128))
    act = _matmul(hf, [p["ffn_w1"].astype(BF16), p["ffn_w3"].astype(BF16)], _ep_swiglu, [BF16],
                  tm=tm, tn=tn_f, name="ffn_up")[0]
    tk_f = _pick(FF, (5504, 2816, 2048, 1024, 512, 256, 128))
    x2 = _matmul(act, [p["ffn_w2"].astype(BF16)], _ep_residual, [F32], tm=tm, tn=tn_d,
                 tk=tk_f, exts=[x1], name="ffn_down")[0]

    new_p = (k[:Tp].reshape(1, Tp, KV, HEAD_DIM), v[:Tp].reshape(1, Tp, KV, HEAD_DIM),
             ki[:Tp].reshape(1, Tp, DI), conv_p, h_p)
    new_s = (k[Tp:].reshape(Bs, Ts, KV, HEAD_DIM), v[Tp:].reshape(Bs, Ts, KV, HEAD_DIM),
             ki[Tp:].reshape(Bs, Ts, DI), conv_s, h_s)
    return x2, new_p, new_s


def kernel(x_prompt, x_sample, cache_k, cache_v, cache_kidx, state_conv, state_rglru, norm_mix, w_in, q_norm, k_norm, rel_bias, conv_w, conv_b, rg_wa, rg_ba, rg_wx, rg_bx, rg_lambda, w_out_attn, w_out_rg, w_o, norm_ffn, ffn_w1, ffn_w3, ffn_w2):
    Bp, Tp, D = x_prompt.shape
    Bs, Ts, _ = x_sample.shape
    assert Bp == 1 and Tp % CHUNK == 0
    depth = w_in.shape[0]
    past = cache_k.shape[2]
    x = jnp.concatenate([x_prompt.reshape(Tp, D), x_sample.reshape(Bs * Ts, D)], axis=0)
    outs_p, outs_s = [], []
    for l in range(depth):
        p = dict(norm_mix=norm_mix[l], w_in=w_in[l], q_norm=q_norm[l], k_norm=k_norm[l],
                 conv_w=conv_w[l], conv_b=conv_b[l], rg_wa=rg_wa[l], rg_ba=rg_ba[l],
                 rg_wx=rg_wx[l], rg_bx=rg_bx[l], rg_lambda=rg_lambda[l],
                 w_out_attn=w_out_attn[l], w_out_rg=w_out_rg[l], w_o=w_o[l],
                 norm_ffn=norm_ffn[l], ffn_w1=ffn_w1[l], ffn_w3=ffn_w3[l], ffn_w2=ffn_w2[l])
        hist = (cache_k[l], cache_v[l], cache_kidx[l], state_conv[l], state_rglru[l])
        x, new_p, new_s = _layer(x, hist, p, rel_bias, (Tp, Bs, Ts, past))
        outs_p.append(new_p)
        outs_s.append(new_s)
    stack = lambda outs, i: jnp.stack([o[i] for o in outs])
    return (x[:Tp].reshape(1, Tp, D), x[Tp:].reshape(Bs, Ts, D),
            *[stack(outs_p, i) for i in range(5)],
            *[stack(outs_s, i) for i in range(5)])
```

```python
import functools
import math

import numpy as np
import jax
import jax.numpy as jnp
from jax import lax
from jax.experimental import pallas as pl
from jax.experimental.pallas import tpu as pltpu

F32 = jnp.float32
BF16 = jnp.bfloat16

CHUNK = 64
HEAD_DIM = 128
GROUP = 4
TOPK_MAX = 256
N_BUCKETS = 32
REL_MAX_DIST = 128
RG_C = 8.0
CONV_W = 4
EPS = 1e-6

LANES = 128
SUBLANES = 8
KEY_TILE = 256
Q_STEP = 2 * CHUNK
FAR_TILE = 512
FRONT_PAD = FAR_TILE
LOG2E = math.log2(math.e)
VMEM_CAP = 56 << 20
NEG = -1e30


def _params(sem, vmem_bytes):
    limit = min(max(int(vmem_bytes) + (6 << 20), 24 << 20), VMEM_CAP)
    return pltpu.CompilerParams(dimension_semantics=sem, vmem_limit_bytes=limit)


def _rmsnorm_kernel(x_ref, g_ref, o_ref):
    x = x_ref[...]
    ms = jnp.mean(x * x, axis=-1, keepdims=True)
    o_ref[...] = (x * lax.rsqrt(ms + EPS) * g_ref[...]).astype(o_ref.dtype)


def _rmsnorm(x, g, tm):
    M, D = x.shape
    tm = min(tm, 256)
    return pl.pallas_call(
        _rmsnorm_kernel,
        grid=(M // tm,),
        in_specs=[pl.BlockSpec((tm, D), lambda i: (i, 0)),
                  pl.BlockSpec((1, D), lambda i: (0, 0))],
        out_specs=pl.BlockSpec((tm, D), lambda i: (i, 0)),
        out_shape=jax.ShapeDtypeStruct((M, D), BF16),
        compiler_params=_params(("parallel",), 2 * tm * D * 6 + 2 * tm * D * 4),
        name="rmsnorm",
    )(x, g.reshape(1, D))


def _mm_kernel(*refs, n_w, n_ext, n_vec, n_out, nk, epilogue):
    x_ref = refs[0]
    w_refs = refs[1:1 + n_w]
    p = 1 + n_w
    ext_refs = refs[p:p + n_ext]
    p += n_ext
    vec_refs = refs[p:p + n_vec]
    p += n_vec
    out_refs = refs[p:p + n_out]
    acc_refs = refs[p + n_out:]

    def finish(accs):
        res = epilogue(accs, [e[...] for e in ext_refs], [v[...] for v in vec_refs])
        for o, r in zip(out_refs, res):
            o[...] = r.astype(o.dtype)

    if nk == 1:
        finish([jnp.dot(x_ref[...], w[...], preferred_element_type=F32) for w in w_refs])
        return

    k = pl.program_id(2)

    @pl.when(k == 0)
    def _():
        for a in acc_refs:
            a[...] = jnp.zeros_like(a)

    for a, w in zip(acc_refs, w_refs):
        a[...] += jnp.dot(x_ref[...], w[...], preferred_element_type=F32)

    @pl.when(k == nk - 1)
    def _():
        finish([a[...] for a in acc_refs])


def _matmul(x, ws, epilogue, out_dtypes, *, tm, tn, tk=None, exts=(), vecs=(), name="matmul"):
    M, K = x.shape
    n = ws[0].shape[1]
    tk = K if tk is None else tk
    nk = K // tk
    assert M % tm == 0 and n % tn == 0 and K % tk == 0
    x_map = lambda j, i, k: (i, k)
    w_map = lambda j, i, k: (k, j)
    mn_map = lambda j, i, k: (i, j)
    in_specs = [pl.BlockSpec((tm, tk), x_map)]
    in_specs += [pl.BlockSpec((tk, tn), w_map) for _ in ws]
    in_specs += [pl.BlockSpec((tm, tn), mn_map) for _ in exts]
    in_specs += [pl.BlockSpec((1, tn), lambda j, i, k: (0, j)) for _ in vecs]
    out_specs = [pl.BlockSpec((tm, tn), mn_map) for _ in out_dtypes]
    out_shape = [jax.ShapeDtypeStruct((M, n), dt) for dt in out_dtypes]
    scratch = [pltpu.VMEM((tm, tn), F32) for _ in ws] if nk > 1 else []
    vmem = (2 * (tm * tk * 2 + len(ws) * tk * tn * 2 + (len(exts) + len(out_dtypes)) * tm * tn * 4)
            + 3 * len(ws) * tm * tn * 4)
    kern = functools.partial(_mm_kernel, n_w=len(ws), n_ext=len(exts), n_vec=len(vecs),
                             n_out=len(out_dtypes), nk=nk, epilogue=epilogue)
    outs = pl.pallas_call(
        kern,
        grid=(n // tn, M // tm, nk),
        in_specs=in_specs,
        out_specs=out_specs,
        out_shape=out_shape,
        scratch_shapes=scratch,
        compiler_params=_params(("parallel", "parallel", "arbitrary"), vmem),
        name=name,
    )(x, *ws, *exts, *[v.reshape(1, -1) for v in vecs])
    return outs


def _ep_identity(accs, exts, vecs):
    return (accs[0],)


def _ep_headnorm(scale):
    def ep(accs, exts, vecs):
        a, g = accs[0], vecs[0]
        outs = []
        for j in range(a.shape[1] // HEAD_DIM):
            aj = a[:, j * HEAD_DIM:(j + 1) * HEAD_DIM]
            ms = jnp.mean(aj * aj, axis=-1, keepdims=True)
            outs.append(aj * lax.rsqrt(ms + EPS) * g[:, j * HEAD_DIM:(j + 1) * HEAD_DIM])
        y = jnp.concatenate(outs, axis=1)
        return (y * scale if scale != 1.0 else y,)
    return ep


def _ep_gate(accs, exts, vecs):
    return (jax.nn.sigmoid(exts[0]) * accs[0],)


def _ep_gate_add(accs, exts, vecs):
    return (exts[1] + jax.nn.sigmoid(exts[0]) * accs[0],)


def _ep_residual(accs, exts, vecs):
    return (exts[0] + accs[0],)


def _ep_swiglu(accs, exts, vecs):
    return (jax.nn.silu(accs[0]) * accs[1],)


def _softplus(x):
    return jnp.maximum(x, 0.0) + jnp.log1p(jnp.exp(-jnp.abs(x)))


def _ep_rglru_gates(accs, exts, vecs):
    xc = exts[0]
    r = jax.nn.sigmoid(accs[0] + vecs[0])
    i = jax.nn.sigmoid(accs[1] + vecs[1])
    log_a = -RG_C * r * _softplus(-vecs[2])
    a = jnp.exp(log_a)
    u = jnp.sqrt(1.0 - a * a) * (i * xc)
    return (a, u)


def _select_kernel(qi_ref, wi_ref, ki_ref, mask_ref, q2_ref, w2_ref, key_ref, *,
                   TQ, TK, NT, HI, lend_a, lend_b, n_sel, wscale):
    nl = TK // LANES
    lend = lend_a * pl.program_id(0) + lend_b
    nt = (lend + TK - 1) // TK
    ksel = jnp.minimum(n_sel, lend).astype(F32)

    wi = wi_ref[...] * wscale
    for h in range(HI):
        q2_ref[h * TQ:(h + 1) * TQ, :] = qi_ref[:, h * LANES:(h + 1) * LANES]
        w2_ref[h * TQ:(h + 1) * TQ, :] = jnp.broadcast_to(wi[:, h:h + 1], (TQ, LANES))

    lane = lax.broadcasted_iota(jnp.int32, (TQ, TK), 1)
    int_min = jnp.int32(-2 ** 31)

    def to_key(s):
        b = pltpu.bitcast(s, jnp.int32)
        return b ^ ((b >> 31) & jnp.int32(0x7FFFFFFF))

    def score_tile(j, carry):
        smin, smax = carry
        start = pl.multiple_of(j * TK, TK)
        kt = ki_ref[pl.ds(start, TK), :]
        s = lax.dot_general(q2_ref[...], kt, (((1,), (1,)), ((), ())),
                            preferred_element_type=F32)
        cols = []
        for l in range(nl):
            acc = jnp.zeros((TQ, LANES), F32)
            for h in range(HI):
                acc = acc + (jnp.maximum(s[h * TQ:(h + 1) * TQ, l * LANES:(l + 1) * LANES], 0.0)
                             * w2_ref[h * TQ:(h + 1) * TQ, :])
            cols.append(acc)
        sc = jnp.concatenate(cols, axis=1)
        valid = (lane + j * TK) < lend
        key_ref[j] = jnp.where(valid, to_key(sc), int_min)
        lo_s = jnp.where(valid, sc, jnp.inf)
        hi_s = jnp.where(valid, sc, -jnp.inf)
        for l in range(nl):
            smin = jnp.minimum(smin, lo_s[:, l * LANES:(l + 1) * LANES])
            smax = jnp.maximum(smax, hi_s[:, l * LANES:(l + 1) * LANES])
        return smin, smax

    smin, smax = lax.fori_loop(
        0, nt, score_tile,
        (jnp.full((TQ, LANES), jnp.inf, F32), jnp.full((TQ, LANES), -jnp.inf, F32)))
    lo = to_key(jnp.broadcast_to(jnp.min(smin, axis=1, keepdims=True), (TQ, LANES)))
    hi = to_key(jnp.broadcast_to(jnp.max(smax, axis=1, keepdims=True), (TQ, LANES)))

    def count_ge(mid):
        def body(j, c):
            kk = key_ref[j]
            for l in range(nl):
                c = c + jnp.where(kk[:, l * LANES:(l + 1) * LANES] >= mid, 1.0, 0.0)
            return c
        c = lax.fori_loop(0, nt, body, jnp.zeros((TQ, LANES), F32))
        return jnp.broadcast_to(jnp.sum(c, axis=1, keepdims=True), (TQ, LANES))

    def open_rows(lo, hi):
        return jnp.max(jnp.where(lo < hi, 1.0, 0.0))

    def cond(st):
        return st[2] > 0.5

    def body(st):
        lo, hi, _ = st
        active = lo < hi
        mid = (lo >> 1) + (hi >> 1) + ((lo | hi) & 1)
        c = count_ge(mid)
        ge = c >= ksel
        lo_n = jnp.where(ge, mid, lo)
        hi_n = jnp.where(c == ksel, mid, jnp.where(ge, hi, mid - 1))
        lo = jnp.where(active, lo_n, lo)
        hi = jnp.where(active, hi_n, hi)
        return lo, hi, open_rows(lo, hi)

    tau, _, _ = lax.while_loop(cond, body, (lo, hi, open_rows(lo, hi)))
    tau_t = jnp.concatenate([tau] * nl, axis=1)

    def write(j, _):
        mask_ref[0, j] = jnp.where(key_ref[j] >= tau_t, 1, 0).astype(jnp.int8)
        return 0

    lax.fori_loop(0, nt, write, 0)

    def clear(j, _):
        mask_ref[0, j] = jnp.zeros((TQ, TK), jnp.int8)
        return 0

    lax.fori_loop(nt, NT, clear, 0)


def _select(qi, wi, ki, *, TQ, NT, lend_a, lend_b, n_sel, batched):
    TK = KEY_TILE
    Mq = qi.shape[0]
    HI = wi.shape[1]
    steps = Mq // TQ
    if batched:
        ki_spec = pl.BlockSpec((None, NT * TK, LANES), lambda i: (i, 0, 0))
    else:
        ki_spec = pl.BlockSpec((NT * TK, LANES), lambda i: (0, 0))
    kern = functools.partial(_select_kernel, TQ=TQ, TK=TK, NT=NT, HI=HI, lend_a=lend_a,
                             lend_b=lend_b, n_sel=n_sel,
                             wscale=float(HI ** -0.5 * LANES ** -0.5))
    vmem = (2 * (TQ * HI * LANES * 2 + NT * TK * LANES * 2 + NT * TQ * TK)
            + HI * TQ * LANES * 6 + NT * TQ * TK * 4 + 3 * HI * TQ * TK * 4)
    return pl.pallas_call(
        kern,
        grid=(steps,),
        in_specs=[pl.BlockSpec((TQ, HI * LANES), lambda i: (i, 0)),
                  pl.BlockSpec((TQ, HI), lambda i: (i, 0)),
                  ki_spec],
        out_specs=pl.BlockSpec((1, NT, TQ, TK), lambda i: (i, 0, 0, 0)),
        out_shape=jax.ShapeDtypeStruct((steps, NT, TQ, TK), jnp.int8),
        scratch_shapes=[pltpu.VMEM((HI * TQ, LANES), BF16),
                        pltpu.VMEM((HI * TQ, LANES), F32),
                        pltpu.VMEM((NT, TQ, TK), jnp.int32)],
        compiler_params=_params(("parallel",), vmem),
        name="dsa_select",
    )(qi, wi, ki)


def _select_prompt_kernel(qi_ref, wi_ref, ki_ref, mask_ref, q2_ref, w2_ref, key_ref, acc_ref, *,
                          HI, HG, NT, n_sel, wscale):
    TQ, TK = Q_STEP, FAR_TILE
    nl = TK // LANES
    pad_tiles = FRONT_PAD // LANES
    RB = 64
    c2 = pl.program_id(0)
    row = lax.broadcasted_iota(jnp.int32, (TQ, LANES), 0)
    lend = jnp.where(row < CHUNK, c2 * TQ + CHUNK, c2 * TQ + TQ)
    nt = (c2 * TQ + TQ + TK - 1) // TK
    ksel = jnp.minimum(n_sel, lend).astype(F32)

    wi = wi_ref[...] * wscale
    for h in range(HI):
        q2_ref[h * TQ:(h + 1) * TQ, :] = qi_ref[:, h * LANES:(h + 1) * LANES]
        w2_ref[h * TQ:(h + 1) * TQ, :] = jnp.broadcast_to(wi[:, h:h + 1], (TQ, LANES))

    lane = lax.broadcasted_iota(jnp.int32, (TQ, TK), 1)
    lend_t = jnp.concatenate([lend] * nl, axis=1)
    int_min = jnp.int32(-2 ** 31)

    def to_key(s):
        b = pltpu.bitcast(s, jnp.int32)
        return b ^ ((b >> 31) & jnp.int32(0x7FFFFFFF))

    def score_tile(j, carry):
        smin, smax = carry
        start = pl.multiple_of(j * TK, TK)
        kt = ki_ref[pl.ds(start, TK), :]
        for hg in range(HI // HG):
            s = lax.dot_general(q2_ref[hg * HG * TQ:(hg + 1) * HG * TQ, :], kt,
                                (((1,), (1,)), ((), ())), preferred_element_type=F32)
            for r in range(TQ // RB):
                cs = [None] * nl
                for h in range(HG):
                    r0 = h * TQ + r * RB
                    w = w2_ref[(hg * HG) * TQ + r0:(hg * HG) * TQ + r0 + RB, :]
                    for l in range(nl):
                        term = jnp.maximum(s[r0:r0 + RB, l * LANES:(l + 1) * LANES], 0.0) * w
                        cs[l] = term if cs[l] is None else cs[l] + term
                for l in range(nl):
                    if hg == 0:
                        acc_ref[r * RB:(r + 1) * RB, l * LANES:(l + 1) * LANES] = cs[l]
                    else:
                        acc_ref[r * RB:(r + 1) * RB, l * LANES:(l + 1) * LANES] += cs[l]
        sc = acc_ref[...]
        valid = (lane + j * TK) < lend_t
        key_ref[j] = jnp.where(valid, to_key(sc), int_min)
        lo_s = jnp.where(valid, sc, jnp.inf)
        hi_s = jnp.where(valid, sc, -jnp.inf)
        for l in range(nl):
            smin = jnp.minimum(smin, lo_s[:, l * LANES:(l + 1) * LANES])
            smax = jnp.maximum(smax, hi_s[:, l * LANES:(l + 1) * LANES])
        return smin, smax

    smin, smax = lax.fori_loop(
        0, nt, score_tile,
        (jnp.full((TQ, LANES), jnp.inf, F32), jnp.full((TQ, LANES), -jnp.inf, F32)))
    lo = to_key(jnp.broadcast_to(jnp.min(smin, axis=1, keepdims=True), (TQ, LANES)))
    hi = to_key(jnp.broadcast_to(jnp.max(smax, axis=1, keepdims=True), (TQ, LANES)))

    def count_ge(mid):
        def body(j, c):
            kk = key_ref[j]
            for l in range(nl):
                c = c + jnp.where(kk[:, l * LANES:(l + 1) * LANES] >= mid, 1.0, 0.0)
            return c
        c = lax.fori_loop(0, nt, body, jnp.zeros((TQ, LANES), F32))
        return jnp.broadcast_to(jnp.sum(c, axis=1, keepdims=True), (TQ, LANES))

    def open_rows(lo, hi):
        return jnp.max(jnp.where(lo < hi, 1.0, 0.0))

    def cond(st):
        return st[2] > 0.5

    def body(st):
        lo, hi, _ = st
        active = lo < hi
        mid = (lo >> 1) + (hi >> 1) + ((lo | hi) & 1)
        c = count_ge(mid)
        ge = c >= ksel
        lo_n = jnp.where(ge, mid, lo)
        hi_n = jnp.where(c == ksel, mid, jnp.where(ge, hi, mid - 1))
        lo = jnp.where(active, lo_n, lo)
        hi = jnp.where(active, hi_n, hi)
        return lo, hi, open_rows(lo, hi)

    tau, _, _ = lax.while_loop(cond, body, (lo, hi, open_rows(lo, hi)))

    neg_tile = jnp.full((TQ, LANES), NEG, BF16)
    for i in range(pad_tiles):
        mask_ref[0, i] = neg_tile

    def write(j, _):
        kk = key_ref[j]
        for i in range(nl):
            mask_ref[0, pad_tiles + nl * j + i] = jnp.where(
                kk[:, i * LANES:(i + 1) * LANES] >= tau, 0.0, NEG).T.astype(BF16)
        return 0

    lax.fori_loop(0, nt, write, 0)

    def clear(j, _):
        for i in range(nl):
            mask_ref[0, pad_tiles + nl * j + i] = neg_tile
        return 0

    lax.fori_loop(nt, NT, clear, 0)


def _select_prompt(qi, wi, ki, n_sel, T):
    TQ, TK = Q_STEP, FAR_TILE
    HI = wi.shape[1]
    HG = 4 if HI % 4 == 0 else 1
    NT = ki.shape[0] // TK
    steps = T // TQ
    ntile = (FRONT_PAD + NT * TK) // LANES
    kern = functools.partial(_select_prompt_kernel, HI=HI, HG=HG, NT=NT, n_sel=n_sel,
                             wscale=float(HI ** -0.5 * LANES ** -0.5))
    vmem = (2 * (TQ * HI * LANES * 2 + NT * TK * LANES * 2 + ntile * TQ * LANES * 2)
            + HI * TQ * LANES * 6 + NT * TQ * TK * 4 + TQ * TK * 4 + 4 * HG * TQ * TK * 4)
    return pl.pallas_call(
        kern,
        grid=(steps,),
        in_specs=[pl.BlockSpec((TQ, HI * LANES), lambda i: (i, 0)),
                  pl.BlockSpec((TQ, HI), lambda i: (i, 0)),
                  pl.BlockSpec((NT * TK, LANES), lambda i: (0, 0))],
        out_specs=pl.BlockSpec((1, ntile, TQ, LANES), lambda i: (i, 0, 0, 0)),
        out_shape=jax.ShapeDtypeStruct((steps, ntile, TQ, LANES), BF16),
        scratch_shapes=[pltpu.VMEM((HI * TQ, LANES), BF16),
                        pltpu.VMEM((HI * TQ, LANES), F32),
                        pltpu.VMEM((NT, TQ, TK), jnp.int32),
                        pltpu.VMEM((TQ, TK), F32)],
        compiler_params=_params(("parallel",), vmem),
        name="dsa_select_prompt",
    )(qi, wi, ki)


def _attn_kernel(q_ref, k_ref, v_ref, mask_ref, nb_ref, o_ref, q2_ref, m_ref, l_ref, acc_ref, *,
                 TQ, TK, G, lend_a, lend_b, step_axis):
    nl = TK // LANES
    R = G * TQ
    lend = lend_a * pl.program_id(step_axis) + lend_b
    jl = (lend - 1) // TK

    for g in range(G):
        q2_ref[g * TQ:(g + 1) * TQ, :] = q_ref[:, g * HEAD_DIM:(g + 1) * HEAD_DIM]
    m_ref[...] = jnp.full((R, LANES), NEG, F32)
    l_ref[...] = jnp.zeros((R, LANES), F32)
    acc_ref[...] = jnp.zeros((R, HEAD_DIM), F32)

    def tile(j, half):
        start = pl.multiple_of(j * TK, TK)
        kt = k_ref[pl.ds(start, TK), :]
        vt = v_ref[pl.ds(start, TK), :]
        s = lax.dot_general(q2_ref[...], kt, (((1,), (1,)), ((), ())),
                            preferred_element_type=F32)
        madd = jnp.where(mask_ref[0, j].astype(jnp.int32) != 0, 0.0, NEG)
        rows = []
        for g in range(G):
            sg = s[g * TQ:(g + 1) * TQ, :]
            if half is not None:
                sg = sg + nb_ref[0, g, :, half * TK:(half + 1) * TK]
            rows.append(sg + madd)
        s = jnp.concatenate(rows, axis=0)
        m_old = m_ref[...]
        m_cur = s[:, :LANES]
        for l in range(1, nl):
            m_cur = jnp.maximum(m_cur, s[:, l * LANES:(l + 1) * LANES])
        m_new = jnp.maximum(m_old, jnp.broadcast_to(jnp.max(m_cur, axis=1, keepdims=True), (R, LANES)))
        alpha = jnp.exp2(m_old - m_new)
        p = jnp.exp2(s - jnp.concatenate([m_new] * nl, axis=1))
        psum = p[:, :LANES]
        for l in range(1, nl):
            psum = psum + p[:, l * LANES:(l + 1) * LANES]
        l_ref[...] = alpha * l_ref[...] + jnp.broadcast_to(
            jnp.sum(psum, axis=1, keepdims=True), (R, LANES))
        acc_ref[...] = alpha * acc_ref[...] + jnp.dot(
            p.astype(BF16), vt, preferred_element_type=F32)
        m_ref[...] = m_new

    def far(j, _):
        tile(j, None)
        return 0

    lax.fori_loop(0, jnp.maximum(jl - 1, 0), far, 0)

    @pl.when(jl >= 1)
    def _():
        tile(jl - 1, 0)

    tile(jl, 1)

    o = acc_ref[...] / l_ref[...]
    for g in range(G):
        o_ref[:, g * HEAD_DIM:(g + 1) * HEAD_DIM] = o[g * TQ:(g + 1) * TQ, :].astype(o_ref.dtype)


def _attention(q, k, v, mask, nb, *, TQ, lend_a, lend_b, batched):
    TK = KEY_TILE
    G = GROUP
    Mq, HD = q.shape
    KV = HD // (G * HEAD_DIM)
    steps = Mq // TQ
    NT = mask.shape[1]
    L = k.shape[-2]
    P = nb.shape[0]
    if batched:
        grid = (steps, KV)
        q_map = lambda b, h: (b, h)
        kv_spec = pl.BlockSpec((None, L, HEAD_DIM), lambda b, h: (b, 0, h))
        mask_map = lambda b, h: (b, 0, 0, 0)
        nb_map = lambda b, h: (0, h, 0, 0)
        step_axis = 0
    else:
        grid = (KV, steps)
        q_map = lambda h, c: (c, h)
        kv_spec = pl.BlockSpec((L, HEAD_DIM), lambda h, c: (0, h))
        mask_map = lambda h, c: (c, 0, 0, 0)
        nb_map = lambda h, c: (c % P, h, 0, 0)
        step_axis = 1
    kern = functools.partial(_attn_kernel, TQ=TQ, TK=TK, G=G, lend_a=lend_a, lend_b=lend_b,
                             step_axis=step_axis)
    R = G * TQ
    vmem = (2 * (2 * TQ * G * HEAD_DIM * 2 + 2 * L * HEAD_DIM * 2 + NT * TQ * TK + G * TQ * 2 * TK * 4)
            + R * LANES * 14 + 6 * R * TK * 4)
    return pl.pallas_call(
        kern,
        grid=grid,
        in_specs=[pl.BlockSpec((TQ, G * HEAD_DIM), q_map),
                  kv_spec, kv_spec,
                  pl.BlockSpec((1, NT, TQ, TK), mask_map),
                  pl.BlockSpec((1, G, TQ, 2 * TK), nb_map)],
        out_specs=pl.BlockSpec((TQ, G * HEAD_DIM), q_map),
        out_shape=jax.ShapeDtypeStruct((Mq, HD), BF16),
        scratch_shapes=[pltpu.VMEM((R, HEAD_DIM), BF16),
                        pltpu.VMEM((R, LANES), F32),
                        pltpu.VMEM((R, LANES), F32),
                        pltpu.VMEM((R, HEAD_DIM), F32)],
        compiler_params=_params(("parallel", "arbitrary"), vmem),
        name="dsa_attention",
    )(q, k, v, mask, nb)


def _attn_prompt_kernel(q_ref, k_ref, v_ref, mask_ref, nb_ref, o_ref,
                        q2_ref, m_ref, l_ref, acc_ref, sa_ref, sb_ref, p_ref):
    TQ, TK, G = Q_STEP, FAR_TILE, GROUP
    R = G * TQ
    NW = 2 * TQ
    c2 = pl.program_id(1)
    far_len = TQ * (c2 - 1)
    nfar = jnp.maximum((far_len + TK - 1) // TK, 0)
    nt_dims = (((1,), (1,)), ((), ()))

    for g in range(G):
        q2_ref[g * TQ:(g + 1) * TQ, :] = q_ref[:, g * HEAD_DIM:(g + 1) * HEAD_DIM]
    m_ref[...] = jnp.full((1, R), NEG, F32)
    l_ref[...] = jnp.zeros((1, R), F32)
    acc_ref[...] = jnp.zeros((HEAD_DIM, R), F32)

    def far_start(j):
        return pl.multiple_of(jnp.maximum(far_len - TK * j, 0), LANES)

    def logits(start, width):
        return lax.dot_general(k_ref[pl.ds(start, width), :], q2_ref[...], nt_dims,
                               preferred_element_type=F32)

    def fold8(x, op):
        y = x[:SUBLANES]
        for i in range(1, x.shape[0] // SUBLANES):
            y = op(y, x[i * SUBLANES:(i + 1) * SUBLANES])
        return y

    def update(s_ref, start, width, biased):
        t0 = start // LANES
        CH = 64
        mx = jnp.full((SUBLANES, R), NEG, F32)
        for r in range(width // CH):
            rows = slice(r * CH, (r + 1) * CH)
            off = (r * CH) % LANES
            madd = mask_ref[0, t0 + (r * CH) // LANES, off:off + CH, :].astype(F32)
            x = s_ref[rows, :] + jnp.concatenate([madd] * G, axis=1)
            if biased:
                x = x + nb_ref[rows, :]
            s_ref[rows, :] = x
            mx = jnp.maximum(mx, fold8(x, jnp.maximum))
        m_old = m_ref[...]
        m_new = jnp.maximum(m_old, jnp.max(mx, axis=0, keepdims=True))
        alpha = jnp.exp2(m_old - m_new)
        sm = jnp.zeros((SUBLANES, R), F32)
        for r in range(width // CH):
            rows = slice(r * CH, (r + 1) * CH)
            p = jnp.exp2(s_ref[rows, :] - m_new)
            sm = sm + fold8(p, jnp.add)
            p_ref[rows, :] = p.astype(BF16)
        l_ref[...] = alpha * l_ref[...] + jnp.sum(sm, axis=0, keepdims=True)
        vt = jnp.concatenate([v_ref[t0 + i] for i in range(width // LANES)], axis=1)
        acc_ref[...] = alpha * acc_ref[...] + jnp.dot(
            vt, p_ref[0:width, :], preferred_element_type=F32)
        m_ref[...] = m_new

    sa_ref[...] = logits(far_start(0), TK)

    def pair(i, _):
        sb_ref[...] = logits(far_start(2 * i + 1), TK)
        update(sa_ref, far_start(2 * i), TK, False)
        sa_ref[...] = logits(far_start(2 * i + 2), TK)
        update(sb_ref, far_start(2 * i + 1), TK, False)
        return 0

    lax.fori_loop(0, (nfar + 1) // 2, pair, 0)

    near = pl.multiple_of(TQ * c2 + FRONT_PAD - TQ, LANES)
    sa_ref[0:NW, :] = logits(near, NW)
    update(sa_ref, near, NW, True)

    o = acc_ref[...] / l_ref[...]
    for g in range(G):
        o_ref[:, g * HEAD_DIM:(g + 1) * HEAD_DIM] = o[:, g * TQ:(g + 1) * TQ].T.astype(o_ref.dtype)


def _attention_prompt(q, k, v, mask, nb):
    TQ, TK, G = Q_STEP, FAR_TILE, GROUP
    HD = q.shape[1]
    KV = HD // (G * HEAD_DIM)
    steps = mask.shape[0]
    T = steps * TQ
    ntile = mask.shape[1]
    Lp = k.shape[0]
    assert Lp == ntile * LANES and Lp >= FRONT_PAD + T and TQ == LANES
    R = G * TQ
    vt = v.reshape(ntile, LANES, KV, HEAD_DIM).transpose(2, 0, 3, 1)
    nbt = nb.reshape(KV, G, TQ, 2 * TQ).transpose(0, 3, 1, 2).reshape(KV, 2 * TQ, R)
    vmem = (2 * (2 * TQ * G * HEAD_DIM * 2 + 2 * Lp * HEAD_DIM * 2 + ntile * TQ * LANES * 2
                 + 2 * TQ * R * 4)
            + R * LANES * 6 + 2 * R * TK * 4 + 5 * R * TK * 4)
    q_map = lambda h, c: (c, h)
    return pl.pallas_call(
        _attn_prompt_kernel,
        grid=(KV, steps),
        in_specs=[pl.BlockSpec((TQ, G * HEAD_DIM), q_map),
                  pl.BlockSpec((Lp, HEAD_DIM), lambda h, c: (0, h)),
                  pl.BlockSpec((None, ntile, HEAD_DIM, LANES), lambda h, c: (h, 0, 0, 0)),
                  pl.BlockSpec((1, ntile, LANES, TQ), lambda h, c: (c, 0, 0, 0)),
                  pl.BlockSpec((None, 2 * TQ, R), lambda h, c: (h, 0, 0))],
        out_specs=pl.BlockSpec((TQ, G * HEAD_DIM), q_map),
        out_shape=jax.ShapeDtypeStruct((T, HD), BF16),
        scratch_shapes=[pltpu.VMEM((R, HEAD_DIM), BF16),
                        pltpu.VMEM((1, R), F32),
                        pltpu.VMEM((1, R), F32),
                        pltpu.VMEM((HEAD_DIM, R), F32),
                        pltpu.VMEM((TK, R), F32),
                        pltpu.VMEM((TK, R), F32),
                        pltpu.VMEM((TK, R), BF16)],
        compiler_params=_params(("parallel", "arbitrary"), vmem),
        name="dsa_attention_prompt",
    )(q, k, vt, mask, nbt)


def _t5_bucket_np(rel):
    nb = N_BUCKETS // 2
    max_exact = nb // 2
    side = np.where(rel > 0, nb, 0)
    n = np.abs(rel)
    nf = np.maximum(n, 1).astype(np.float32)
    large = max_exact + (np.log(nf / np.float32(max_exact))
                         / np.float32(math.log(REL_MAX_DIST / max_exact))
                         * np.float32(nb - max_exact)).astype(np.int32)
    large = np.minimum(large, nb - 1)
    return side + np.where(n < max_exact, n, large)


def _near_bias(rel_bias, rel, far_rel):
    far = int(_t5_bucket_np(np.array(far_rel)))
    assert far_rel < 0 and far == int(_t5_bucket_np(np.array(-10 ** 6)))
    rb = rel_bias.astype(F32) * LOG2E
    tab = rb[_t5_bucket_np(rel)] - rb[far][None, None, None, :]
    return tab.transpose(0, 3, 1, 2)


def _near_bias_stream(rel_bias, TQ, TK, phases):
    t = np.arange(TQ)[:, None]
    j = np.arange(2 * TK)[None, :]
    rel = np.stack([j - TK - ph + TQ - t for ph in phases])
    return _near_bias(rel_bias, rel, TQ - TK - 2)


def _near_bias_prompt(rel_bias):
    TQ = Q_STEP
    rel = np.arange(2 * TQ)[None, :] - TQ - np.arange(TQ)[:, None]
    return _near_bias(rel_bias, rel[None], -TQ - 1)[0]


def _conv_kernel(x_ref, halo_ref, buf_ref, w_ref, b_ref, xc_ref, xcb_ref, ext_ref, *, tt):
    first = pl.program_id(1) == 0
    ext_ref[0:SUBLANES, :] = jnp.where(first, buf_ref[...], halo_ref[...])
    ext_ref[SUBLANES:, :] = x_ref[...]
    y = jnp.broadcast_to(b_ref[...], x_ref.shape)
    for j in range(CONV_W):
        off = SUBLANES - (CONV_W - 1) + j
        y = y + ext_ref[off:off + tt, :] * w_ref[j:j + 1, :]
    xc_ref[...] = y
    xcb_ref[...] = y.astype(BF16)


def _conv(x, buf8, w, b, *, row0, B, T, tt):
    C = x.shape[1]
    assert row0 % tt == 0 and T % tt == 0 and tt % SUBLANES == 0
    nt = T // tt
    hb = tt // SUBLANES
    blk0 = row0 // tt
    main_map = lambda b, i: (blk0 + b * nt + i, 0)
    halo_map = lambda b, i: (jnp.maximum((blk0 + b * nt + i) * hb - 1, 0), 0)
    out_map = lambda b, i: (b * nt + i, 0)
    return pl.pallas_call(
        functools.partial(_conv_kernel, tt=tt),
        grid=(B, nt),
        in_specs=[pl.BlockSpec((tt, C), main_map),
                  pl.BlockSpec((SUBLANES, C), halo_map),
                  pl.BlockSpec((None, SUBLANES, C), lambda b, i: (b, 0, 0)),
                  pl.BlockSpec((CONV_W, C), lambda b, i: (0, 0)),
                  pl.BlockSpec((1, C), lambda b, i: (0, 0))],
        out_specs=[pl.BlockSpec((tt, C), out_map), pl.BlockSpec((tt, C), out_map)],
        out_shape=[jax.ShapeDtypeStruct((B * T, C), F32),
                   jax.ShapeDtypeStruct((B * T, C), BF16)],
        scratch_shapes=[pltpu.VMEM((tt + SUBLANES, C), F32)],
        compiler_params=_params(("parallel", "parallel"), 2 * tt * C * 10 + tt * C * 12),
        name="causal_conv",
    )(x, x, buf8, w, b.reshape(1, C))


def _scan_kernel(a_ref, u_ref, g_ref, h0_ref, y_ref, hlast_ref, h_ref, hs_ref, *, tt):
    i = pl.program_id(1)

    @pl.when(i == 0)
    def _():
        h_ref[...] = h0_ref[...]

    def step(t, h):
        h = a_ref[pl.ds(t, 1), :] * h + u_ref[pl.ds(t, 1), :]
        hs_ref[pl.ds(t, 1), :] = h
        return h

    h = lax.fori_loop(0, tt, step, h_ref[...])
    h_ref[...] = h
    hlast_ref[...] = h
    y_ref[...] = (hs_ref[...] * jax.nn.gelu(g_ref[...])).astype(y_ref.dtype)


def _scan(a, u, g, h0, *, row0, B, T, tt):
    C = a.shape[1]
    assert row0 % tt == 0 and T % tt == 0
    nt = T // tt
    blk0 = row0 // tt
    blk = pl.BlockSpec((tt, C), lambda b, i: (b * nt + i, 0))
    g_blk = pl.BlockSpec((tt, C), lambda b, i: (blk0 + b * nt + i, 0))
    vec = pl.BlockSpec((None, 1, C), lambda b, i: (b, 0, 0))
    return pl.pallas_call(
        functools.partial(_scan_kernel, tt=tt),
        grid=(B, nt),
        in_specs=[blk, blk, g_blk, vec],
        out_specs=[blk, vec],
        out_shape=[jax.ShapeDtypeStruct((B * T, C), BF16),
                   jax.ShapeDtypeStruct((B, 1, C), F32)],
        scratch_shapes=[pltpu.VMEM((1, C), F32), pltpu.VMEM((tt, C), F32)],
        compiler_params=_params(("parallel", "arbitrary"), 2 * tt * C * 14 + tt * C * 12),
        name="rglru_scan",
    )(a, u, g, h0)


GATE_TN = 256


def _gate_window(rb, C):
    raw = [((j * GATE_TN) // rb * rb) // LANES * LANES for j in range(C // GATE_TN)]
    ends = [((j * GATE_TN + GATE_TN - 1) // rb + 1) * rb for j in range(C // GATE_TN)]
    kw = -(-max(e - s for s, e in zip(raw, ends)) // LANES) * LANES
    kw = min(kw, C)
    starts = [min(s, C - kw) for s in raw]
    assert all(s + kw >= e for s, e in zip(starts, ends))
    return starts, kw


def _blockdiag_tiles(w):
    nblk, rb, _ = w.shape
    C = nblk * rb
    starts, kw = _gate_window(rb, C)
    dense = jax.scipy.linalg.block_diag(*[w[i] for i in range(nblk)])
    tiles = [dense[s:s + kw, j * GATE_TN:(j + 1) * GATE_TN] for j, s in enumerate(starts)]
    return jnp.stack(tiles).astype(BF16)


def _gates_kernel(x_ref, wa_ref, wx_ref, xc_ref, ba_ref, bx_ref, lam_ref, a_ref, u_ref, *, rb, kw):
    C = x_ref.shape[1]
    j = pl.program_id(1)
    start = jnp.minimum(((j * GATE_TN) // rb * rb) // LANES * LANES, C - kw)
    x = x_ref[:, pl.ds(pl.multiple_of(start, LANES), kw)]
    accs = [jnp.dot(x, w[0], preferred_element_type=F32) for w in (wa_ref, wx_ref)]
    a, u = _ep_rglru_gates(accs, [xc_ref[...]], [ba_ref[...], bx_ref[...], lam_ref[...]])
    a_ref[...] = a
    u_ref[...] = u


def _rglru_gates(xcb, xc, wa_t, wx_t, ba, bx, lam, *, rb, tm):
    rows, C = xc.shape
    nt, kw, _ = wa_t.shape
    tile = lambda: pl.BlockSpec((tm, GATE_TN), lambda i, j: (i, j))
    vec = lambda: pl.BlockSpec((1, GATE_TN), lambda i, j: (0, j))
    wspec = lambda: pl.BlockSpec((1, kw, GATE_TN), lambda i, j: (j, 0, 0))
    vmem = 2 * (tm * C * 2 + 2 * kw * GATE_TN * 2 + 3 * tm * GATE_TN * 4) + 8 * tm * GATE_TN * 4
    return pl.pallas_call(
        functools.partial(_gates_kernel, rb=rb, kw=kw),
        grid=(rows // tm, nt),
        in_specs=[pl.BlockSpec((tm, C), lambda i, j: (i, 0)), wspec(), wspec(), tile(),
                  vec(), vec(), vec()],
        out_specs=[tile(), tile()],
        out_shape=[jax.ShapeDtypeStruct((rows, C), F32)] * 2,
        compiler_params=_params(("parallel", "arbitrary"), vmem),
        name="rglru_gates",
    )(xcb, wa_t, wx_t, xc, ba.reshape(1, C), bx.reshape(1, C), lam.reshape(1, C))


def _pick(n, cands):
    for c in cands:
        if n % c == 0:
            return c
    raise ValueError(f"no tile for {n}")


def _layer(x, hist, p, rel_bias, dims):
    Tp, Bs, Ts, past = dims
    M, D = x.shape
    Ms = Bs * Ts
    k_past, v_past, ki_past, conv_buf, h0 = hist
    KV = k_past.shape[2]
    HQ = KV * GROUP * HEAD_DIM
    KVD = KV * HEAD_DIM
    DI = ki_past.shape[-1]
    C = p["conv_w"].shape[-1]
    n_in = p["w_in"].shape[1]
    HI = (n_in - HQ - 2 * KVD - DI - 2 * C - 2 * D) // (DI + 1)
    assert DI == LANES
    sizes = (HQ, KVD, KVD, HI * DI, DI, HI, C, C, D, D)
    offs = np.concatenate([[0], np.cumsum(sizes)])
    assert offs[-1] == n_in
    w_in = p["w_in"]

    def wslice(i, j=None):
        j = i if j is None else j
        return w_in[:, offs[i]:offs[j + 1]].astype(BF16)

    tm = _pick(M, (512, 256, 128, 64))
    TK = KEY_TILE

    h = _rmsnorm(x, p["norm_mix"], tm)

    def proj(w, ep, dt, tn, vecs=(), name="in_proj"):
        return _matmul(h, [w], ep, [dt], tm=tm, tn=tn, vecs=vecs, name=name)[0]

    q = proj(wslice(0), _ep_headnorm(HEAD_DIM ** -0.5 * LOG2E), BF16, _pick(HQ, (512, 256, 128)),
             vecs=[jnp.tile(p["q_norm"], HQ // HEAD_DIM)], name="in_proj_q")
    k = proj(wslice(1), _ep_headnorm(1.0), F32, _pick(KVD, (512, 256, 128)),
             vecs=[jnp.tile(p["k_norm"], KV)], name="in_proj_k")
    v = proj(wslice(2), _ep_identity, F32, _pick(KVD, (512, 256, 128)), name="in_proj_v")
    qi = proj(wslice(3), _ep_identity, BF16, _pick(HI * DI, (512, 256, 128)), name="in_proj_qi")
    kw_w = jnp.pad(wslice(4, 5), ((0, 0), (0, 2 * LANES - DI - HI)))
    kiwi = proj(kw_w, _ep_identity, F32, 2 * LANES, name="in_proj_ki")
    ki, wi = kiwi[:, :DI], kiwi[:, DI:DI + HI]
    tn_c = _pick(C, (768, 384, 128))
    xr = proj(wslice(6), _ep_identity, F32, tn_c, name="in_proj_xr")
    gr = proj(wslice(7), _ep_identity, F32, tn_c, name="in_proj_gr")
    tn_d = _pick(D, (512, 256, 128))
    ga = proj(wslice(8), _ep_identity, F32, tn_d, name="in_proj_ga")
    gb = proj(wslice(9), _ep_identity, F32, tn_d, name="in_proj_gb")

    k_bf, v_bf, ki_bf = k.astype(BF16), v.astype(BF16), ki.astype(BF16)

    assert Tp % Q_STEP == 0
    back_p = -Tp % FAR_TILE
    n_sel_p = min(TOPK_MAX, Tp // 4)
    mask_p = _select_prompt(qi, wi, jnp.pad(ki_bf[:Tp], ((0, back_p), (0, 0))), n_sel_p, Tp)
    kv_pad = lambda a: jnp.pad(a[:Tp], ((FRONT_PAD, back_p), (0, 0)))
    o_p = _attention_prompt(q, kv_pad(k_bf), kv_pad(v_bf), mask_p, _near_bias_prompt(rel_bias))

    Ls = past + Ts
    nt_s = -(-Ls // TK)
    pad_s = nt_s * TK - Ls

    def with_cache(cache, new):
        new = new[Tp:].reshape(Bs, Ts, -1)
        parts = [cache.reshape(Bs, past, -1).astype(BF16), new]
        if pad_s:
            parts.append(jnp.zeros((Bs, pad_s, new.shape[-1]), BF16))
        return jnp.concatenate(parts, axis=1)

    n_sel_s = min(TOPK_MAX, Ls // 4)
    mask_s = _select(qi[Tp:], wi[Tp:], with_cache(ki_past, ki_bf), TQ=Ts, NT=nt_s,
                     lend_a=0, lend_b=Ls, n_sel=n_sel_s, batched=True)
    nb_s = _near_bias_stream(rel_bias, Ts, TK, [Ls - ((Ls - 1) // TK) * TK])
    o_s = _attention(q[Tp:], with_cache(k_past, k_bf), with_cache(v_past, v_bf), mask_s, nb_s,
                     TQ=Ts, lend_a=0, lend_b=Ls, batched=True)
    o_a = jnp.concatenate([o_p, o_s], axis=0)

    assert C % GATE_TN == 0
    wa_t = _blockdiag_tiles(p["rg_wa"])
    wx_t = _blockdiag_tiles(p["rg_wx"])

    def griffin(row0, B, T, buf, h_init):
        tt = _pick(math.gcd(T, row0) if row0 else T, (128, 64, 32, 16, 8))
        buf8 = jnp.pad(buf.astype(F32), ((0, 0), (SUBLANES - (CONV_W - 1), 0), (0, 0)))
        xc, xcb = _conv(xr, buf8, p["conv_w"], p["conv_b"], row0=row0, B=B, T=T, tt=tt)
        rows = B * T
        tmr = _pick(rows, (512, 256, 128, 64, 32))
        a, u = _rglru_gates(xcb, xc, wa_t, wx_t, p["rg_ba"], p["rg_bx"], p["rg_lambda"],
                            rb=p["rg_wa"].shape[1], tm=tmr)
        y, h_last = _scan(a, u, gr, h_init.reshape(B, 1, C), row0=row0, B=B, T=T, tt=tt)
        tail = xr[row0:row0 + rows].reshape(B, T, C)[:, -(CONV_W - 1):]
        conv_new = jnp.concatenate([buf.astype(F32), tail], axis=1)[:, -(CONV_W - 1):]
        return y, conv_new, h_last.reshape(B, C)

    y_p, conv_p, h_p = griffin(0, 1, Tp, jnp.zeros((1, CONV_W - 1, C), F32), jnp.zeros((1, C), F32))
    y_s, conv_s, h_s = griffin(Tp, Bs, Ts, conv_buf, h0)
    o_b = jnp.concatenate([y_p, y_s], axis=0)

    part = _matmul(o_a, [p["w_out_attn"].astype(BF16)], _ep_gate, [F32], tm=tm, tn=tn_d,
                   exts=[ga], name="out_attn")[0]
    merged = _matmul(o_b, [p["w_out_rg"].astype(BF16)], _ep_gate_add, [BF16], tm=tm, tn=tn_d,
                     exts=[gb, part], name="out_rg")[0]
    x1 = _matmul(merged, [p["w_o"].astype(BF16)], _ep_residual, [F32], tm=tm, tn=tn_d,
                 exts=[x], name="w_o")[0]
    hf = _rmsnorm(x1, p["norm_ffn"], tm)
    FF = p["ffn_w1"].shape[1]
    tn_f = _pick(FF, (512, 256, 128))
    act = _matmul(hf, [p["ffn_w1"].astype(BF16), p["ffn_w3"].astype(BF16)], _ep_swiglu, [BF16],
                  tm=tm, tn=tn_f, name="ffn_up")[0]
    tk_f = _pick(FF, (5504, 2816, 2048, 1024, 512, 256, 128))
    x2 = _matmul(act, [p["ffn_w2"].astype(BF16)], _ep_residual, [F32],
                 tm=_pick(M, (768, 512, 256, 128, 64)), tn=tn_d, tk=tk_f, exts=[x1],
                 name="ffn_down")[0]

    new_p = (k[:Tp].reshape(1, Tp, KV, HEAD_DIM), v[:Tp].reshape(1, Tp, KV, HEAD_DIM),
             ki[:Tp].reshape(1, Tp, DI), conv_p, h_p)
    new_s = (k[Tp:].reshape(Bs, Ts, KV, HEAD_DIM), v[Tp:].reshape(Bs, Ts, KV, HEAD_DIM),
             ki[Tp:].reshape(Bs, Ts, DI), conv_s, h_s)
    return x2, new_p, new_s


def kernel(x_prompt, x_sample, cache_k, cache_v, cache_kidx, state_conv, state_rglru, norm_mix, w_in, q_norm, k_norm, rel_bias, conv_w, conv_b, rg_wa, rg_ba, rg_wx, rg_bx, rg_lambda, w_out_attn, w_out_rg, w_o, norm_ffn, ffn_w1, ffn_w3, ffn_w2):
    Bp, Tp, D = x_prompt.shape
    Bs, Ts, _ = x_sample.shape
    assert Bp == 1 and Tp % CHUNK == 0
    depth = w_in.shape[0]
    past = cache_k.shape[2]
    x = jnp.concatenate([x_prompt.reshape(Tp, D), x_sample.reshape(Bs * Ts, D)], axis=0)
    outs_p, outs_s = [], []
    for l in range(depth):
        p = dict(norm_mix=norm_mix[l], w_in=w_in[l], q_norm=q_norm[l], k_norm=k_norm[l],
                 conv_w=conv_w[l], conv_b=conv_b[l], rg_wa=rg_wa[l], rg_ba=rg_ba[l],
                 rg_wx=rg_wx[l], rg_bx=rg_bx[l], rg_lambda=rg_lambda[l],
                 w_out_attn=w_out_attn[l], w_out_rg=w_out_rg[l], w_o=w_o[l],
                 norm_ffn=norm_ffn[l], ffn_w1=ffn_w1[l], ffn_w3=ffn_w3[l], ffn_w2=ffn_w2[l])
        hist = (cache_k[l], cache_v[l], cache_kidx[l], state_conv[l], state_rglru[l])
        x, new_p, new_s = _layer(x, hist, p, rel_bias, (Tp, Bs, Ts, past))
        outs_p.append(new_p)
        outs_s.append(new_s)
    stack = lambda outs, i: jnp.stack([o[i] for o in outs])
    return (x[:Tp].reshape(1, Tp, D), x[Tp:].reshape(Bs, Ts, D),
            *[stack(outs_p, i) for i in range(5)],
            *[stack(outs_s, i) for i in range(5)])
```

```python
import functools
import math

import numpy as np
import jax
import jax.numpy as jnp
from jax import lax
from jax.experimental import pallas as pl
from jax.experimental.pallas import tpu as pltpu

F32 = jnp.float32
BF16 = jnp.bfloat16

CHUNK = 64
HEAD_DIM = 128
GROUP = 4
TOPK_MAX = 256
N_BUCKETS = 32
REL_MAX_DIST = 128
RG_C = 8.0
CONV_W = 4
EPS = 1e-6

LANES = 128
SUBLANES = 8
KEY_TILE = 256
Q_STEP = 2 * CHUNK
FAR_TILE = 512
FRONT_PAD = FAR_TILE
ONES_ROWS = 16
LOG2E = math.log2(math.e)
VMEM_CAP = 56 << 20
NEG = -1e30


def _params(sem, vmem_bytes):
    limit = min(max(int(vmem_bytes) + (6 << 20), 24 << 20), VMEM_CAP)
    return pltpu.CompilerParams(dimension_semantics=sem, vmem_limit_bytes=limit)


def _rmsnorm_kernel(x_ref, g_ref, o_ref):
    x = x_ref[...]
    ms = jnp.mean(x * x, axis=-1, keepdims=True)
    o_ref[...] = (x * lax.rsqrt(ms + EPS) * g_ref[...]).astype(o_ref.dtype)


def _rmsnorm(x, g, tm):
    M, D = x.shape
    tm = min(tm, 256)
    return pl.pallas_call(
        _rmsnorm_kernel,
        grid=(M // tm,),
        in_specs=[pl.BlockSpec((tm, D), lambda i: (i, 0)),
                  pl.BlockSpec((1, D), lambda i: (0, 0))],
        out_specs=pl.BlockSpec((tm, D), lambda i: (i, 0)),
        out_shape=jax.ShapeDtypeStruct((M, D), BF16),
        compiler_params=_params(("parallel",), 2 * tm * D * 6 + 2 * tm * D * 4),
        name="rmsnorm",
    )(x, g.reshape(1, D))


def _mm_kernel(*refs, n_w, n_ext, n_vec, n_out, nk, epilogue):
    x_ref = refs[0]
    w_refs = refs[1:1 + n_w]
    p = 1 + n_w
    ext_refs = refs[p:p + n_ext]
    p += n_ext
    vec_refs = refs[p:p + n_vec]
    p += n_vec
    out_refs = refs[p:p + n_out]
    acc_refs = refs[p + n_out:]

    def finish(accs):
        res = epilogue(accs, [e[...] for e in ext_refs], [v[...] for v in vec_refs])
        for o, r in zip(out_refs, res):
            o[...] = r.astype(o.dtype)

    if nk == 1:
        finish([jnp.dot(x_ref[...], w[...], preferred_element_type=F32) for w in w_refs])
        return

    k = pl.program_id(2)

    @pl.when(k == 0)
    def _():
        for a in acc_refs:
            a[...] = jnp.zeros_like(a)

    for a, w in zip(acc_refs, w_refs):
        a[...] += jnp.dot(x_ref[...], w[...], preferred_element_type=F32)

    @pl.when(k == nk - 1)
    def _():
        finish([a[...] for a in acc_refs])


def _matmul(x, ws, epilogue, out_dtypes, *, tm, tn, tk=None, exts=(), vecs=(), name="matmul"):
    M, K = x.shape
    n = ws[0].shape[1]
    tk = K if tk is None else tk
    nk = K // tk
    assert M % tm == 0 and n % tn == 0 and K % tk == 0
    x_map = lambda j, i, k: (i, k)
    w_map = lambda j, i, k: (k, j)
    mn_map = lambda j, i, k: (i, j)
    in_specs = [pl.BlockSpec((tm, tk), x_map)]
    in_specs += [pl.BlockSpec((tk, tn), w_map) for _ in ws]
    in_specs += [pl.BlockSpec((tm, tn), mn_map) for _ in exts]
    in_specs += [pl.BlockSpec((1, tn), lambda j, i, k: (0, j)) for _ in vecs]
    out_specs = [pl.BlockSpec((tm, tn), mn_map) for _ in out_dtypes]
    out_shape = [jax.ShapeDtypeStruct((M, n), dt) for dt in out_dtypes]
    scratch = [pltpu.VMEM((tm, tn), F32) for _ in ws] if nk > 1 else []
    vmem = (2 * (tm * tk * 2 + len(ws) * tk * tn * 2 + (len(exts) + len(out_dtypes)) * tm * tn * 4)
            + 3 * len(ws) * tm * tn * 4)
    kern = functools.partial(_mm_kernel, n_w=len(ws), n_ext=len(exts), n_vec=len(vecs),
                             n_out=len(out_dtypes), nk=nk, epilogue=epilogue)
    outs = pl.pallas_call(
        kern,
        grid=(n // tn, M // tm, nk),
        in_specs=in_specs,
        out_specs=out_specs,
        out_shape=out_shape,
        scratch_shapes=scratch,
        compiler_params=_params(("parallel", "parallel", "arbitrary"), vmem),
        name=name,
    )(x, *ws, *exts, *[v.reshape(1, -1) for v in vecs])
    return outs


def _ep_identity(accs, exts, vecs):
    return (accs[0],)


def _ep_headnorm(scale):
    def ep(accs, exts, vecs):
        a, g = accs[0], vecs[0]
        outs = []
        for j in range(a.shape[1] // HEAD_DIM):
            aj = a[:, j * HEAD_DIM:(j + 1) * HEAD_DIM]
            ms = jnp.mean(aj * aj, axis=-1, keepdims=True)
            outs.append(aj * lax.rsqrt(ms + EPS) * g[:, j * HEAD_DIM:(j + 1) * HEAD_DIM])
        y = jnp.concatenate(outs, axis=1)
        return (y * scale if scale != 1.0 else y,)
    return ep


def _ep_gate(accs, exts, vecs):
    return (jax.nn.sigmoid(exts[0]) * accs[0],)


def _ep_gate_add(accs, exts, vecs):
    return (exts[1] + jax.nn.sigmoid(exts[0]) * accs[0],)


def _ep_residual(accs, exts, vecs):
    return (exts[0] + accs[0],)


def _ep_swiglu(accs, exts, vecs):
    return (jax.nn.silu(accs[0]) * accs[1],)


def _softplus(x):
    return jnp.maximum(x, 0.0) + jnp.log1p(jnp.exp(-jnp.abs(x)))


def _ep_rglru_gates(accs, exts, vecs):
    xc = exts[0]
    r = jax.nn.sigmoid(accs[0] + vecs[0])
    i = jax.nn.sigmoid(accs[1] + vecs[1])
    log_a = -RG_C * r * _softplus(-vecs[2])
    a = jnp.exp(log_a)
    u = jnp.sqrt(1.0 - a * a) * (i * xc)
    return (a, u)


def _select_kernel(qi_ref, wi_ref, ki_ref, mask_ref, q2_ref, w2_ref, key_ref, *,
                   TQ, TK, NT, HI, lend_a, lend_b, n_sel, wscale):
    nl = TK // LANES
    lend = lend_a * pl.program_id(0) + lend_b
    nt = (lend + TK - 1) // TK
    ksel = jnp.minimum(n_sel, lend).astype(F32)

    wi = wi_ref[...] * wscale
    for h in range(HI):
        q2_ref[h * TQ:(h + 1) * TQ, :] = qi_ref[:, h * LANES:(h + 1) * LANES]
        w2_ref[h * TQ:(h + 1) * TQ, :] = jnp.broadcast_to(wi[:, h:h + 1], (TQ, LANES))

    lane = lax.broadcasted_iota(jnp.int32, (TQ, TK), 1)
    int_min = jnp.int32(-2 ** 31)

    def to_key(s):
        b = pltpu.bitcast(s, jnp.int32)
        return b ^ ((b >> 31) & jnp.int32(0x7FFFFFFF))

    def score_tile(j, carry):
        smin, smax = carry
        start = pl.multiple_of(j * TK, TK)
        kt = ki_ref[pl.ds(start, TK), :]
        s = lax.dot_general(q2_ref[...], kt, (((1,), (1,)), ((), ())),
                            preferred_element_type=F32)
        cols = []
        for l in range(nl):
            acc = jnp.zeros((TQ, LANES), F32)
            for h in range(HI):
                acc = acc + (jnp.maximum(s[h * TQ:(h + 1) * TQ, l * LANES:(l + 1) * LANES], 0.0)
                             * w2_ref[h * TQ:(h + 1) * TQ, :])
            cols.append(acc)
        sc = jnp.concatenate(cols, axis=1)
        valid = (lane + j * TK) < lend
        key_ref[j] = jnp.where(valid, to_key(sc), int_min)
        lo_s = jnp.where(valid, sc, jnp.inf)
        hi_s = jnp.where(valid, sc, -jnp.inf)
        for l in range(nl):
            smin = jnp.minimum(smin, lo_s[:, l * LANES:(l + 1) * LANES])
            smax = jnp.maximum(smax, hi_s[:, l * LANES:(l + 1) * LANES])
        return smin, smax

    smin, smax = lax.fori_loop(
        0, nt, score_tile,
        (jnp.full((TQ, LANES), jnp.inf, F32), jnp.full((TQ, LANES), -jnp.inf, F32)))
    lo = to_key(jnp.broadcast_to(jnp.min(smin, axis=1, keepdims=True), (TQ, LANES)))
    hi = to_key(jnp.broadcast_to(jnp.max(smax, axis=1, keepdims=True), (TQ, LANES)))

    def count_ge(mid):
        def body(j, c):
            kk = key_ref[j]
            for l in range(nl):
                c = c + jnp.where(kk[:, l * LANES:(l + 1) * LANES] >= mid, 1.0, 0.0)
            return c
        c = lax.fori_loop(0, nt, body, jnp.zeros((TQ, LANES), F32))
        return jnp.broadcast_to(jnp.sum(c, axis=1, keepdims=True), (TQ, LANES))

    def open_rows(lo, hi):
        return jnp.max(jnp.where(lo < hi, 1.0, 0.0))

    def cond(st):
        return st[2] > 0.5

    def body(st):
        lo, hi, _ = st
        active = lo < hi
        mid = (lo >> 1) + (hi >> 1) + ((lo | hi) & 1)
        c = count_ge(mid)
        ge = c >= ksel
        lo_n = jnp.where(ge, mid, lo)
        hi_n = jnp.where(c == ksel, mid, jnp.where(ge, hi, mid - 1))
        lo = jnp.where(active, lo_n, lo)
        hi = jnp.where(active, hi_n, hi)
        return lo, hi, open_rows(lo, hi)

    tau, _, _ = lax.while_loop(cond, body, (lo, hi, open_rows(lo, hi)))
    tau_t = jnp.concatenate([tau] * nl, axis=1)

    def write(j, _):
        mask_ref[0, j] = jnp.where(key_ref[j] >= tau_t, 1, 0).astype(jnp.int8)
        return 0

    lax.fori_loop(0, nt, write, 0)

    def clear(j, _):
        mask_ref[0, j] = jnp.zeros((TQ, TK), jnp.int8)
        return 0

    lax.fori_loop(nt, NT, clear, 0)


def _select(qi, wi, ki, *, TQ, NT, lend_a, lend_b, n_sel, batched):
    TK = KEY_TILE
    Mq = qi.shape[0]
    HI = wi.shape[1]
    steps = Mq // TQ
    if batched:
        ki_spec = pl.BlockSpec((None, NT * TK, LANES), lambda i: (i, 0, 0))
    else:
        ki_spec = pl.BlockSpec((NT * TK, LANES), lambda i: (0, 0))
    kern = functools.partial(_select_kernel, TQ=TQ, TK=TK, NT=NT, HI=HI, lend_a=lend_a,
                             lend_b=lend_b, n_sel=n_sel,
                             wscale=float(HI ** -0.5 * LANES ** -0.5))
    vmem = (2 * (TQ * HI * LANES * 2 + NT * TK * LANES * 2 + NT * TQ * TK)
            + HI * TQ * LANES * 6 + NT * TQ * TK * 4 + 3 * HI * TQ * TK * 4)
    return pl.pallas_call(
        kern,
        grid=(steps,),
        in_specs=[pl.BlockSpec((TQ, HI * LANES), lambda i: (i, 0)),
                  pl.BlockSpec((TQ, HI), lambda i: (i, 0)),
                  ki_spec],
        out_specs=pl.BlockSpec((1, NT, TQ, TK), lambda i: (i, 0, 0, 0)),
        out_shape=jax.ShapeDtypeStruct((steps, NT, TQ, TK), jnp.int8),
        scratch_shapes=[pltpu.VMEM((HI * TQ, LANES), BF16),
                        pltpu.VMEM((HI * TQ, LANES), F32),
                        pltpu.VMEM((NT, TQ, TK), jnp.int32)],
        compiler_params=_params(("parallel",), vmem),
        name="dsa_select",
    )(qi, wi, ki)


def _select_prompt_kernel(qi_ref, wi_ref, ki_ref, mask_ref, q2_ref, w2_ref, key_ref, acc_ref, *,
                          HI, HG, NT, n_sel, wscale):
    TQ, TK = Q_STEP, FAR_TILE
    nl = TK // LANES
    pad_tiles = FRONT_PAD // LANES
    RB = 64
    c2 = pl.program_id(0)
    row = lax.broadcasted_iota(jnp.int32, (TQ, LANES), 0)
    lend = jnp.where(row < CHUNK, c2 * TQ + CHUNK, c2 * TQ + TQ)
    nt = (c2 * TQ + TQ + TK - 1) // TK
    ksel = jnp.minimum(n_sel, lend).astype(F32)

    wi = wi_ref[...] * wscale
    for h in range(HI):
        q2_ref[h * TQ:(h + 1) * TQ, :] = qi_ref[:, h * LANES:(h + 1) * LANES]
        w2_ref[h * TQ:(h + 1) * TQ, :] = jnp.broadcast_to(wi[:, h:h + 1], (TQ, LANES))

    lane = lax.broadcasted_iota(jnp.int32, (TQ, TK), 1)
    lend_t = jnp.concatenate([lend] * nl, axis=1)
    int_min = jnp.int32(-2 ** 31)

    def to_key(s):
        b = pltpu.bitcast(s, jnp.int32)
        return b ^ ((b >> 31) & jnp.int32(0x7FFFFFFF))

    def score_tile(j, carry):
        smin, smax = carry
        start = pl.multiple_of(j * TK, TK)
        kt = ki_ref[pl.ds(start, TK), :]
        for hg in range(HI // HG):
            s = lax.dot_general(q2_ref[hg * HG * TQ:(hg + 1) * HG * TQ, :], kt,
                                (((1,), (1,)), ((), ())), preferred_element_type=F32)
            for r in range(TQ // RB):
                cs = [None] * nl
                for h in range(HG):
                    r0 = h * TQ + r * RB
                    w = w2_ref[(hg * HG) * TQ + r0:(hg * HG) * TQ + r0 + RB, :]
                    for l in range(nl):
                        term = jnp.maximum(s[r0:r0 + RB, l * LANES:(l + 1) * LANES], 0.0) * w
                        cs[l] = term if cs[l] is None else cs[l] + term
                for l in range(nl):
                    if hg == 0:
                        acc_ref[r * RB:(r + 1) * RB, l * LANES:(l + 1) * LANES] = cs[l]
                    else:
                        acc_ref[r * RB:(r + 1) * RB, l * LANES:(l + 1) * LANES] += cs[l]
        sc = acc_ref[...]
        valid = (lane + j * TK) < lend_t
        key_ref[j] = jnp.where(valid, to_key(sc), int_min)
        lo_s = jnp.where(valid, sc, jnp.inf)
        hi_s = jnp.where(valid, sc, -jnp.inf)
        for l in range(nl):
            smin = jnp.minimum(smin, lo_s[:, l * LANES:(l + 1) * LANES])
            smax = jnp.maximum(smax, hi_s[:, l * LANES:(l + 1) * LANES])
        return smin, smax

    smin, smax = lax.fori_loop(
        0, nt, score_tile,
        (jnp.full((TQ, LANES), jnp.inf, F32), jnp.full((TQ, LANES), -jnp.inf, F32)))
    lo = to_key(jnp.broadcast_to(jnp.min(smin, axis=1, keepdims=True), (TQ, LANES)))
    hi = to_key(jnp.broadcast_to(jnp.max(smax, axis=1, keepdims=True), (TQ, LANES)))

    def count_ge(mid):
        def body(j, c):
            kk = key_ref[j]
            for l in range(nl):
                c = c + jnp.where(kk[:, l * LANES:(l + 1) * LANES] >= mid, 1.0, 0.0)
            return c
        c = lax.fori_loop(0, nt, body, jnp.zeros((TQ, LANES), F32))
        return jnp.broadcast_to(jnp.sum(c, axis=1, keepdims=True), (TQ, LANES))

    def open_rows(lo, hi):
        return jnp.max(jnp.where(lo < hi, 1.0, 0.0))

    def cond(st):
        return st[2] > 0.5

    def body(st):
        lo, hi, _ = st
        active = lo < hi
        mid = (lo >> 1) + (hi >> 1) + ((lo | hi) & 1)
        c = count_ge(mid)
        ge = c >= ksel
        lo_n = jnp.where(ge, mid, lo)
        hi_n = jnp.where(c == ksel, mid, jnp.where(ge, hi, mid - 1))
        lo = jnp.where(active, lo_n, lo)
        hi = jnp.where(active, hi_n, hi)
        return lo, hi, open_rows(lo, hi)

    tau, _, _ = lax.while_loop(cond, body, (lo, hi, open_rows(lo, hi)))

    neg_tile = jnp.full((TQ, LANES), NEG, BF16)
    for i in range(pad_tiles):
        mask_ref[0, i] = neg_tile

    def write(j, _):
        kk = key_ref[j]
        for i in range(nl):
            mask_ref[0, pad_tiles + nl * j + i] = jnp.where(
                kk[:, i * LANES:(i + 1) * LANES] >= tau, 0.0, NEG).T.astype(BF16)
        return 0

    lax.fori_loop(0, nt, write, 0)

    def clear(j, _):
        for i in range(nl):
            mask_ref[0, pad_tiles + nl * j + i] = neg_tile
        return 0

    lax.fori_loop(nt, NT, clear, 0)


def _select_prompt(qi, wi, ki, n_sel, T):
    TQ, TK = Q_STEP, FAR_TILE
    HI = wi.shape[1]
    HG = 4 if HI % 4 == 0 else 1
    NT = ki.shape[0] // TK
    steps = T // TQ
    ntile = (FRONT_PAD + NT * TK) // LANES
    kern = functools.partial(_select_prompt_kernel, HI=HI, HG=HG, NT=NT, n_sel=n_sel,
                             wscale=float(HI ** -0.5 * LANES ** -0.5))
    vmem = (2 * (TQ * HI * LANES * 2 + NT * TK * LANES * 2 + ntile * TQ * LANES * 2)
            + HI * TQ * LANES * 6 + NT * TQ * TK * 4 + TQ * TK * 4 + 4 * HG * TQ * TK * 4)
    return pl.pallas_call(
        kern,
        grid=(steps,),
        in_specs=[pl.BlockSpec((TQ, HI * LANES), lambda i: (i, 0)),
                  pl.BlockSpec((TQ, HI), lambda i: (i, 0)),
                  pl.BlockSpec((NT * TK, LANES), lambda i: (0, 0))],
        out_specs=pl.BlockSpec((1, ntile, TQ, LANES), lambda i: (i, 0, 0, 0)),
        out_shape=jax.ShapeDtypeStruct((steps, ntile, TQ, LANES), BF16),
        scratch_shapes=[pltpu.VMEM((HI * TQ, LANES), BF16),
                        pltpu.VMEM((HI * TQ, LANES), F32),
                        pltpu.VMEM((NT, TQ, TK), jnp.int32),
                        pltpu.VMEM((TQ, TK), F32)],
        compiler_params=_params(("parallel",), vmem),
        name="dsa_select_prompt",
    )(qi, wi, ki)


def _attn_kernel(q_ref, k_ref, v_ref, mask_ref, nb_ref, o_ref, q2_ref, m_ref, l_ref, acc_ref, *,
                 TQ, TK, G, lend_a, lend_b, step_axis):
    nl = TK // LANES
    R = G * TQ
    lend = lend_a * pl.program_id(step_axis) + lend_b
    jl = (lend - 1) // TK

    for g in range(G):
        q2_ref[g * TQ:(g + 1) * TQ, :] = q_ref[:, g * HEAD_DIM:(g + 1) * HEAD_DIM]
    m_ref[...] = jnp.full((R, LANES), NEG, F32)
    l_ref[...] = jnp.zeros((R, LANES), F32)
    acc_ref[...] = jnp.zeros((R, HEAD_DIM), F32)

    def tile(j, half):
        start = pl.multiple_of(j * TK, TK)
        kt = k_ref[pl.ds(start, TK), :]
        vt = v_ref[pl.ds(start, TK), :]
        s = lax.dot_general(q2_ref[...], kt, (((1,), (1,)), ((), ())),
                            preferred_element_type=F32)
        madd = jnp.where(mask_ref[0, j].astype(jnp.int32) != 0, 0.0, NEG)
        rows = []
        for g in range(G):
            sg = s[g * TQ:(g + 1) * TQ, :]
            if half is not None:
                sg = sg + nb_ref[0, g, :, half * TK:(half + 1) * TK]
            rows.append(sg + madd)
        s = jnp.concatenate(rows, axis=0)
        m_old = m_ref[...]
        m_cur = s[:, :LANES]
        for l in range(1, nl):
            m_cur = jnp.maximum(m_cur, s[:, l * LANES:(l + 1) * LANES])
        m_new = jnp.maximum(m_old, jnp.broadcast_to(jnp.max(m_cur, axis=1, keepdims=True), (R, LANES)))
        alpha = jnp.exp2(m_old - m_new)
        p = jnp.exp2(s - jnp.concatenate([m_new] * nl, axis=1))
        psum = p[:, :LANES]
        for l in range(1, nl):
            psum = psum + p[:, l * LANES:(l + 1) * LANES]
        l_ref[...] = alpha * l_ref[...] + jnp.broadcast_to(
            jnp.sum(psum, axis=1, keepdims=True), (R, LANES))
        acc_ref[...] = alpha * acc_ref[...] + jnp.dot(
            p.astype(BF16), vt, preferred_element_type=F32)
        m_ref[...] = m_new

    def far(j, _):
        tile(j, None)
        return 0

    lax.fori_loop(0, jnp.maximum(jl - 1, 0), far, 0)

    @pl.when(jl >= 1)
    def _():
        tile(jl - 1, 0)

    tile(jl, 1)

    o = acc_ref[...] / l_ref[...]
    for g in range(G):
        o_ref[:, g * HEAD_DIM:(g + 1) * HEAD_DIM] = o[g * TQ:(g + 1) * TQ, :].astype(o_ref.dtype)


def _attention(q, k, v, mask, nb, *, TQ, lend_a, lend_b, batched):
    TK = KEY_TILE
    G = GROUP
    Mq, HD = q.shape
    KV = HD // (G * HEAD_DIM)
    steps = Mq // TQ
    NT = mask.shape[1]
    L = k.shape[-2]
    P = nb.shape[0]
    if batched:
        grid = (steps, KV)
        q_map = lambda b, h: (b, h)
        kv_spec = pl.BlockSpec((None, L, HEAD_DIM), lambda b, h: (b, 0, h))
        mask_map = lambda b, h: (b, 0, 0, 0)
        nb_map = lambda b, h: (0, h, 0, 0)
        step_axis = 0
    else:
        grid = (KV, steps)
        q_map = lambda h, c: (c, h)
        kv_spec = pl.BlockSpec((L, HEAD_DIM), lambda h, c: (0, h))
        mask_map = lambda h, c: (c, 0, 0, 0)
        nb_map = lambda h, c: (c % P, h, 0, 0)
        step_axis = 1
    kern = functools.partial(_attn_kernel, TQ=TQ, TK=TK, G=G, lend_a=lend_a, lend_b=lend_b,
                             step_axis=step_axis)
    R = G * TQ
    vmem = (2 * (2 * TQ * G * HEAD_DIM * 2 + 2 * L * HEAD_DIM * 2 + NT * TQ * TK + G * TQ * 2 * TK * 4)
            + R * LANES * 14 + 6 * R * TK * 4)
    return pl.pallas_call(
        kern,
        grid=grid,
        in_specs=[pl.BlockSpec((TQ, G * HEAD_DIM), q_map),
                  kv_spec, kv_spec,
                  pl.BlockSpec((1, NT, TQ, TK), mask_map),
                  pl.BlockSpec((1, G, TQ, 2 * TK), nb_map)],
        out_specs=pl.BlockSpec((TQ, G * HEAD_DIM), q_map),
        out_shape=jax.ShapeDtypeStruct((Mq, HD), BF16),
        scratch_shapes=[pltpu.VMEM((R, HEAD_DIM), BF16),
                        pltpu.VMEM((R, LANES), F32),
                        pltpu.VMEM((R, LANES), F32),
                        pltpu.VMEM((R, HEAD_DIM), F32)],
        compiler_params=_params(("parallel", "arbitrary"), vmem),
        name="dsa_attention",
    )(q, k, v, mask, nb)


def _attn_prompt_kernel(q_ref, k_ref, v_ref, mask_ref, nb_ref, o_ref,
                        q2_ref, m_ref, acc_ref, sa_ref, sb_ref, p_ref):
    TQ, TK, G = Q_STEP, FAR_TILE, GROUP
    R = G * TQ
    NW = 2 * TQ
    c2 = pl.program_id(1)
    far_len = TQ * (c2 - 1)
    nfar = jnp.maximum((far_len + TK - 1) // TK, 0)
    nt_dims = (((1,), (1,)), ((), ()))

    for g in range(G):
        q2_ref[g * TQ:(g + 1) * TQ, :] = q_ref[:, g * HEAD_DIM:(g + 1) * HEAD_DIM]
    m_ref[...] = jnp.full((1, R), NEG, F32)
    acc_ref[...] = jnp.zeros(acc_ref.shape, F32)

    def far_start(j):
        return pl.multiple_of(jnp.maximum(far_len - TK * j, 0), LANES)

    def logits(start, width):
        return lax.dot_general(k_ref[pl.ds(start, width), :], q2_ref[...], nt_dims,
                               preferred_element_type=F32)

    def fold8(x, op):
        y = x[:SUBLANES]
        for i in range(1, x.shape[0] // SUBLANES):
            y = op(y, x[i * SUBLANES:(i + 1) * SUBLANES])
        return y

    def update(s_ref, start, width, biased):
        t0 = start // LANES
        CH = 64
        mx = jnp.full((SUBLANES, R), NEG, F32)
        for r in range(width // CH):
            rows = slice(r * CH, (r + 1) * CH)
            off = (r * CH) % LANES
            madd = mask_ref[0, t0 + (r * CH) // LANES, off:off + CH, :].astype(F32)
            x = s_ref[rows, :] + jnp.concatenate([madd] * G, axis=1)
            if biased:
                x = x + nb_ref[rows, :]
            s_ref[rows, :] = x
            mx = jnp.maximum(mx, fold8(x, jnp.maximum))
        m_old = m_ref[...]
        m_new = jnp.maximum(m_old, jnp.max(mx, axis=0, keepdims=True))
        alpha = jnp.exp2(m_old - m_new)
        for r in range(width // CH):
            rows = slice(r * CH, (r + 1) * CH)
            p_ref[rows, :] = jnp.exp2((s_ref[rows, :] - m_new).astype(BF16))
        vt = jnp.concatenate([v_ref[t0 + i] for i in range(width // LANES)], axis=1)
        acc_ref[...] = alpha * acc_ref[...] + jnp.dot(
            vt, p_ref[0:width, :], preferred_element_type=F32)
        m_ref[...] = m_new

    sa_ref[...] = logits(far_start(0), TK)
    near = pl.multiple_of(TQ * c2 + FRONT_PAD - TQ, LANES)
    sb_ref[0:NW, :] = logits(near, NW)
    update(sb_ref, near, NW, True)

    def pair(j):
        sb_ref[...] = logits(far_start(j + 1), TK)
        update(sa_ref, far_start(j), TK, False)
        sa_ref[...] = logits(far_start(j + 2), TK)
        update(sb_ref, far_start(j + 1), TK, False)

    def quad(i, _):
        pair(4 * i)
        pair(4 * i + 2)
        return 0

    nquad = nfar // 4
    lax.fori_loop(0, nquad, quad, 0)

    def rest(i, _):
        pair(4 * nquad + 2 * i)
        return 0

    lax.fori_loop(0, (nfar - 4 * nquad + 1) // 2, rest, 0)

    o = acc_ref[0:HEAD_DIM, :] / acc_ref[HEAD_DIM:HEAD_DIM + 1, :]
    for g in range(G):
        o_ref[:, g * HEAD_DIM:(g + 1) * HEAD_DIM] = o[:, g * TQ:(g + 1) * TQ].T.astype(o_ref.dtype)


def _attention_prompt(q, k, v, mask, nb):
    TQ, TK, G = Q_STEP, FAR_TILE, GROUP
    HD = q.shape[1]
    KV = HD // (G * HEAD_DIM)
    steps = mask.shape[0]
    T = steps * TQ
    ntile = mask.shape[1]
    Lp = k.shape[0]
    assert Lp == ntile * LANES and Lp >= FRONT_PAD + T and TQ == LANES
    R = G * TQ
    VR = HEAD_DIM + ONES_ROWS
    vt = v.reshape(ntile, LANES, KV, HEAD_DIM).transpose(2, 0, 3, 1)
    vt = jnp.concatenate([vt, jnp.ones((KV, ntile, ONES_ROWS, LANES), BF16)], axis=2)
    nbt = nb.reshape(KV, G, TQ, 2 * TQ).transpose(0, 3, 1, 2).reshape(KV, 2 * TQ, R)
    vmem = (2 * (2 * TQ * G * HEAD_DIM * 2 + 2 * Lp * HEAD_DIM * 2 + ntile * TQ * LANES * 2
                 + 2 * TQ * R * 4)
            + R * LANES * 6 + 2 * R * TK * 4 + 5 * R * TK * 4)
    q_map = lambda h, c: (c, h)
    return pl.pallas_call(
        _attn_prompt_kernel,
        grid=(KV, steps),
        in_specs=[pl.BlockSpec((TQ, G * HEAD_DIM), q_map),
                  pl.BlockSpec((Lp, HEAD_DIM), lambda h, c: (0, h)),
                  pl.BlockSpec((None, ntile, VR, LANES), lambda h, c: (h, 0, 0, 0)),
                  pl.BlockSpec((1, ntile, LANES, TQ), lambda h, c: (c, 0, 0, 0)),
                  pl.BlockSpec((None, 2 * TQ, R), lambda h, c: (h, 0, 0))],
        out_specs=pl.BlockSpec((TQ, G * HEAD_DIM), q_map),
        out_shape=jax.ShapeDtypeStruct((T, HD), BF16),
        scratch_shapes=[pltpu.VMEM((R, HEAD_DIM), BF16),
                        pltpu.VMEM((1, R), F32),
                        pltpu.VMEM((VR, R), F32),
                        pltpu.VMEM((TK, R), F32),
                        pltpu.VMEM((TK, R), F32),
                        pltpu.VMEM((TK, R), BF16)],
        compiler_params=_params(("parallel", "arbitrary"), vmem),
        name="dsa_attention_prompt",
    )(q, k, vt, mask, nbt)


def _t5_bucket_np(rel):
    nb = N_BUCKETS // 2
    max_exact = nb // 2
    side = np.where(rel > 0, nb, 0)
    n = np.abs(rel)
    nf = np.maximum(n, 1).astype(np.float32)
    large = max_exact + (np.log(nf / np.float32(max_exact))
                         / np.float32(math.log(REL_MAX_DIST / max_exact))
                         * np.float32(nb - max_exact)).astype(np.int32)
    large = np.minimum(large, nb - 1)
    return side + np.where(n < max_exact, n, large)


def _near_bias(rel_bias, rel, far_rel):
    far = int(_t5_bucket_np(np.array(far_rel)))
    assert far_rel < 0 and far == int(_t5_bucket_np(np.array(-10 ** 6)))
    rb = rel_bias.astype(F32) * LOG2E
    tab = rb[_t5_bucket_np(rel)] - rb[far][None, None, None, :]
    return tab.transpose(0, 3, 1, 2)


def _near_bias_stream(rel_bias, TQ, TK, phases):
    t = np.arange(TQ)[:, None]
    j = np.arange(2 * TK)[None, :]
    rel = np.stack([j - TK - ph + TQ - t for ph in phases])
    return _near_bias(rel_bias, rel, TQ - TK - 2)


def _near_bias_prompt(rel_bias):
    TQ = Q_STEP
    rel = np.arange(2 * TQ)[None, :] - TQ - np.arange(TQ)[:, None]
    return _near_bias(rel_bias, rel[None], -TQ - 1)[0]


def _conv_kernel(x_ref, halo_ref, buf_ref, w_ref, b_ref, xc_ref, xcb_ref, ext_ref, *, tt):
    first = pl.program_id(1) == 0
    ext_ref[0:SUBLANES, :] = jnp.where(first, buf_ref[...], halo_ref[...])
    ext_ref[SUBLANES:, :] = x_ref[...]
    y = jnp.broadcast_to(b_ref[...], x_ref.shape)
    for j in range(CONV_W):
        off = SUBLANES - (CONV_W - 1) + j
        y = y + ext_ref[off:off + tt, :] * w_ref[j:j + 1, :]
    xc_ref[...] = y
    xcb_ref[...] = y.astype(BF16)


def _conv(x, buf8, w, b, *, row0, B, T, tt):
    C = x.shape[1]
    assert row0 % tt == 0 and T % tt == 0 and tt % SUBLANES == 0
    nt = T // tt
    hb = tt // SUBLANES
    blk0 = row0 // tt
    main_map = lambda b, i: (blk0 + b * nt + i, 0)
    halo_map = lambda b, i: (jnp.maximum((blk0 + b * nt + i) * hb - 1, 0), 0)
    out_map = lambda b, i: (b * nt + i, 0)
    return pl.pallas_call(
        functools.partial(_conv_kernel, tt=tt),
        grid=(B, nt),
        in_specs=[pl.BlockSpec((tt, C), main_map),
                  pl.BlockSpec((SUBLANES, C), halo_map),
                  pl.BlockSpec((None, SUBLANES, C), lambda b, i: (b, 0, 0)),
                  pl.BlockSpec((CONV_W, C), lambda b, i: (0, 0)),
                  pl.BlockSpec((1, C), lambda b, i: (0, 0))],
        out_specs=[pl.BlockSpec((tt, C), out_map), pl.BlockSpec((tt, C), out_map)],
        out_shape=[jax.ShapeDtypeStruct((B * T, C), F32),
                   jax.ShapeDtypeStruct((B * T, C), BF16)],
        scratch_shapes=[pltpu.VMEM((tt + SUBLANES, C), F32)],
        compiler_params=_params(("parallel", "parallel"), 2 * tt * C * 10 + tt * C * 12),
        name="causal_conv",
    )(x, x, buf8, w, b.reshape(1, C))


def _scan_kernel(a_ref, u_ref, g_ref, h0_ref, y_ref, hlast_ref, h_ref, hs_ref, *, tt):
    i = pl.program_id(1)

    @pl.when(i == 0)
    def _():
        h_ref[...] = h0_ref[...]

    def step(t, h):
        h = a_ref[pl.ds(t, 1), :] * h + u_ref[pl.ds(t, 1), :]
        hs_ref[pl.ds(t, 1), :] = h
        return h

    h = lax.fori_loop(0, tt, step, h_ref[...])
    h_ref[...] = h
    hlast_ref[...] = h
    y_ref[...] = (hs_ref[...] * jax.nn.gelu(g_ref[...])).astype(y_ref.dtype)


def _scan(a, u, g, h0, *, row0, B, T, tt):
    C = a.shape[1]
    assert row0 % tt == 0 and T % tt == 0
    nt = T // tt
    blk0 = row0 // tt
    blk = pl.BlockSpec((tt, C), lambda b, i: (b * nt + i, 0))
    g_blk = pl.BlockSpec((tt, C), lambda b, i: (blk0 + b * nt + i, 0))
    vec = pl.BlockSpec((None, 1, C), lambda b, i: (b, 0, 0))
    return pl.pallas_call(
        functools.partial(_scan_kernel, tt=tt),
        grid=(B, nt),
        in_specs=[blk, blk, g_blk, vec],
        out_specs=[blk, vec],
        out_shape=[jax.ShapeDtypeStruct((B * T, C), BF16),
                   jax.ShapeDtypeStruct((B, 1, C), F32)],
        scratch_shapes=[pltpu.VMEM((1, C), F32), pltpu.VMEM((tt, C), F32)],
        compiler_params=_params(("parallel", "arbitrary"), 2 * tt * C * 14 + tt * C * 12),
        name="rglru_scan",
    )(a, u, g, h0)


GATE_TN = 256


def _gate_window(rb, C):
    raw = [((j * GATE_TN) // rb * rb) // LANES * LANES for j in range(C // GATE_TN)]
    ends = [((j * GATE_TN + GATE_TN - 1) // rb + 1) * rb for j in range(C // GATE_TN)]
    kw = -(-max(e - s for s, e in zip(raw, ends)) // LANES) * LANES
    kw = min(kw, C)
    starts = [min(s, C - kw) for s in raw]
    assert all(s + kw >= e for s, e in zip(starts, ends))
    return starts, kw


def _blockdiag_tiles(w):
    nblk, rb, _ = w.shape
    C = nblk * rb
    starts, kw = _gate_window(rb, C)
    dense = jax.scipy.linalg.block_diag(*[w[i] for i in range(nblk)])
    tiles = [dense[s:s + kw, j * GATE_TN:(j + 1) * GATE_TN] for j, s in enumerate(starts)]
    return jnp.stack(tiles).astype(BF16)


def _gates_kernel(x_ref, wa_ref, wx_ref, xc_ref, ba_ref, bx_ref, lam_ref, a_ref, u_ref, *, rb, kw):
    C = x_ref.shape[1]
    j = pl.program_id(1)
    start = jnp.minimum(((j * GATE_TN) // rb * rb) // LANES * LANES, C - kw)
    x = x_ref[:, pl.ds(pl.multiple_of(start, LANES), kw)]
    accs = [jnp.dot(x, w[0], preferred_element_type=F32) for w in (wa_ref, wx_ref)]
    a, u = _ep_rglru_gates(accs, [xc_ref[...]], [ba_ref[...], bx_ref[...], lam_ref[...]])
    a_ref[...] = a
    u_ref[...] = u


def _rglru_gates(xcb, xc, wa_t, wx_t, ba, bx, lam, *, rb, tm):
    rows, C = xc.shape
    nt, kw, _ = wa_t.shape
    tile = lambda: pl.BlockSpec((tm, GATE_TN), lambda i, j: (i, j))
    vec = lambda: pl.BlockSpec((1, GATE_TN), lambda i, j: (0, j))
    wspec = lambda: pl.BlockSpec((1, kw, GATE_TN), lambda i, j: (j, 0, 0))
    vmem = 2 * (tm * C * 2 + 2 * kw * GATE_TN * 2 + 3 * tm * GATE_TN * 4) + 8 * tm * GATE_TN * 4
    return pl.pallas_call(
        functools.partial(_gates_kernel, rb=rb, kw=kw),
        grid=(rows // tm, nt),
        in_specs=[pl.BlockSpec((tm, C), lambda i, j: (i, 0)), wspec(), wspec(), tile(),
                  vec(), vec(), vec()],
        out_specs=[tile(), tile()],
        out_shape=[jax.ShapeDtypeStruct((rows, C), F32)] * 2,
        compiler_params=_params(("parallel", "arbitrary"), vmem),
        name="rglru_gates",
    )(xcb, wa_t, wx_t, xc, ba.reshape(1, C), bx.reshape(1, C), lam.reshape(1, C))


def _pick(n, cands):
    for c in cands:
        if n % c == 0:
            return c
    raise ValueError(f"no tile for {n}")


def _layer(x, hist, p, rel_bias, dims):
    Tp, Bs, Ts, past = dims
    M, D = x.shape
    Ms = Bs * Ts
    k_past, v_past, ki_past, conv_buf, h0 = hist
    KV = k_past.shape[2]
    HQ = KV * GROUP * HEAD_DIM
    KVD = KV * HEAD_DIM
    DI = ki_past.shape[-1]
    C = p["conv_w"].shape[-1]
    n_in = p["w_in"].shape[1]
    HI = (n_in - HQ - 2 * KVD - DI - 2 * C - 2 * D) // (DI + 1)
    assert DI == LANES
    sizes = (HQ, KVD, KVD, HI * DI, DI, HI, C, C, D, D)
    offs = np.concatenate([[0], np.cumsum(sizes)])
    assert offs[-1] == n_in
    w_in = p["w_in"]

    def wslice(i, j=None):
        j = i if j is None else j
        return w_in[:, offs[i]:offs[j + 1]].astype(BF16)

    tm = _pick(M, (512, 256, 128, 64))
    TK = KEY_TILE

    h = _rmsnorm(x, p["norm_mix"], tm)

    def proj(w, ep, dt, tn, vecs=(), name="in_proj"):
        return _matmul(h, [w], ep, [dt], tm=tm, tn=tn, vecs=vecs, name=name)[0]

    q = proj(wslice(0), _ep_headnorm(HEAD_DIM ** -0.5 * LOG2E), BF16, _pick(HQ, (512, 256, 128)),
             vecs=[jnp.tile(p["q_norm"], HQ // HEAD_DIM)], name="in_proj_q")
    k = proj(wslice(1), _ep_headnorm(1.0), F32, _pick(KVD, (512, 256, 128)),
             vecs=[jnp.tile(p["k_norm"], KV)], name="in_proj_k")
    v = proj(wslice(2), _ep_identity, F32, _pick(KVD, (512, 256, 128)), name="in_proj_v")
    qi = proj(wslice(3), _ep_identity, BF16, _pick(HI * DI, (512, 256, 128)), name="in_proj_qi")
    kw_w = jnp.pad(wslice(4, 5), ((0, 0), (0, 2 * LANES - DI - HI)))
    kiwi = proj(kw_w, _ep_identity, F32, 2 * LANES, name="in_proj_ki")
    ki, wi = kiwi[:, :DI], kiwi[:, DI:DI + HI]
    tn_c = _pick(C, (768, 384, 128))
    xr = proj(wslice(6), _ep_identity, F32, tn_c, name="in_proj_xr")
    gr = proj(wslice(7), _ep_identity, F32, tn_c, name="in_proj_gr")
    tn_d = _pick(D, (512, 256, 128))
    ga = proj(wslice(8), _ep_identity, F32, tn_d, name="in_proj_ga")
    gb = proj(wslice(9), _ep_identity, F32, tn_d, name="in_proj_gb")

    k_bf, v_bf, ki_bf = k.astype(BF16), v.astype(BF16), ki.astype(BF16)

    assert Tp % Q_STEP == 0
    back_p = -Tp % FAR_TILE
    n_sel_p = min(TOPK_MAX, Tp // 4)
    mask_p = _select_prompt(qi, wi, jnp.pad(ki_bf[:Tp], ((0, back_p), (0, 0))), n_sel_p, Tp)
    kv_pad = lambda a: jnp.pad(a[:Tp], ((FRONT_PAD, back_p), (0, 0)))
    o_p = _attention_prompt(q, kv_pad(k_bf), kv_pad(v_bf), mask_p, _near_bias_prompt(rel_bias))

    Ls = past + Ts
    nt_s = -(-Ls // TK)
    pad_s = nt_s * TK - Ls

    def with_cache(cache, new):
        new = new[Tp:].reshape(Bs, Ts, -1)
        parts = [cache.reshape(Bs, past, -1).astype(BF16), new]
        if pad_s:
            parts.append(jnp.zeros((Bs, pad_s, new.shape[-1]), BF16))
        return jnp.concatenate(parts, axis=1)

    n_sel_s = min(TOPK_MAX, Ls // 4)
    mask_s = _select(qi[Tp:], wi[Tp:], with_cache(ki_past, ki_bf), TQ=Ts, NT=nt_s,
                     lend_a=0, lend_b=Ls, n_sel=n_sel_s, batched=True)
    nb_s = _near_bias_stream(rel_bias, Ts, TK, [Ls - ((Ls - 1) // TK) * TK])
    o_s = _attention(q[Tp:], with_cache(k_past, k_bf), with_cache(v_past, v_bf), mask_s, nb_s,
                     TQ=Ts, lend_a=0, lend_b=Ls, batched=True)
    o_a = jnp.concatenate([o_p, o_s], axis=0)

    assert C % GATE_TN == 0
    wa_t = _blockdiag_tiles(p["rg_wa"])
    wx_t = _blockdiag_tiles(p["rg_wx"])

    def griffin(row0, B, T, buf, h_init):
        tt = _pick(math.gcd(T, row0) if row0 else T, (128, 64, 32, 16, 8))
        buf8 = jnp.pad(buf.astype(F32), ((0, 0), (SUBLANES - (CONV_W - 1), 0), (0, 0)))
        xc, xcb = _conv(xr, buf8, p["conv_w"], p["conv_b"], row0=row0, B=B, T=T, tt=tt)
        rows = B * T
        tmr = _pick(rows, (512, 256, 128, 64, 32))
        a, u = _rglru_gates(xcb, xc, wa_t, wx_t, p["rg_ba"], p["rg_bx"], p["rg_lambda"],
                            rb=p["rg_wa"].shape[1], tm=tmr)
        y, h_last = _scan(a, u, gr, h_init.reshape(B, 1, C), row0=row0, B=B, T=T, tt=tt)
        tail = xr[row0:row0 + rows].reshape(B, T, C)[:, -(CONV_W - 1):]
        conv_new = jnp.concatenate([buf.astype(F32), tail], axis=1)[:, -(CONV_W - 1):]
        return y, conv_new, h_last.reshape(B, C)

    y_p, conv_p, h_p = griffin(0, 1, Tp, jnp.zeros((1, CONV_W - 1, C), F32), jnp.zeros((1, C), F32))
    y_s, conv_s, h_s = griffin(Tp, Bs, Ts, conv_buf, h0)
    o_b = jnp.concatenate([y_p, y_s], axis=0)

    part = _matmul(o_a, [p["w_out_attn"].astype(BF16)], _ep_gate, [F32], tm=tm, tn=tn_d,
                   exts=[ga], name="out_attn")[0]
    merged = _matmul(o_b, [p["w_out_rg"].astype(BF16)], _ep_gate_add, [BF16], tm=tm, tn=tn_d,
                     exts=[gb, part], name="out_rg")[0]
    x1 = _matmul(merged, [p["w_o"].astype(BF16)], _ep_residual, [F32], tm=tm, tn=tn_d,
                 exts=[x], name="w_o")[0]
    hf = _rmsnorm(x1, p["norm_ffn"], tm)
    FF = p["ffn_w1"].shape[1]
    tn_f = _pick(FF, (512, 256, 128))
    act = _matmul(hf, [p["ffn_w1"].astype(BF16), p["ffn_w3"].astype(BF16)], _ep_swiglu, [BF16],
                  tm=tm, tn=tn_f, name="ffn_up")[0]
    tk_f = _pick(FF, (5504, 2816, 2048, 1024, 512, 256, 128))
    x2 = _matmul(act, [p["ffn_w2"].astype(BF16)], _ep_residual, [F32],
                 tm=_pick(M, (768, 512, 256, 128, 64)), tn=tn_d, tk=tk_f, exts=[x1],
                 name="ffn_down")[0]

    new_p = (k[:Tp].reshape(1, Tp, KV, HEAD_DIM), v[:Tp].reshape(1, Tp, KV, HEAD_DIM),
             ki[:Tp].reshape(1, Tp, DI), conv_p, h_p)
    new_s = (k[Tp:].reshape(Bs, Ts, KV, HEAD_DIM), v[Tp:].reshape(Bs, Ts, KV, HEAD_DIM),
             ki[Tp:].reshape(Bs, Ts, DI), conv_s, h_s)
    return x2, new_p, new_s


def kernel(x_prompt, x_sample, cache_k, cache_v, cache_kidx, state_conv, state_rglru, norm_mix, w_in, q_norm, k_norm, rel_bias, conv_w, conv_b, rg_wa, rg_ba, rg_wx, rg_bx, rg_lambda, w_out_attn, w_out_rg, w_o, norm_ffn, ffn_w1, ffn_w3, ffn_w2):
    Bp, Tp, D = x_prompt.shape
    Bs, Ts, _ = x_sample.shape
    assert Bp == 1 and Tp % CHUNK == 0
    depth = w_in.shape[0]
    past = cache_k.shape[2]
    x = jnp.concatenate([x_prompt.reshape(Tp, D), x_sample.reshape(Bs * Ts, D)], axis=0)
    outs_p, outs_s = [], []
    for l in range(depth):
        p = dict(norm_mix=norm_mix[l], w_in=w_in[l], q_norm=q_norm[l], k_norm=k_norm[l],
                 conv_w=conv_w[l], conv_b=conv_b[l], rg_wa=rg_wa[l], rg_ba=rg_ba[l],
                 rg_wx=rg_wx[l], rg_bx=rg_bx[l], rg_lambda=rg_lambda[l],
                 w_out_attn=w_out_attn[l], w_out_rg=w_out_rg[l], w_o=w_o[l],
                 norm_ffn=norm_ffn[l], ffn_w1=ffn_w1[l], ffn_w3=ffn_w3[l], ffn_w2=ffn_w2[l])
        hist = (cache_k[l], cache_v[l], cache_kidx[l], state_conv[l], state_rglru[l])
        x, new_p, new_s = _layer(x, hist, p, rel_bias, (Tp, Bs, Ts, past))
        outs_p.append(new_p)
        outs_s.append(new_s)
    stack = lambda outs, i: jnp.stack([o[i] for o in outs])
    return (x[:Tp].reshape(1, Tp, D), x[Tp:].reshape(Bs, Ts, D),
            *[stack(outs_p, i) for i in range(5)],
            *[stack(outs_s, i) for i in range(5)])
```

```python
import functools
import math

import numpy as np
import jax
import jax.numpy as jnp
from jax import lax
from jax.experimental import pallas as pl
from jax.experimental.pallas import tpu as pltpu

F32 = jnp.float32
BF16 = jnp.bfloat16

CHUNK = 64
HEAD_DIM = 128
GROUP = 4
TOPK_MAX = 256
N_BUCKETS = 32
REL_MAX_DIST = 128
RG_C = 8.0
CONV_W = 4
EPS = 1e-6

LANES = 128
SUBLANES = 8
KEY_TILE = 256
Q_STEP = 2 * CHUNK
FAR_TILE = 512
FRONT_PAD = FAR_TILE
ONES_ROWS = 16
LOG2E = math.log2(math.e)
VMEM_CAP = 56 << 20
NEG = -1e30


def _params(sem, vmem_bytes):
    limit = min(max(int(vmem_bytes) + (6 << 20), 24 << 20), VMEM_CAP)
    return pltpu.CompilerParams(dimension_semantics=sem, vmem_limit_bytes=limit)


def _rmsnorm_kernel(x_ref, g_ref, o_ref):
    x = x_ref[...]
    ms = jnp.mean(x * x, axis=-1, keepdims=True)
    o_ref[...] = (x * lax.rsqrt(ms + EPS) * g_ref[...]).astype(o_ref.dtype)


def _rmsnorm(x, g, tm):
    M, D = x.shape
    tm = min(tm, 256)
    return pl.pallas_call(
        _rmsnorm_kernel,
        grid=(M // tm,),
        in_specs=[pl.BlockSpec((tm, D), lambda i: (i, 0)),
                  pl.BlockSpec((1, D), lambda i: (0, 0))],
        out_specs=pl.BlockSpec((tm, D), lambda i: (i, 0)),
        out_shape=jax.ShapeDtypeStruct((M, D), BF16),
        compiler_params=_params(("parallel",), 2 * tm * D * 6 + 2 * tm * D * 4),
        name="rmsnorm",
    )(x, g.reshape(1, D))


def _mm_kernel(*refs, n_w, n_ext, n_vec, n_out, nk, epilogue):
    x_ref = refs[0]
    w_refs = refs[1:1 + n_w]
    p = 1 + n_w
    ext_refs = refs[p:p + n_ext]
    p += n_ext
    vec_refs = refs[p:p + n_vec]
    p += n_vec
    out_refs = refs[p:p + n_out]
    acc_refs = refs[p + n_out:]

    def finish(accs):
        res = epilogue(accs, [e[...] for e in ext_refs], [v[...] for v in vec_refs])
        for o, r in zip(out_refs, res):
            o[...] = r.astype(o.dtype)

    if nk == 1:
        finish([jnp.dot(x_ref[...], w[...], preferred_element_type=F32) for w in w_refs])
        return

    k = pl.program_id(2)

    @pl.when(k == 0)
    def _():
        for a in acc_refs:
            a[...] = jnp.zeros_like(a)

    for a, w in zip(acc_refs, w_refs):
        a[...] += jnp.dot(x_ref[...], w[...], preferred_element_type=F32)

    @pl.when(k == nk - 1)
    def _():
        finish([a[...] for a in acc_refs])


def _matmul(x, ws, epilogue, out_dtypes, *, tm, tn, tk=None, exts=(), vecs=(), name="matmul"):
    M, K = x.shape
    n = ws[0].shape[1]
    tk = K if tk is None else tk
    nk = K // tk
    assert M % tm == 0 and n % tn == 0 and K % tk == 0
    x_map = lambda j, i, k: (i, k)
    w_map = lambda j, i, k: (k, j)
    mn_map = lambda j, i, k: (i, j)
    in_specs = [pl.BlockSpec((tm, tk), x_map)]
    in_specs += [pl.BlockSpec((tk, tn), w_map) for _ in ws]
    in_specs += [pl.BlockSpec((tm, tn), mn_map) for _ in exts]
    in_specs += [pl.BlockSpec((1, tn), lambda j, i, k: (0, j)) for _ in vecs]
    out_specs = [pl.BlockSpec((tm, tn), mn_map) for _ in out_dtypes]
    out_shape = [jax.ShapeDtypeStruct((M, n), dt) for dt in out_dtypes]
    scratch = [pltpu.VMEM((tm, tn), F32) for _ in ws] if nk > 1 else []
    vmem = (2 * (tm * tk * 2 + len(ws) * tk * tn * 2 + (len(exts) + len(out_dtypes)) * tm * tn * 4)
            + 3 * len(ws) * tm * tn * 4)
    kern = functools.partial(_mm_kernel, n_w=len(ws), n_ext=len(exts), n_vec=len(vecs),
                             n_out=len(out_dtypes), nk=nk, epilogue=epilogue)
    outs = pl.pallas_call(
        kern,
        grid=(n // tn, M // tm, nk),
        in_specs=in_specs,
        out_specs=out_specs,
        out_shape=out_shape,
        scratch_shapes=scratch,
        compiler_params=_params(("parallel", "parallel", "arbitrary"), vmem),
        name=name,
    )(x, *ws, *exts, *[v.reshape(1, -1) for v in vecs])
    return outs


def _ep_identity(accs, exts, vecs):
    return (accs[0],)


def _ep_headnorm(scale):
    def ep(accs, exts, vecs):
        a, g = accs[0], vecs[0]
        outs = []
        for j in range(a.shape[1] // HEAD_DIM):
            aj = a[:, j * HEAD_DIM:(j + 1) * HEAD_DIM]
            ms = jnp.mean(aj * aj, axis=-1, keepdims=True)
            outs.append(aj * lax.rsqrt(ms + EPS) * g[:, j * HEAD_DIM:(j + 1) * HEAD_DIM])
        y = jnp.concatenate(outs, axis=1)
        return (y * scale if scale != 1.0 else y,)
    return ep


def _ep_gate(accs, exts, vecs):
    return (jax.nn.sigmoid(exts[0]) * accs[0],)


def _ep_gate_add(accs, exts, vecs):
    return (exts[1] + jax.nn.sigmoid(exts[0]) * accs[0],)


def _ep_residual(accs, exts, vecs):
    return (exts[0] + accs[0],)


def _ep_swiglu(accs, exts, vecs):
    return (jax.nn.silu(accs[0]) * accs[1],)


def _softplus(x):
    return jnp.maximum(x, 0.0) + jnp.log1p(jnp.exp(-jnp.abs(x)))


def _ep_rglru_gates(accs, exts, vecs):
    xc = exts[0]
    r = jax.nn.sigmoid(accs[0] + vecs[0])
    i = jax.nn.sigmoid(accs[1] + vecs[1])
    log_a = -RG_C * r * _softplus(-vecs[2])
    a = jnp.exp(log_a)
    u = jnp.sqrt(1.0 - a * a) * (i * xc)
    return (a, u)


INT_MAX = 2 ** 31 - 1


def _open_rows(lo, hi):
    return jnp.max(jnp.where(lo < hi, 1.0, 0.0))


def _kth_largest_key(count_ge, lo, hi, ksel, n_valid):
    def body(st):
        lo, hi, c_lo, _ = st
        active = lo < hi
        mid = (lo >> 1) + (hi >> 1) + ((lo | hi) & 1)
        c = count_ge(mid)
        ge = c >= ksel
        lo_n = jnp.where(ge, mid, lo)
        c_n = jnp.where(ge, c, c_lo)
        hi_n = jnp.where(c == ksel, mid, jnp.where(ge, hi, mid - 1))
        lo = jnp.where(active, lo_n, lo)
        c_lo = jnp.where(active, c_n, c_lo)
        hi = jnp.where(active, hi_n, hi)
        return lo, hi, c_lo, _open_rows(lo, hi)

    tau, _, c_tau, _ = lax.while_loop(lambda st: st[3] > 0.5, body,
                                      (lo, hi, n_valid, _open_rows(lo, hi)))
    return tau, c_tau


def _tie_cut(count_tie_le, need, n_keys):
    def body(st):
        lo, hi, _ = st
        active = lo < hi
        mid = (lo + hi) >> 1
        ok = count_tie_le(mid) >= need
        hi = jnp.where(active, jnp.where(ok, mid, hi), hi)
        lo = jnp.where(active, jnp.where(ok, lo, mid + 1), lo)
        return lo, hi, _open_rows(lo, hi)

    lo = jnp.zeros(need.shape, jnp.int32)
    hi = jnp.full(need.shape, n_keys - 1, jnp.int32)
    cut, _, _ = lax.while_loop(lambda st: st[2] > 0.5, body, (lo, hi, _open_rows(lo, hi)))
    return cut


def _threshold_and_cut(key_ref, cut_ref, nt, lo, hi, ksel, n_valid):
    _, TQ, TK = key_ref.shape
    nl = TK // LANES
    lane = lax.broadcasted_iota(jnp.int32, (TQ, LANES), 1)

    def count(indicator):
        def body(j, c):
            kk = key_ref[j]
            for l in range(nl):
                c = c + indicator(kk[:, l * LANES:(l + 1) * LANES], j * TK + l * LANES)
            return c
        c = lax.fori_loop(0, nt, body, jnp.zeros((TQ, LANES), F32))
        return jnp.broadcast_to(jnp.sum(c, axis=1, keepdims=True), (TQ, LANES))

    tau, c_tau = _kth_largest_key(
        lambda mid: count(lambda kk, base: jnp.where(kk >= mid, 1.0, 0.0)), lo, hi, ksel, n_valid)
    cut_ref[...] = jnp.full((TQ, LANES), INT_MAX, jnp.int32)

    tied = jnp.max(jnp.where(c_tau > ksel, 1.0, 0.0)) > 0.5

    @pl.when(tied)
    def _():
        need = ksel - count(lambda kk, base: jnp.where(kk > tau, 1.0, 0.0))
        cut_ref[...] = _tie_cut(
            lambda mid: count(lambda kk, base: jnp.where(
                kk == tau, jnp.where(lane + base <= mid, 1.0, 0.0), 0.0)),
            need, nt * TK)

    return tau, cut_ref[...], tied


def _selected(kk, base, tau, cut):
    lane = lax.broadcasted_iota(jnp.int32, kk.shape, 1)
    tie = jnp.where(kk == tau, jnp.where(lane + base <= cut, 1, 0), 0)
    return jnp.where(kk > tau, 1, tie)


def _write_selection(write_tile, nt, tied):
    @pl.when(tied)
    def _():
        lax.fori_loop(0, nt, lambda j, _: write_tile(j, True) or 0, 0)

    @pl.when(jnp.logical_not(tied))
    def _():
        lax.fori_loop(0, nt, lambda j, _: write_tile(j, False) or 0, 0)
def _select_kernel(qi_ref, wi_ref, ki_ref, mask_ref, q2_ref, w2_ref, key_ref, cut_ref, *,
                   TQ, TK, NT, HI, lend_a, lend_b, n_sel, wscale):
    nl = TK // LANES
    lend = lend_a * pl.program_id(0) + lend_b
    nt = (lend + TK - 1) // TK
    ksel = jnp.minimum(n_sel, lend).astype(F32)

    wi = wi_ref[...] * wscale
    for h in range(HI):
        q2_ref[h * TQ:(h + 1) * TQ, :] = qi_ref[:, h * LANES:(h + 1) * LANES]
        w2_ref[h * TQ:(h + 1) * TQ, :] = jnp.broadcast_to(wi[:, h:h + 1], (TQ, LANES))

    lane = lax.broadcasted_iota(jnp.int32, (TQ, TK), 1)
    int_min = jnp.int32(-2 ** 31)

    def to_key(s):
        b = pltpu.bitcast(s, jnp.int32)
        return b ^ ((b >> 31) & jnp.int32(0x7FFFFFFF))

    def score_tile(j, carry):
        smin, smax = carry
        start = pl.multiple_of(j * TK, TK)
        kt = ki_ref[pl.ds(start, TK), :]
        s = lax.dot_general(q2_ref[...], kt, (((1,), (1,)), ((), ())),
                            preferred_element_type=F32)
        cols = []
        for l in range(nl):
            acc = jnp.zeros((TQ, LANES), F32)
            for h in range(HI):
                acc = acc + (jnp.maximum(s[h * TQ:(h + 1) * TQ, l * LANES:(l + 1) * LANES], 0.0)
                             * w2_ref[h * TQ:(h + 1) * TQ, :])
            cols.append(acc)
        sc = jnp.concatenate(cols, axis=1)
        valid = (lane + j * TK) < lend
        key_ref[j] = jnp.where(valid, to_key(sc), int_min)
        lo_s = jnp.where(valid, sc, jnp.inf)
        hi_s = jnp.where(valid, sc, -jnp.inf)
        for l in range(nl):
            smin = jnp.minimum(smin, lo_s[:, l * LANES:(l + 1) * LANES])
            smax = jnp.maximum(smax, hi_s[:, l * LANES:(l + 1) * LANES])
        return smin, smax

    smin, smax = lax.fori_loop(
        0, nt, score_tile,
        (jnp.full((TQ, LANES), jnp.inf, F32), jnp.full((TQ, LANES), -jnp.inf, F32)))
    lo = to_key(jnp.broadcast_to(jnp.min(smin, axis=1, keepdims=True), (TQ, LANES)))
    hi = to_key(jnp.broadcast_to(jnp.max(smax, axis=1, keepdims=True), (TQ, LANES)))

    n_valid = jnp.full((TQ, LANES), lend, jnp.int32).astype(F32)
    tau, cut, tied = _threshold_and_cut(key_ref, cut_ref, nt, lo, hi,
                                        jnp.full((TQ, LANES), ksel, F32), n_valid)

    def write(j, exact_ties):
        kk = key_ref[j]
        sel = []
        for l in range(nl):
            slab = kk[:, l * LANES:(l + 1) * LANES]
            sel.append(_selected(slab, j * TK + l * LANES, tau, cut) if exact_ties
                       else jnp.where(slab >= tau, 1, 0))
        mask_ref[0, j] = jnp.concatenate(sel, axis=1).astype(jnp.int8)

    _write_selection(write, nt, tied)

    def clear(j, _):
        mask_ref[0, j] = jnp.zeros((TQ, TK), jnp.int8)
        return 0

    lax.fori_loop(nt, NT, clear, 0)


def _select(qi, wi, ki, *, TQ, NT, lend_a, lend_b, n_sel, batched):
    TK = KEY_TILE
    Mq = qi.shape[0]
    HI = wi.shape[1]
    steps = Mq // TQ
    if batched:
        ki_spec = pl.BlockSpec((None, NT * TK, LANES), lambda i: (i, 0, 0))
    else:
        ki_spec = pl.BlockSpec((NT * TK, LANES), lambda i: (0, 0))
    kern = functools.partial(_select_kernel, TQ=TQ, TK=TK, NT=NT, HI=HI, lend_a=lend_a,
                             lend_b=lend_b, n_sel=n_sel,
                             wscale=float(HI ** -0.5 * LANES ** -0.5))
    vmem = (2 * (TQ * HI * LANES * 2 + NT * TK * LANES * 2 + NT * TQ * TK)
            + HI * TQ * LANES * 6 + NT * TQ * TK * 4 + 3 * HI * TQ * TK * 4)
    return pl.pallas_call(
        kern,
        grid=(steps,),
        in_specs=[pl.BlockSpec((TQ, HI * LANES), lambda i: (i, 0)),
                  pl.BlockSpec((TQ, HI), lambda i: (i, 0)),
                  ki_spec],
        out_specs=pl.BlockSpec((1, NT, TQ, TK), lambda i: (i, 0, 0, 0)),
        out_shape=jax.ShapeDtypeStruct((steps, NT, TQ, TK), jnp.int8),
        scratch_shapes=[pltpu.VMEM((HI * TQ, LANES), BF16),
                        pltpu.VMEM((HI * TQ, LANES), F32),
                        pltpu.VMEM((NT, TQ, TK), jnp.int32),
                        pltpu.VMEM((TQ, LANES), jnp.int32)],
        compiler_params=_params(("parallel",), vmem),
        name="dsa_select",
    )(qi, wi, ki)


def _select_prompt_kernel(qi_ref, wi_ref, ki_ref, mask_ref, q2_ref, w2_ref, key_ref, acc_ref,
                          cut_ref, *,
                          HI, HG, NT, n_sel, wscale):
    TQ, TK = Q_STEP, FAR_TILE
    nl = TK // LANES
    pad_tiles = FRONT_PAD // LANES
    RB = 64
    c2 = pl.program_id(0)
    row = lax.broadcasted_iota(jnp.int32, (TQ, LANES), 0)
    lend = jnp.where(row < CHUNK, c2 * TQ + CHUNK, c2 * TQ + TQ)
    nt = (c2 * TQ + TQ + TK - 1) // TK
    ksel = jnp.minimum(n_sel, lend).astype(F32)

    wi = wi_ref[...] * wscale
    for h in range(HI):
        q2_ref[h * TQ:(h + 1) * TQ, :] = qi_ref[:, h * LANES:(h + 1) * LANES]
        w2_ref[h * TQ:(h + 1) * TQ, :] = jnp.broadcast_to(wi[:, h:h + 1], (TQ, LANES))

    lane = lax.broadcasted_iota(jnp.int32, (TQ, TK), 1)
    lend_t = jnp.concatenate([lend] * nl, axis=1)
    int_min = jnp.int32(-2 ** 31)

    def to_key(s):
        b = pltpu.bitcast(s, jnp.int32)
        return b ^ ((b >> 31) & jnp.int32(0x7FFFFFFF))

    def score_tile(j, carry):
        smin, smax = carry
        start = pl.multiple_of(j * TK, TK)
        kt = ki_ref[pl.ds(start, TK), :]
        for hg in range(HI // HG):
            s = lax.dot_general(q2_ref[hg * HG * TQ:(hg + 1) * HG * TQ, :], kt,
                                (((1,), (1,)), ((), ())), preferred_element_type=F32)
            for r in range(TQ // RB):
                cs = [None] * nl
                for h in range(HG):
                    r0 = h * TQ + r * RB
                    w = w2_ref[(hg * HG) * TQ + r0:(hg * HG) * TQ + r0 + RB, :]
                    for l in range(nl):
                        term = jnp.maximum(s[r0:r0 + RB, l * LANES:(l + 1) * LANES], 0.0) * w
                        cs[l] = term if cs[l] is None else cs[l] + term
                for l in range(nl):
                    if hg == 0:
                        acc_ref[r * RB:(r + 1) * RB, l * LANES:(l + 1) * LANES] = cs[l]
                    else:
                        acc_ref[r * RB:(r + 1) * RB, l * LANES:(l + 1) * LANES] += cs[l]
        sc = acc_ref[...]
        valid = (lane + j * TK) < lend_t
        key_ref[j] = jnp.where(valid, to_key(sc), int_min)
        lo_s = jnp.where(valid, sc, jnp.inf)
        hi_s = jnp.where(valid, sc, -jnp.inf)
        for l in range(nl):
            smin = jnp.minimum(smin, lo_s[:, l * LANES:(l + 1) * LANES])
            smax = jnp.maximum(smax, hi_s[:, l * LANES:(l + 1) * LANES])
        return smin, smax

    smin, smax = lax.fori_loop(
        0, nt, score_tile,
        (jnp.full((TQ, LANES), jnp.inf, F32), jnp.full((TQ, LANES), -jnp.inf, F32)))
    lo = to_key(jnp.broadcast_to(jnp.min(smin, axis=1, keepdims=True), (TQ, LANES)))
    hi = to_key(jnp.broadcast_to(jnp.max(smax, axis=1, keepdims=True), (TQ, LANES)))

    tau, cut, tied = _threshold_and_cut(key_ref, cut_ref, nt, lo, hi, ksel, lend.astype(F32))

    neg_tile = jnp.full((TQ, LANES), NEG, BF16)
    for i in range(pad_tiles):
        mask_ref[0, i] = neg_tile

    def write(j, exact_ties):
        kk = key_ref[j]
        for i in range(nl):
            slab = kk[:, i * LANES:(i + 1) * LANES]
            keep = (_selected(slab, j * TK + i * LANES, tau, cut) > 0) if exact_ties else slab >= tau
            mask_ref[0, pad_tiles + nl * j + i] = jnp.where(keep, 0.0, NEG).T.astype(BF16)

    _write_selection(write, nt, tied)

    def clear(j, _):
        for i in range(nl):
            mask_ref[0, pad_tiles + nl * j + i] = neg_tile
        return 0

    lax.fori_loop(nt, NT, clear, 0)


def _select_prompt(qi, wi, ki, n_sel, T):
    TQ, TK = Q_STEP, FAR_TILE
    HI = wi.shape[1]
    HG = 4 if HI % 4 == 0 else 1
    NT = ki.shape[0] // TK
    steps = T // TQ
    ntile = (FRONT_PAD + NT * TK) // LANES
    kern = functools.partial(_select_prompt_kernel, HI=HI, HG=HG, NT=NT, n_sel=n_sel,
                             wscale=float(HI ** -0.5 * LANES ** -0.5))
    vmem = (2 * (TQ * HI * LANES * 2 + NT * TK * LANES * 2 + ntile * TQ * LANES * 2)
            + HI * TQ * LANES * 6 + NT * TQ * TK * 4 + TQ * TK * 4 + 4 * HG * TQ * TK * 4)
    return pl.pallas_call(
        kern,
        grid=(steps,),
        in_specs=[pl.BlockSpec((TQ, HI * LANES), lambda i: (i, 0)),
                  pl.BlockSpec((TQ, HI), lambda i: (i, 0)),
                  pl.BlockSpec((NT * TK, LANES), lambda i: (0, 0))],
        out_specs=pl.BlockSpec((1, ntile, TQ, LANES), lambda i: (i, 0, 0, 0)),
        out_shape=jax.ShapeDtypeStruct((steps, ntile, TQ, LANES), BF16),
        scratch_shapes=[pltpu.VMEM((HI * TQ, LANES), BF16),
                        pltpu.VMEM((HI * TQ, LANES), F32),
                        pltpu.VMEM((NT, TQ, TK), jnp.int32),
                        pltpu.VMEM((TQ, TK), F32),
                        pltpu.VMEM((TQ, LANES), jnp.int32)],
        compiler_params=_params(("parallel",), vmem),
        name="dsa_select_prompt",
    )(qi, wi, ki)


def _attn_kernel(q_ref, k_ref, v_ref, mask_ref, nb_ref, o_ref, q2_ref, m_ref, l_ref, acc_ref, *,
                 TQ, TK, G, lend_a, lend_b, step_axis):
    nl = TK // LANES
    R = G * TQ
    lend = lend_a * pl.program_id(step_axis) + lend_b
    jl = (lend - 1) // TK

    for g in range(G):
        q2_ref[g * TQ:(g + 1) * TQ, :] = q_ref[:, g * HEAD_DIM:(g + 1) * HEAD_DIM]
    m_ref[...] = jnp.full((R, LANES), NEG, F32)
    l_ref[...] = jnp.zeros((R, LANES), F32)
    acc_ref[...] = jnp.zeros((R, HEAD_DIM), F32)

    def tile(j, half):
        start = pl.multiple_of(j * TK, TK)
        kt = k_ref[pl.ds(start, TK), :]
        vt = v_ref[pl.ds(start, TK), :]
        s = lax.dot_general(q2_ref[...], kt, (((1,), (1,)), ((), ())),
                            preferred_element_type=F32)
        madd = jnp.where(mask_ref[0, j].astype(jnp.int32) != 0, 0.0, NEG)
        rows = []
        for g in range(G):
            sg = s[g * TQ:(g + 1) * TQ, :]
            if half is not None:
                sg = sg + nb_ref[0, g, :, half * TK:(half + 1) * TK]
            rows.append(sg + madd)
        s = jnp.concatenate(rows, axis=0)
        m_old = m_ref[...]
        m_cur = s[:, :LANES]
        for l in range(1, nl):
            m_cur = jnp.maximum(m_cur, s[:, l * LANES:(l + 1) * LANES])
        m_new = jnp.maximum(m_old, jnp.broadcast_to(jnp.max(m_cur, axis=1, keepdims=True), (R, LANES)))
        alpha = jnp.exp2(m_old - m_new)
        p = jnp.exp2(s - jnp.concatenate([m_new] * nl, axis=1))
        psum = p[:, :LANES]
        for l in range(1, nl):
            psum = psum + p[:, l * LANES:(l + 1) * LANES]
        l_ref[...] = alpha * l_ref[...] + jnp.broadcast_to(
            jnp.sum(psum, axis=1, keepdims=True), (R, LANES))
        acc_ref[...] = alpha * acc_ref[...] + jnp.dot(
            p.astype(BF16), vt, preferred_element_type=F32)
        m_ref[...] = m_new

    def far(j, _):
        tile(j, None)
        return 0

    lax.fori_loop(0, jnp.maximum(jl - 1, 0), far, 0)

    @pl.when(jl >= 1)
    def _():
        tile(jl - 1, 0)

    tile(jl, 1)

    o = acc_ref[...] / l_ref[...]
    for g in range(G):
        o_ref[:, g * HEAD_DIM:(g + 1) * HEAD_DIM] = o[g * TQ:(g + 1) * TQ, :].astype(o_ref.dtype)


def _attention(q, k, v, mask, nb, *, TQ, lend_a, lend_b, batched):
    TK = KEY_TILE
    G = GROUP
    Mq, HD = q.shape
    KV = HD // (G * HEAD_DIM)
    steps = Mq // TQ
    NT = mask.shape[1]
    L = k.shape[-2]
    P = nb.shape[0]
    if batched:
        grid = (steps, KV)
        q_map = lambda b, h: (b, h)
        kv_spec = pl.BlockSpec((None, L, HEAD_DIM), lambda b, h: (b, 0, h))
        mask_map = lambda b, h: (b, 0, 0, 0)
        nb_map = lambda b, h: (0, h, 0, 0)
        step_axis = 0
    else:
        grid = (KV, steps)
        q_map = lambda h, c: (c, h)
        kv_spec = pl.BlockSpec((L, HEAD_DIM), lambda h, c: (0, h))
        mask_map = lambda h, c: (c, 0, 0, 0)
        nb_map = lambda h, c: (c % P, h, 0, 0)
        step_axis = 1
    kern = functools.partial(_attn_kernel, TQ=TQ, TK=TK, G=G, lend_a=lend_a, lend_b=lend_b,
                             step_axis=step_axis)
    R = G * TQ
    vmem = (2 * (2 * TQ * G * HEAD_DIM * 2 + 2 * L * HEAD_DIM * 2 + NT * TQ * TK + G * TQ * 2 * TK * 4)
            + R * LANES * 14 + 6 * R * TK * 4)
    return pl.pallas_call(
        kern,
        grid=grid,
        in_specs=[pl.BlockSpec((TQ, G * HEAD_DIM), q_map),
                  kv_spec, kv_spec,
                  pl.BlockSpec((1, NT, TQ, TK), mask_map),
                  pl.BlockSpec((1, G, TQ, 2 * TK), nb_map)],
        out_specs=pl.BlockSpec((TQ, G * HEAD_DIM), q_map),
        out_shape=jax.ShapeDtypeStruct((Mq, HD), BF16),
        scratch_shapes=[pltpu.VMEM((R, HEAD_DIM), BF16),
                        pltpu.VMEM((R, LANES), F32),
                        pltpu.VMEM((R, LANES), F32),
                        pltpu.VMEM((R, HEAD_DIM), F32)],
        compiler_params=_params(("parallel", "arbitrary"), vmem),
        name="dsa_attention",
    )(q, k, v, mask, nb)


def _attn_prompt_kernel(q_ref, k_ref, v_ref, mask_ref, nb_ref, o_ref,
                        q2_ref, m_ref, acc_ref, sa_ref, sb_ref, p_ref):
    TQ, TK, G = Q_STEP, FAR_TILE, GROUP
    R = G * TQ
    NW = 2 * TQ
    c2 = pl.program_id(1)
    far_len = TQ * (c2 - 1)
    nfar = jnp.maximum((far_len + TK - 1) // TK, 0)
    nt_dims = (((1,), (1,)), ((), ()))

    for g in range(G):
        q2_ref[g * TQ:(g + 1) * TQ, :] = q_ref[:, g * HEAD_DIM:(g + 1) * HEAD_DIM]
    m_ref[...] = jnp.full((1, R), NEG, F32)
    acc_ref[...] = jnp.zeros(acc_ref.shape, F32)

    def far_start(j):
        return pl.multiple_of(jnp.maximum(far_len - TK * j, 0), LANES)

    def logits(start, width):
        return lax.dot_general(k_ref[pl.ds(start, width), :], q2_ref[...], nt_dims,
                               preferred_element_type=F32)

    def fold8(x, op):
        y = x[:SUBLANES]
        for i in range(1, x.shape[0] // SUBLANES):
            y = op(y, x[i * SUBLANES:(i + 1) * SUBLANES])
        return y

    def update(s_ref, start, width, biased):
        t0 = start // LANES
        CH = 64
        mx = jnp.full((SUBLANES, R), NEG, F32)
        for r in range(width // CH):
            rows = slice(r * CH, (r + 1) * CH)
            off = (r * CH) % LANES
            madd = mask_ref[0, t0 + (r * CH) // LANES, off:off + CH, :].astype(F32)
            x = s_ref[rows, :] + jnp.concatenate([madd] * G, axis=1)
            if biased:
                x = x + nb_ref[rows, :]
            s_ref[rows, :] = x
            mx = jnp.maximum(mx, fold8(x, jnp.maximum))
        m_old = m_ref[...]
        m_new = jnp.maximum(m_old, jnp.max(mx, axis=0, keepdims=True))
        alpha = jnp.exp2(m_old - m_new)
        for r in range(width // CH):
            rows = slice(r * CH, (r + 1) * CH)
            p_ref[rows, :] = jnp.exp2((s_ref[rows, :] - m_new).astype(BF16))
        vt = jnp.concatenate([v_ref[t0 + i] for i in range(width // LANES)], axis=1)
        acc_ref[...] = alpha * acc_ref[...] + jnp.dot(
            vt, p_ref[0:width, :], preferred_element_type=F32)
        m_ref[...] = m_new

    sa_ref[...] = logits(far_start(0), TK)
    near = pl.multiple_of(TQ * c2 + FRONT_PAD - TQ, LANES)
    sb_ref[0:NW, :] = logits(near, NW)
    update(sb_ref, near, NW, True)

    def pair(j):
        sb_ref[...] = logits(far_start(j + 1), TK)
        update(sa_ref, far_start(j), TK, False)
        sa_ref[...] = logits(far_start(j + 2), TK)
        update(sb_ref, far_start(j + 1), TK, False)

    def quad(i, _):
        pair(4 * i)
        pair(4 * i + 2)
        return 0

    nquad = nfar // 4
    lax.fori_loop(0, nquad, quad, 0)

    def rest(i, _):
        pair(4 * nquad + 2 * i)
        return 0

    lax.fori_loop(0, (nfar - 4 * nquad + 1) // 2, rest, 0)

    o = acc_ref[0:HEAD_DIM, :] / acc_ref[HEAD_DIM:HEAD_DIM + 1, :]
    for g in range(G):
        o_ref[:, g * HEAD_DIM:(g + 1) * HEAD_DIM] = o[:, g * TQ:(g + 1) * TQ].T.astype(o_ref.dtype)


def _attention_prompt(q, k, v, mask, nb):
    TQ, TK, G = Q_STEP, FAR_TILE, GROUP
    HD = q.shape[1]
    KV = HD // (G * HEAD_DIM)
    steps = mask.shape[0]
    T = steps * TQ
    ntile = mask.shape[1]
    Lp = k.shape[0]
    assert Lp == ntile * LANES and Lp >= FRONT_PAD + T and TQ == LANES
    R = G * TQ
    VR = HEAD_DIM + ONES_ROWS
    vt = v.reshape(ntile, LANES, KV, HEAD_DIM).transpose(2, 0, 3, 1)
    vt = jnp.concatenate([vt, jnp.ones((KV, ntile, ONES_ROWS, LANES), BF16)], axis=2)
    nbt = nb.reshape(KV, G, TQ, 2 * TQ).transpose(0, 3, 1, 2).reshape(KV, 2 * TQ, R)
    vmem = (2 * (2 * TQ * G * HEAD_DIM * 2 + 2 * Lp * HEAD_DIM * 2 + ntile * TQ * LANES * 2
                 + 2 * TQ * R * 4)
            + R * LANES * 6 + 2 * R * TK * 4 + 5 * R * TK * 4)
    q_map = lambda h, c: (c, h)
    return pl.pallas_call(
        _attn_prompt_kernel,
        grid=(KV, steps),
        in_specs=[pl.BlockSpec((TQ, G * HEAD_DIM), q_map),
                  pl.BlockSpec((Lp, HEAD_DIM), lambda h, c: (0, h)),
                  pl.BlockSpec((None, ntile, VR, LANES), lambda h, c: (h, 0, 0, 0)),
                  pl.BlockSpec((1, ntile, LANES, TQ), lambda h, c: (c, 0, 0, 0)),
                  pl.BlockSpec((None, 2 * TQ, R), lambda h, c: (h, 0, 0))],
        out_specs=pl.BlockSpec((TQ, G * HEAD_DIM), q_map),
        out_shape=jax.ShapeDtypeStruct((T, HD), BF16),
        scratch_shapes=[pltpu.VMEM((R, HEAD_DIM), BF16),
                        pltpu.VMEM((1, R), F32),
                        pltpu.VMEM((VR, R), F32),
                        pltpu.VMEM((TK, R), F32),
                        pltpu.VMEM((TK, R), F32),
                        pltpu.VMEM((TK, R), BF16)],
        compiler_params=_params(("parallel", "arbitrary"), vmem),
        name="dsa_attention_prompt",
    )(q, k, vt, mask, nbt)


def _t5_bucket_np(rel):
    nb = N_BUCKETS // 2
    max_exact = nb // 2
    side = np.where(rel > 0, nb, 0)
    n = np.abs(rel)
    nf = np.maximum(n, 1).astype(np.float32)
    large = max_exact + (np.log(nf / np.float32(max_exact))
                         / np.float32(math.log(REL_MAX_DIST / max_exact))
                         * np.float32(nb - max_exact)).astype(np.int32)
    large = np.minimum(large, nb - 1)
    return side + np.where(n < max_exact, n, large)


def _near_bias(rel_bias, rel, far_rel):
    far = int(_t5_bucket_np(np.array(far_rel)))
    assert far_rel < 0 and far == int(_t5_bucket_np(np.array(-10 ** 6)))
    rb = rel_bias.astype(F32) * LOG2E
    tab = rb[_t5_bucket_np(rel)] - rb[far][None, None, None, :]
    return tab.transpose(0, 3, 1, 2)


def _near_bias_stream(rel_bias, TQ, TK, phases):
    t = np.arange(TQ)[:, None]
    j = np.arange(2 * TK)[None, :]
    rel = np.stack([j - TK - ph + TQ - t for ph in phases])
    return _near_bias(rel_bias, rel, TQ - TK - 2)


def _near_bias_prompt(rel_bias):
    TQ = Q_STEP
    rel = np.arange(2 * TQ)[None, :] - TQ - np.arange(TQ)[:, None]
    return _near_bias(rel_bias, rel[None], -TQ - 1)[0]


def _conv_kernel(x_ref, halo_ref, buf_ref, w_ref, b_ref, xc_ref, xcb_ref, ext_ref, *, tt):
    first = pl.program_id(1) == 0
    ext_ref[0:SUBLANES, :] = jnp.where(first, buf_ref[...], halo_ref[...])
    ext_ref[SUBLANES:, :] = x_ref[...]
    y = jnp.broadcast_to(b_ref[...], x_ref.shape)
    for j in range(CONV_W):
        off = SUBLANES - (CONV_W - 1) + j
        y = y + ext_ref[off:off + tt, :] * w_ref[j:j + 1, :]
    xc_ref[...] = y
    xcb_ref[...] = y.astype(BF16)


def _conv(x, buf8, w, b, *, row0, B, T, tt):
    C = x.shape[1]
    assert row0 % tt == 0 and T % tt == 0 and tt % SUBLANES == 0
    nt = T // tt
    hb = tt // SUBLANES
    blk0 = row0 // tt
    main_map = lambda b, i: (blk0 + b * nt + i, 0)
    halo_map = lambda b, i: (jnp.maximum((blk0 + b * nt + i) * hb - 1, 0), 0)
    out_map = lambda b, i: (b * nt + i, 0)
    return pl.pallas_call(
        functools.partial(_conv_kernel, tt=tt),
        grid=(B, nt),
        in_specs=[pl.BlockSpec((tt, C), main_map),
                  pl.BlockSpec((SUBLANES, C), halo_map),
                  pl.BlockSpec((None, SUBLANES, C), lambda b, i: (b, 0, 0)),
                  pl.BlockSpec((CONV_W, C), lambda b, i: (0, 0)),
                  pl.BlockSpec((1, C), lambda b, i: (0, 0))],
        out_specs=[pl.BlockSpec((tt, C), out_map), pl.BlockSpec((tt, C), out_map)],
        out_shape=[jax.ShapeDtypeStruct((B * T, C), F32),
                   jax.ShapeDtypeStruct((B * T, C), BF16)],
        scratch_shapes=[pltpu.VMEM((tt + SUBLANES, C), F32)],
        compiler_params=_params(("parallel", "parallel"), 2 * tt * C * 10 + tt * C * 12),
        name="causal_conv",
    )(x, x, buf8, w, b.reshape(1, C))


def _scan_kernel(a_ref, u_ref, g_ref, h0_ref, y_ref, hlast_ref, h_ref, hs_ref, *, tt):
    i = pl.program_id(1)

    @pl.when(i == 0)
    def _():
        h_ref[...] = h0_ref[...]

    def step(t, h):
        h = a_ref[pl.ds(t, 1), :] * h + u_ref[pl.ds(t, 1), :]
        hs_ref[pl.ds(t, 1), :] = h
        return h

    h = lax.fori_loop(0, tt, step, h_ref[...])
    h_ref[...] = h
    hlast_ref[...] = h
    y_ref[...] = (hs_ref[...] * jax.nn.gelu(g_ref[...])).astype(y_ref.dtype)


def _scan(a, u, g, h0, *, row0, B, T, tt):
    C = a.shape[1]
    assert row0 % tt == 0 and T % tt == 0
    nt = T // tt
    blk0 = row0 // tt
    blk = pl.BlockSpec((tt, C), lambda b, i: (b * nt + i, 0))
    g_blk = pl.BlockSpec((tt, C), lambda b, i: (blk0 + b * nt + i, 0))
    vec = pl.BlockSpec((None, 1, C), lambda b, i: (b, 0, 0))
    return pl.pallas_call(
        functools.partial(_scan_kernel, tt=tt),
        grid=(B, nt),
        in_specs=[blk, blk, g_blk, vec],
        out_specs=[blk, vec],
        out_shape=[jax.ShapeDtypeStruct((B * T, C), BF16),
                   jax.ShapeDtypeStruct((B, 1, C), F32)],
        scratch_shapes=[pltpu.VMEM((1, C), F32), pltpu.VMEM((tt, C), F32)],
        compiler_params=_params(("parallel", "arbitrary"), 2 * tt * C * 14 + tt * C * 12),
        name="rglru_scan",
    )(a, u, g, h0)


GATE_TN = 256


def _gate_window(rb, C):
    raw = [((j * GATE_TN) // rb * rb) // LANES * LANES for j in range(C // GATE_TN)]
    ends = [((j * GATE_TN + GATE_TN - 1) // rb + 1) * rb for j in range(C // GATE_TN)]
    kw = -(-max(e - s for s, e in zip(raw, ends)) // LANES) * LANES
    kw = min(kw, C)
    starts = [min(s, C - kw) for s in raw]
    assert all(s + kw >= e for s, e in zip(starts, ends))
    return starts, kw


def _blockdiag_tiles(w):
    nblk, rb, _ = w.shape
    C = nblk * rb
    starts, kw = _gate_window(rb, C)
    dense = jax.scipy.linalg.block_diag(*[w[i] for i in range(nblk)])
    tiles = [dense[s:s + kw, j * GATE_TN:(j + 1) * GATE_TN] for j, s in enumerate(starts)]
    return jnp.stack(tiles).astype(BF16)


def _gates_kernel(x_ref, wa_ref, wx_ref, xc_ref, ba_ref, bx_ref, lam_ref, a_ref, u_ref, *, rb, kw):
    C = x_ref.shape[1]
    j = pl.program_id(1)
    start = jnp.minimum(((j * GATE_TN) // rb * rb) // LANES * LANES, C - kw)
    x = x_ref[:, pl.ds(pl.multiple_of(start, LANES), kw)]
    accs = [jnp.dot(x, w[0], preferred_element_type=F32) for w in (wa_ref, wx_ref)]
    a, u = _ep_rglru_gates(accs, [xc_ref[...]], [ba_ref[...], bx_ref[...], lam_ref[...]])
    a_ref[...] = a
    u_ref[...] = u


def _rglru_gates(xcb, xc, wa_t, wx_t, ba, bx, lam, *, rb, tm):
    rows, C = xc.shape
    nt, kw, _ = wa_t.shape
    tile = lambda: pl.BlockSpec((tm, GATE_TN), lambda i, j: (i, j))
    vec = lambda: pl.BlockSpec((1, GATE_TN), lambda i, j: (0, j))
    wspec = lambda: pl.BlockSpec((1, kw, GATE_TN), lambda i, j: (j, 0, 0))
    vmem = 2 * (tm * C * 2 + 2 * kw * GATE_TN * 2 + 3 * tm * GATE_TN * 4) + 8 * tm * GATE_TN * 4
    return pl.pallas_call(
        functools.partial(_gates_kernel, rb=rb, kw=kw),
        grid=(rows // tm, nt),
        in_specs=[pl.BlockSpec((tm, C), lambda i, j: (i, 0)), wspec(), wspec(), tile(),
                  vec(), vec(), vec()],
        out_specs=[tile(), tile()],
        out_shape=[jax.ShapeDtypeStruct((rows, C), F32)] * 2,
        compiler_params=_params(("parallel", "arbitrary"), vmem),
        name="rglru_gates",
    )(xcb, wa_t, wx_t, xc, ba.reshape(1, C), bx.reshape(1, C), lam.reshape(1, C))


def _pick(n, cands):
    for c in cands:
        if n % c == 0:
            return c
    raise ValueError(f"no tile for {n}")


def _layer(x, hist, p, rel_bias, dims):
    Tp, Bs, Ts, past = dims
    M, D = x.shape
    Ms = Bs * Ts
    k_past, v_past, ki_past, conv_buf, h0 = hist
    KV = k_past.shape[2]
    HQ = KV * GROUP * HEAD_DIM
    KVD = KV * HEAD_DIM
    DI = ki_past.shape[-1]
    C = p["conv_w"].shape[-1]
    n_in = p["w_in"].shape[1]
    HI = (n_in - HQ - 2 * KVD - DI - 2 * C - 2 * D) // (DI + 1)
    assert DI == LANES
    sizes = (HQ, KVD, KVD, HI * DI, DI, HI, C, C, D, D)
    offs = np.concatenate([[0], np.cumsum(sizes)])
    assert offs[-1] == n_in
    w_in = p["w_in"]

    def wslice(i, j=None):
        j = i if j is None else j
        return w_in[:, offs[i]:offs[j + 1]].astype(BF16)

    tm = _pick(M, (768, 512, 256, 128, 64))
    TK = KEY_TILE

    h = _rmsnorm(x, p["norm_mix"], tm)

    def proj(w, ep, dt, tn, vecs=(), name="in_proj"):
        return _matmul(h, [w], ep, [dt], tm=tm, tn=tn, vecs=vecs, name=name)[0]

    q = proj(wslice(0), _ep_headnorm(HEAD_DIM ** -0.5 * LOG2E), BF16, _pick(HQ, (512, 256, 128)),
             vecs=[jnp.tile(p["q_norm"], HQ // HEAD_DIM)], name="in_proj_q")
    k = proj(wslice(1), _ep_headnorm(1.0), F32, _pick(KVD, (512, 256, 128)),
             vecs=[jnp.tile(p["k_norm"], KV)], name="in_proj_k")
    v = proj(wslice(2), _ep_identity, F32, _pick(KVD, (512, 256, 128)), name="in_proj_v")
    qi = proj(wslice(3), _ep_identity, BF16, _pick(HI * DI, (512, 256, 128)), name="in_proj_qi")
    kw_w = jnp.pad(wslice(4, 5), ((0, 0), (0, 2 * LANES - DI - HI)))
    kiwi = proj(kw_w, _ep_identity, F32, 2 * LANES, name="in_proj_ki")
    ki, wi = kiwi[:, :DI], kiwi[:, DI:DI + HI]
    tn_c = _pick(C, (768, 384, 128))
    xr = proj(wslice(6), _ep_identity, F32, tn_c, name="in_proj_xr")
    gr = proj(wslice(7), _ep_identity, F32, tn_c, name="in_proj_gr")
    tn_d = _pick(D, (512, 256, 128))
    ga = proj(wslice(8), _ep_identity, F32, tn_d, name="in_proj_ga")
    gb = proj(wslice(9), _ep_identity, F32, tn_d, name="in_proj_gb")

    k_bf, v_bf, ki_bf = k.astype(BF16), v.astype(BF16), ki.astype(BF16)

    assert Tp % Q_STEP == 0
    back_p = -Tp % FAR_TILE
    n_sel_p = min(TOPK_MAX, Tp // 4)
    mask_p = _select_prompt(qi, wi, jnp.pad(ki_bf[:Tp], ((0, back_p), (0, 0))), n_sel_p, Tp)
    kv_pad = lambda a: jnp.pad(a[:Tp], ((FRONT_PAD, back_p), (0, 0)))
    o_p = _attention_prompt(q, kv_pad(k_bf), kv_pad(v_bf), mask_p, _near_bias_prompt(rel_bias))

    Ls = past + Ts
    nt_s = -(-Ls // TK)
    pad_s = nt_s * TK - Ls

    def with_cache(cache, new):
        new = new[Tp:].reshape(Bs, Ts, -1)
        parts = [cache.reshape(Bs, past, -1).astype(BF16), new]
        if pad_s:
            parts.append(jnp.zeros((Bs, pad_s, new.shape[-1]), BF16))
        return jnp.concatenate(parts, axis=1)

    n_sel_s = min(TOPK_MAX, Ls // 4)
    mask_s = _select(qi[Tp:], wi[Tp:], with_cache(ki_past, ki_bf), TQ=Ts, NT=nt_s,
                     lend_a=0, lend_b=Ls, n_sel=n_sel_s, batched=True)
    nb_s = _near_bias_stream(rel_bias, Ts, TK, [Ls - ((Ls - 1) // TK) * TK])
    o_s = _attention(q[Tp:], with_cache(k_past, k_bf), with_cache(v_past, v_bf), mask_s, nb_s,
                     TQ=Ts, lend_a=0, lend_b=Ls, batched=True)
    o_a = jnp.concatenate([o_p, o_s], axis=0)

    assert C % GATE_TN == 0
    wa_t = _blockdiag_tiles(p["rg_wa"])
    wx_t = _blockdiag_tiles(p["rg_wx"])

    def griffin(row0, B, T, buf, h_init):
        tt = _pick(math.gcd(T, row0) if row0 else T, (128, 64, 32, 16, 8))
        buf8 = jnp.pad(buf.astype(F32), ((0, 0), (SUBLANES - (CONV_W - 1), 0), (0, 0)))
        xc, xcb = _conv(xr, buf8, p["conv_w"], p["conv_b"], row0=row0, B=B, T=T, tt=tt)
        rows = B * T
        tmr = _pick(rows, (512, 256, 128, 64, 32))
        a, u = _rglru_gates(xcb, xc, wa_t, wx_t, p["rg_ba"], p["rg_bx"], p["rg_lambda"],
                            rb=p["rg_wa"].shape[1], tm=tmr)
        y, h_last = _scan(a, u, gr, h_init.reshape(B, 1, C), row0=row0, B=B, T=T, tt=tt)
        tail = xr[row0:row0 + rows].reshape(B, T, C)[:, -(CONV_W - 1):]
        conv_new = jnp.concatenate([buf.astype(F32), tail], axis=1)[:, -(CONV_W - 1):]
        return y, conv_new, h_last.reshape(B, C)

    y_p, conv_p, h_p = griffin(0, 1, Tp, jnp.zeros((1, CONV_W - 1, C), F32), jnp.zeros((1, C), F32))
    y_s, conv_s, h_s = griffin(Tp, Bs, Ts, conv_buf, h0)
    o_b = jnp.concatenate([y_p, y_s], axis=0)

    part = _matmul(o_a, [p["w_out_attn"].astype(BF16)], _ep_gate, [F32], tm=tm, tn=tn_d,
                   exts=[ga], name="out_attn")[0]
    merged = _matmul(o_b, [p["w_out_rg"].astype(BF16)], _ep_gate_add, [BF16], tm=tm, tn=tn_d,
                     exts=[gb, part], name="out_rg")[0]
    x1 = _matmul(merged, [p["w_o"].astype(BF16)], _ep_residual, [F32], tm=tm, tn=tn_d,
                 exts=[x], name="w_o")[0]
    hf = _rmsnorm(x1, p["norm_ffn"], tm)
    FF = p["ffn_w1"].shape[1]
    tn_f = _pick(FF, (512, 256, 128))
    act = _matmul(hf, [p["ffn_w1"].astype(BF16), p["ffn_w3"].astype(BF16)], _ep_swiglu, [BF16],
                  tm=tm, tn=tn_f, name="ffn_up")[0]
    tk_f = _pick(FF, (5504, 2816, 2048, 1024, 512, 256, 128))
    x2 = _matmul(act, [p["ffn_w2"].astype(BF16)], _ep_residual, [F32],
                 tm=tm, tn=tn_d, tk=tk_f, exts=[x1],
                 name="ffn_down")[0]

    new_p = (k[:Tp].reshape(1, Tp, KV, HEAD_DIM), v[:Tp].reshape(1, Tp, KV, HEAD_DIM),
             ki[:Tp].reshape(1, Tp, DI), conv_p, h_p)
    new_s = (k[Tp:].reshape(Bs, Ts, KV, HEAD_DIM), v[Tp:].reshape(Bs, Ts, KV, HEAD_DIM),
             ki[Tp:].reshape(Bs, Ts, DI), conv_s, h_s)
    return x2, new_p, new_s


def kernel(x_prompt, x_sample, cache_k, cache_v, cache_kidx, state_conv, state_rglru, norm_mix, w_in, q_norm, k_norm, rel_bias, conv_w, conv_b, rg_wa, rg_ba, rg_wx, rg_bx, rg_lambda, w_out_attn, w_out_rg, w_o, norm_ffn, ffn_w1, ffn_w3, ffn_w2):
    Bp, Tp, D = x_prompt.shape
    Bs, Ts, _ = x_sample.shape
    assert Bp == 1 and Tp % CHUNK == 0
    depth = w_in.shape[0]
    past = cache_k.shape[2]
    x = jnp.concatenate([x_prompt.reshape(Tp, D), x_sample.reshape(Bs * Ts, D)], axis=0)
    outs_p, outs_s = [], []
    for l in range(depth):
        p = dict(norm_mix=norm_mix[l], w_in=w_in[l], q_norm=q_norm[l], k_norm=k_norm[l],
                 conv_w=conv_w[l], conv_b=conv_b[l], rg_wa=rg_wa[l], rg_ba=rg_ba[l],
                 rg_wx=rg_wx[l], rg_bx=rg_bx[l], rg_lambda=rg_lambda[l],
                 w_out_attn=w_out_attn[l], w_out_rg=w_out_rg[l], w_o=w_o[l],
                 norm_ffn=norm_ffn[l], ffn_w1=ffn_w1[l], ffn_w3=ffn_w3[l], ffn_w2=ffn_w2[l])
        hist = (cache_k[l], cache_v[l], cache_kidx[l], state_conv[l], state_rglru[l])
        x, new_p, new_s = _layer(x, hist, p, rel_bias, (Tp, Bs, Ts, past))
        outs_p.append(new_p)
        outs_s.append(new_s)
    stack = lambda outs, i: jnp.stack([o[i] for o in outs])
    return (x[:Tp].reshape(1, Tp, D), x[Tp:].reshape(Bs, Ts, D),
            *[stack(outs_p, i) for i in range(5)],
            *[stack(outs_s, i) for i in range(5)])
```

```python
import functools
import math

import numpy as np
import jax
import jax.numpy as jnp
from jax import lax
from jax.experimental import pallas as pl
from jax.experimental.pallas import tpu as pltpu

F32 = jnp.float32
BF16 = jnp.bfloat16

CHUNK = 64
HEAD_DIM = 128
GROUP = 4
TOPK_MAX = 256
N_BUCKETS = 32
REL_MAX_DIST = 128
RG_C = 8.0
CONV_W = 4
EPS = 1e-6

LANES = 128
SUBLANES = 8
KEY_TILE = 256
Q_STEP = 2 * CHUNK
FAR_TILE = 512
FRONT_PAD = FAR_TILE
ONES_ROWS = 16
LOG2E = math.log2(math.e)
VMEM_CAP = 56 << 20
NEG = -1e30


def _params(sem, vmem_bytes):
    limit = min(max(int(vmem_bytes) + (6 << 20), 24 << 20), VMEM_CAP)
    return pltpu.CompilerParams(dimension_semantics=sem, vmem_limit_bytes=limit)


def _rmsnorm_kernel(x_ref, g_ref, o_ref):
    x = x_ref[...]
    ms = jnp.mean(x * x, axis=-1, keepdims=True)
    o_ref[...] = (x * lax.rsqrt(ms + EPS) * g_ref[...]).astype(o_ref.dtype)


def _rmsnorm(x, g, tm):
    M, D = x.shape
    tm = min(tm, 256)
    return pl.pallas_call(
        _rmsnorm_kernel,
        grid=(M // tm,),
        in_specs=[pl.BlockSpec((tm, D), lambda i: (i, 0)),
                  pl.BlockSpec((1, D), lambda i: (0, 0))],
        out_specs=pl.BlockSpec((tm, D), lambda i: (i, 0)),
        out_shape=jax.ShapeDtypeStruct((M, D), BF16),
        compiler_params=_params(("parallel",), 2 * tm * D * 6 + 2 * tm * D * 4),
        name="rmsnorm",
    )(x, g.reshape(1, D))


def _mm_kernel(*refs, n_w, n_ext, n_vec, n_out, nk, epilogue):
    x_ref = refs[0]
    w_refs = refs[1:1 + n_w]
    p = 1 + n_w
    ext_refs = refs[p:p + n_ext]
    p += n_ext
    vec_refs = refs[p:p + n_vec]
    p += n_vec
    out_refs = refs[p:p + n_out]
    acc_refs = refs[p + n_out:]

    def finish(accs):
        res = epilogue(accs, [e[...] for e in ext_refs], [v[...] for v in vec_refs])
        for o, r in zip(out_refs, res):
            o[...] = r.astype(o.dtype)

    if nk == 1:
        finish([jnp.dot(x_ref[...], w[...], preferred_element_type=F32) for w in w_refs])
        return

    k = pl.program_id(2)

    @pl.when(k == 0)
    def _():
        for a in acc_refs:
            a[...] = jnp.zeros_like(a)

    for a, w in zip(acc_refs, w_refs):
        a[...] += jnp.dot(x_ref[...], w[...], preferred_element_type=F32)

    @pl.when(k == nk - 1)
    def _():
        finish([a[...] for a in acc_refs])


def _matmul(x, ws, epilogue, out_dtypes, *, tm, tn, tk=None, exts=(), vecs=(), name="matmul"):
    M, K = x.shape
    n = ws[0].shape[1]
    tk = K if tk is None else tk
    nk = K // tk
    assert M % tm == 0 and n % tn == 0 and K % tk == 0
    x_map = lambda j, i, k: (i, k)
    w_map = lambda j, i, k: (k, j)
    mn_map = lambda j, i, k: (i, j)
    in_specs = [pl.BlockSpec((tm, tk), x_map)]
    in_specs += [pl.BlockSpec((tk, tn), w_map) for _ in ws]
    in_specs += [pl.BlockSpec((tm, tn), mn_map) for _ in exts]
    in_specs += [pl.BlockSpec((1, tn), lambda j, i, k: (0, j)) for _ in vecs]
    out_specs = [pl.BlockSpec((tm, tn), mn_map) for _ in out_dtypes]
    out_shape = [jax.ShapeDtypeStruct((M, n), dt) for dt in out_dtypes]
    scratch = [pltpu.VMEM((tm, tn), F32) for _ in ws] if nk > 1 else []
    vmem = (2 * (tm * tk * 2 + len(ws) * tk * tn * 2 + (len(exts) + len(out_dtypes)) * tm * tn * 4)
            + 3 * len(ws) * tm * tn * 4)
    kern = functools.partial(_mm_kernel, n_w=len(ws), n_ext=len(exts), n_vec=len(vecs),
                             n_out=len(out_dtypes), nk=nk, epilogue=epilogue)
    outs = pl.pallas_call(
        kern,
        grid=(n // tn, M // tm, nk),
        in_specs=in_specs,
        out_specs=out_specs,
        out_shape=out_shape,
        scratch_shapes=scratch,
        compiler_params=_params(("parallel", "parallel", "arbitrary"), vmem),
        name=name,
    )(x, *ws, *exts, *[v.reshape(1, -1) for v in vecs])
    return outs


def _ep_identity(accs, exts, vecs):
    return (accs[0],)


def _ep_headnorm(scale):
    def ep(accs, exts, vecs):
        a, g = accs[0], vecs[0]
        outs = []
        for j in range(a.shape[1] // HEAD_DIM):
            aj = a[:, j * HEAD_DIM:(j + 1) * HEAD_DIM]
            ms = jnp.mean(aj * aj, axis=-1, keepdims=True)
            outs.append(aj * lax.rsqrt(ms + EPS) * g[:, j * HEAD_DIM:(j + 1) * HEAD_DIM])
        y = jnp.concatenate(outs, axis=1)
        return (y * scale if scale != 1.0 else y,)
    return ep


def _ep_gate(accs, exts, vecs):
    return (jax.nn.sigmoid(exts[0]) * accs[0],)


def _ep_gate_add(accs, exts, vecs):
    return (exts[1] + jax.nn.sigmoid(exts[0]) * accs[0],)


def _ep_residual(accs, exts, vecs):
    return (exts[0] + accs[0],)


def _ep_swiglu(accs, exts, vecs):
    return (jax.nn.silu(accs[0]) * accs[1],)


def _softplus(x):
    return jnp.maximum(x, 0.0) + jnp.log1p(jnp.exp(-jnp.abs(x)))


def _ep_rglru_gates(accs, exts, vecs):
    xc = exts[0]
    r = jax.nn.sigmoid(accs[0] + vecs[0])
    i = jax.nn.sigmoid(accs[1] + vecs[1])
    log_a = -RG_C * r * _softplus(-vecs[2])
    a = jnp.exp(log_a)
    u = jnp.sqrt(1.0 - a * a) * (i * xc)
    return (a, u)


INT_MAX = 2 ** 31 - 1


def _open_rows(lo, hi):
    return jnp.max(jnp.where(lo < hi, 1.0, 0.0))


def _to_key(s):
    b = pltpu.bitcast(s, jnp.int32)
    return b ^ ((b >> 31) & jnp.int32(0x7FFFFFFF))


def _from_key(k):
    return pltpu.bitcast(k ^ ((k >> 31) & jnp.int32(0x7FFFFFFF)), F32)


def _kth_largest_key(count_ge, lo, hi, ksel, n_valid):
    def body(st):
        lo, hi, c_lo, _, it = st
        active = lo < hi
        mid_k = (lo >> 1) + (hi >> 1) + ((lo | hi) & 1)
        mid_f = _to_key(0.5 * _from_key(lo) + 0.5 * _from_key(hi))
        mid_f = jnp.minimum(jnp.maximum(mid_f, lo + 1), hi)
        mid = jnp.where((it & 1) == 0, mid_f, mid_k)
        c = count_ge(mid)
        ge = c >= ksel
        lo_n = jnp.where(ge, mid, lo)
        c_n = jnp.where(ge, c, c_lo)
        hi_n = jnp.where(c == ksel, mid, jnp.where(ge, hi, mid - 1))
        lo = jnp.where(active, lo_n, lo)
        c_lo = jnp.where(active, c_n, c_lo)
        hi = jnp.where(active, hi_n, hi)
        return lo, hi, c_lo, _open_rows(lo, hi), it + 1

    tau, _, c_tau, _, _ = lax.while_loop(lambda st: st[3] > 0.5, body,
                                         (lo, hi, n_valid, _open_rows(lo, hi), jnp.int32(0)))
    return tau, c_tau


def _tie_cut(count_tie_le, need, n_keys):
    def body(st):
        lo, hi, _ = st
        active = lo < hi
        mid = (lo + hi) >> 1
        ok = count_tie_le(mid) >= need
        hi = jnp.where(active, jnp.where(ok, mid, hi), hi)
        lo = jnp.where(active, jnp.where(ok, lo, mid + 1), lo)
        return lo, hi, _open_rows(lo, hi)

    lo = jnp.zeros(need.shape, jnp.int32)
    hi = jnp.full(need.shape, n_keys - 1, jnp.int32)
    cut, _, _ = lax.while_loop(lambda st: st[2] > 0.5, body, (lo, hi, _open_rows(lo, hi)))
    return cut


def _threshold_and_cut(key_ref, cut_ref, nt, lo, hi, ksel, n_valid):
    _, TQ, TK = key_ref.shape
    nl = TK // LANES
    lane = lax.broadcasted_iota(jnp.int32, (TQ, LANES), 1)

    def count(indicator):
        def body(j, c):
            kk = key_ref[j]
            for l in range(nl):
                c = c + indicator(kk[:, l * LANES:(l + 1) * LANES], j * TK + l * LANES)
            return c
        c = lax.fori_loop(0, nt, body, jnp.zeros((TQ, LANES), F32))
        return jnp.broadcast_to(jnp.sum(c, axis=1, keepdims=True), (TQ, LANES))

    tau, c_tau = _kth_largest_key(
        lambda mid: count(lambda kk, base: jnp.where(kk >= mid, 1.0, 0.0)), lo, hi, ksel, n_valid)
    cut_ref[...] = jnp.full((TQ, LANES), INT_MAX, jnp.int32)

    tied = jnp.max(jnp.where(c_tau > ksel, 1.0, 0.0)) > 0.5

    @pl.when(tied)
    def _():
        need = ksel - count(lambda kk, base: jnp.where(kk > tau, 1.0, 0.0))
        cut_ref[...] = _tie_cut(
            lambda mid: count(lambda kk, base: jnp.where(
                kk == tau, jnp.where(lane + base <= mid, 1.0, 0.0), 0.0)),
            need, nt * TK)

    return tau, cut_ref[...], tied


def _selected(kk, base, tau, cut):
    lane = lax.broadcasted_iota(jnp.int32, kk.shape, 1)
    tie = jnp.where(kk == tau, jnp.where(lane + base <= cut, 1, 0), 0)
    return jnp.where(kk > tau, 1, tie)


def _write_selection(write_tile, nt, tied):
    @pl.when(tied)
    def _():
        lax.fori_loop(0, nt, lambda j, _: write_tile(j, True) or 0, 0)

    @pl.when(jnp.logical_not(tied))
    def _():
        lax.fori_loop(0, nt, lambda j, _: write_tile(j, False) or 0, 0)
def _select_kernel(qi_ref, wi_ref, ki_ref, mask_ref, q2_ref, w2_ref, key_ref, cut_ref, *,
                   TQ, TK, NT, HI, lend_a, lend_b, n_sel, wscale):
    nl = TK // LANES
    lend = lend_a * pl.program_id(0) + lend_b
    nt = (lend + TK - 1) // TK
    ksel = jnp.minimum(n_sel, lend).astype(F32)

    wi = wi_ref[...] * wscale
    for h in range(HI):
        q2_ref[h * TQ:(h + 1) * TQ, :] = qi_ref[:, h * LANES:(h + 1) * LANES]
        w2_ref[h * TQ:(h + 1) * TQ, :] = jnp.broadcast_to(wi[:, h:h + 1], (TQ, LANES))

    lane = lax.broadcasted_iota(jnp.int32, (TQ, TK), 1)
    int_min = jnp.int32(-2 ** 31)

    def score_tile(j, carry):
        smin, smax = carry
        start = pl.multiple_of(j * TK, TK)
        kt = ki_ref[pl.ds(start, TK), :]
        s = lax.dot_general(q2_ref[...], kt, (((1,), (1,)), ((), ())),
                            preferred_element_type=F32)
        cols = []
        for l in range(nl):
            acc = jnp.zeros((TQ, LANES), F32)
            for h in range(HI):
                acc = acc + (jnp.maximum(s[h * TQ:(h + 1) * TQ, l * LANES:(l + 1) * LANES], 0.0)
                             * w2_ref[h * TQ:(h + 1) * TQ, :])
            cols.append(acc)
        sc = jnp.concatenate(cols, axis=1)
        valid = (lane + j * TK) < lend
        key_ref[j] = jnp.where(valid, _to_key(sc), int_min)
        lo_s = jnp.where(valid, sc, jnp.inf)
        hi_s = jnp.where(valid, sc, -jnp.inf)
        for l in range(nl):
            smin = jnp.minimum(smin, lo_s[:, l * LANES:(l + 1) * LANES])
            smax = jnp.maximum(smax, hi_s[:, l * LANES:(l + 1) * LANES])
        return smin, smax

    smin, smax = lax.fori_loop(
        0, nt, score_tile,
        (jnp.full((TQ, LANES), jnp.inf, F32), jnp.full((TQ, LANES), -jnp.inf, F32)))
    lo = _to_key(jnp.broadcast_to(jnp.min(smin, axis=1, keepdims=True), (TQ, LANES)))
    hi = _to_key(jnp.broadcast_to(jnp.max(smax, axis=1, keepdims=True), (TQ, LANES)))

    n_valid = jnp.full((TQ, LANES), lend, jnp.int32).astype(F32)
    tau, cut, tied = _threshold_and_cut(key_ref, cut_ref, nt, lo, hi,
                                        jnp.full((TQ, LANES), ksel, F32), n_valid)

    def write(j, exact_ties):
        kk = key_ref[j]
        sel = []
        for l in range(nl):
            slab = kk[:, l * LANES:(l + 1) * LANES]
            sel.append(_selected(slab, j * TK + l * LANES, tau, cut) if exact_ties
                       else jnp.where(slab >= tau, 1, 0))
        mask_ref[0, j] = jnp.concatenate(sel, axis=1).astype(jnp.int8)

    _write_selection(write, nt, tied)

    def clear(j, _):
        mask_ref[0, j] = jnp.zeros((TQ, TK), jnp.int8)
        return 0

    lax.fori_loop(nt, NT, clear, 0)


def _select(qi, wi, ki, *, TQ, NT, lend_a, lend_b, n_sel, batched):
    TK = KEY_TILE
    Mq = qi.shape[0]
    HI = wi.shape[1]
    steps = Mq // TQ
    if batched:
        ki_spec = pl.BlockSpec((None, NT * TK, LANES), lambda i: (i, 0, 0))
    else:
        ki_spec = pl.BlockSpec((NT * TK, LANES), lambda i: (0, 0))
    kern = functools.partial(_select_kernel, TQ=TQ, TK=TK, NT=NT, HI=HI, lend_a=lend_a,
                             lend_b=lend_b, n_sel=n_sel,
                             wscale=float(HI ** -0.5 * LANES ** -0.5))
    vmem = (2 * (TQ * HI * LANES * 2 + NT * TK * LANES * 2 + NT * TQ * TK)
            + HI * TQ * LANES * 6 + NT * TQ * TK * 4 + 3 * HI * TQ * TK * 4)
    return pl.pallas_call(
        kern,
        grid=(steps,),
        in_specs=[pl.BlockSpec((TQ, HI * LANES), lambda i: (i, 0)),
                  pl.BlockSpec((TQ, HI), lambda i: (i, 0)),
                  ki_spec],
        out_specs=pl.BlockSpec((1, NT, TQ, TK), lambda i: (i, 0, 0, 0)),
        out_shape=jax.ShapeDtypeStruct((steps, NT, TQ, TK), jnp.int8),
        scratch_shapes=[pltpu.VMEM((HI * TQ, LANES), BF16),
                        pltpu.VMEM((HI * TQ, LANES), F32),
                        pltpu.VMEM((NT, TQ, TK), jnp.int32),
                        pltpu.VMEM((TQ, LANES), jnp.int32)],
        compiler_params=_params(("parallel",), vmem),
        name="dsa_select",
    )(qi, wi, ki)


def _select_prompt_kernel(qi_ref, wi_ref, ki_ref, mask_ref, q2_ref, w2_ref, key_ref, acc_ref,
                          cut_ref, *,
                          HI, HG, NT, n_sel, wscale):
    TQ, TK = Q_STEP, FAR_TILE
    nl = TK // LANES
    pad_tiles = FRONT_PAD // LANES
    RB = 64
    c2 = pl.program_id(0)
    row = lax.broadcasted_iota(jnp.int32, (TQ, LANES), 0)
    lend = jnp.where(row < CHUNK, c2 * TQ + CHUNK, c2 * TQ + TQ)
    nt = (c2 * TQ + TQ + TK - 1) // TK
    ksel = jnp.minimum(n_sel, lend).astype(F32)

    wi = wi_ref[...] * wscale
    for h in range(HI):
        q2_ref[h * TQ:(h + 1) * TQ, :] = qi_ref[:, h * LANES:(h + 1) * LANES]
        w2_ref[h * TQ:(h + 1) * TQ, :] = jnp.broadcast_to(wi[:, h:h + 1], (TQ, LANES))

    lane = lax.broadcasted_iota(jnp.int32, (TQ, TK), 1)
    lend_t = jnp.concatenate([lend] * nl, axis=1)
    int_min = jnp.int32(-2 ** 31)

    def score_tile(j, carry):
        smin, smax = carry
        start = pl.multiple_of(j * TK, TK)
        kt = ki_ref[pl.ds(start, TK), :]
        for hg in range(HI // HG):
            s = lax.dot_general(q2_ref[hg * HG * TQ:(hg + 1) * HG * TQ, :], kt,
                                (((1,), (1,)), ((), ())), preferred_element_type=F32)
            for r in range(TQ // RB):
                cs = [None] * nl
                for h in range(HG):
                    r0 = h * TQ + r * RB
                    w = w2_ref[(hg * HG) * TQ + r0:(hg * HG) * TQ + r0 + RB, :]
                    for l in range(nl):
                        term = jnp.maximum(s[r0:r0 + RB, l * LANES:(l + 1) * LANES], 0.0) * w
                        cs[l] = term if cs[l] is None else cs[l] + term
                for l in range(nl):
                    if hg == 0:
                        acc_ref[r * RB:(r + 1) * RB, l * LANES:(l + 1) * LANES] = cs[l]
                    else:
                        acc_ref[r * RB:(r + 1) * RB, l * LANES:(l + 1) * LANES] += cs[l]
        sc = acc_ref[...]
        valid = (lane + j * TK) < lend_t
        key_ref[j] = jnp.where(valid, _to_key(sc), int_min)
        lo_s = jnp.where(valid, sc, jnp.inf)
        hi_s = jnp.where(valid, sc, -jnp.inf)
        for l in range(nl):
            smin = jnp.minimum(smin, lo_s[:, l * LANES:(l + 1) * LANES])
            smax = jnp.maximum(smax, hi_s[:, l * LANES:(l + 1) * LANES])
        return smin, smax

    smin, smax = lax.fori_loop(
        0, nt, score_tile,
        (jnp.full((TQ, LANES), jnp.inf, F32), jnp.full((TQ, LANES), -jnp.inf, F32)))
    lo = _to_key(jnp.broadcast_to(jnp.min(smin, axis=1, keepdims=True), (TQ, LANES)))
    hi = _to_key(jnp.broadcast_to(jnp.max(smax, axis=1, keepdims=True), (TQ, LANES)))

    tau, cut, tied = _threshold_and_cut(key_ref, cut_ref, nt, lo, hi, ksel, lend.astype(F32))

    neg_tile = jnp.full((TQ, LANES), NEG, BF16)
    for i in range(pad_tiles):
        mask_ref[0, i] = neg_tile

    def write(j, exact_ties):
        kk = key_ref[j]
        for i in range(nl):
            slab = kk[:, i * LANES:(i + 1) * LANES]
            keep = (_selected(slab, j * TK + i * LANES, tau, cut) > 0) if exact_ties else slab >= tau
            mask_ref[0, pad_tiles + nl * j + i] = jnp.where(keep, 0.0, NEG).T.astype(BF16)

    _write_selection(write, nt, tied)

    def clear(j, _):
        for i in range(nl):
            mask_ref[0, pad_tiles + nl * j + i] = neg_tile
        return 0

    lax.fori_loop(nt, NT, clear, 0)


def _select_prompt(qi, wi, ki, n_sel, T):
    TQ, TK = Q_STEP, FAR_TILE
    HI = wi.shape[1]
    HG = 4 if HI % 4 == 0 else 1
    NT = ki.shape[0] // TK
    steps = T // TQ
    ntile = (FRONT_PAD + NT * TK) // LANES
    kern = functools.partial(_select_prompt_kernel, HI=HI, HG=HG, NT=NT, n_sel=n_sel,
                             wscale=float(HI ** -0.5 * LANES ** -0.5))
    vmem = (2 * (TQ * HI * LANES * 2 + NT * TK * LANES * 2 + ntile * TQ * LANES * 2)
            + HI * TQ * LANES * 6 + NT * TQ * TK * 4 + TQ * TK * 4 + 4 * HG * TQ * TK * 4)
    return pl.pallas_call(
        kern,
        grid=(steps,),
        in_specs=[pl.BlockSpec((TQ, HI * LANES), lambda i: (i, 0)),
                  pl.BlockSpec((TQ, HI), lambda i: (i, 0)),
                  pl.BlockSpec((NT * TK, LANES), lambda i: (0, 0))],
        out_specs=pl.BlockSpec((1, ntile, TQ, LANES), lambda i: (i, 0, 0, 0)),
        out_shape=jax.ShapeDtypeStruct((steps, ntile, TQ, LANES), BF16),
        scratch_shapes=[pltpu.VMEM((HI * TQ, LANES), BF16),
                        pltpu.VMEM((HI * TQ, LANES), F32),
                        pltpu.VMEM((NT, TQ, TK), jnp.int32),
                        pltpu.VMEM((TQ, TK), F32),
                        pltpu.VMEM((TQ, LANES), jnp.int32)],
        compiler_params=_params(("parallel",), vmem),
        name="dsa_select_prompt",
    )(qi, wi, ki)


def _attn_kernel(q_ref, k_ref, v_ref, mask_ref, nb_ref, o_ref, q2_ref, m_ref, l_ref, acc_ref, *,
                 TQ, TK, G, lend_a, lend_b, step_axis):
    nl = TK // LANES
    R = G * TQ
    lend = lend_a * pl.program_id(step_axis) + lend_b
    jl = (lend - 1) // TK

    for g in range(G):
        q2_ref[g * TQ:(g + 1) * TQ, :] = q_ref[:, g * HEAD_DIM:(g + 1) * HEAD_DIM]
    m_ref[...] = jnp.full((R, LANES), NEG, F32)
    l_ref[...] = jnp.zeros((R, LANES), F32)
    acc_ref[...] = jnp.zeros((R, HEAD_DIM), F32)

    def tile(j, half):
        start = pl.multiple_of(j * TK, TK)
        kt = k_ref[pl.ds(start, TK), :]
        vt = v_ref[pl.ds(start, TK), :]
        s = lax.dot_general(q2_ref[...], kt, (((1,), (1,)), ((), ())),
                            preferred_element_type=F32)
        madd = jnp.where(mask_ref[0, j].astype(jnp.int32) != 0, 0.0, NEG)
        rows = []
        for g in range(G):
            sg = s[g * TQ:(g + 1) * TQ, :]
            if half is not None:
                sg = sg + nb_ref[0, g, :, half * TK:(half + 1) * TK]
            rows.append(sg + madd)
        s = jnp.concatenate(rows, axis=0)
        m_old = m_ref[...]
        m_cur = s[:, :LANES]
        for l in range(1, nl):
            m_cur = jnp.maximum(m_cur, s[:, l * LANES:(l + 1) * LANES])
        m_new = jnp.maximum(m_old, jnp.broadcast_to(jnp.max(m_cur, axis=1, keepdims=True), (R, LANES)))
        alpha = jnp.exp2(m_old - m_new)
        p = jnp.exp2(s - jnp.concatenate([m_new] * nl, axis=1))
        psum = p[:, :LANES]
        for l in range(1, nl):
            psum = psum + p[:, l * LANES:(l + 1) * LANES]
        l_ref[...] = alpha * l_ref[...] + jnp.broadcast_to(
            jnp.sum(psum, axis=1, keepdims=True), (R, LANES))
        acc_ref[...] = alpha * acc_ref[...] + jnp.dot(
            p.astype(BF16), vt, preferred_element_type=F32)
        m_ref[...] = m_new

    def far(j, _):
        tile(j, None)
        return 0

    lax.fori_loop(0, jnp.maximum(jl - 1, 0), far, 0)

    @pl.when(jl >= 1)
    def _():
        tile(jl - 1, 0)

    tile(jl, 1)

    o = acc_ref[...] / l_ref[...]
    for g in range(G):
        o_ref[:, g * HEAD_DIM:(g + 1) * HEAD_DIM] = o[g * TQ:(g + 1) * TQ, :].astype(o_ref.dtype)


def _attention(q, k, v, mask, nb, *, TQ, lend_a, lend_b, batched):
    TK = KEY_TILE
    G = GROUP
    Mq, HD = q.shape
    KV = HD // (G * HEAD_DIM)
    steps = Mq // TQ
    NT = mask.shape[1]
    L = k.shape[-2]
    P = nb.shape[0]
    if batched:
        grid = (steps, KV)
        q_map = lambda b, h: (b, h)
        kv_spec = pl.BlockSpec((None, L, HEAD_DIM), lambda b, h: (b, 0, h))
        mask_map = lambda b, h: (b, 0, 0, 0)
        nb_map = lambda b, h: (0, h, 0, 0)
        step_axis = 0
    else:
        grid = (KV, steps)
        q_map = lambda h, c: (c, h)
        kv_spec = pl.BlockSpec((L, HEAD_DIM), lambda h, c: (0, h))
        mask_map = lambda h, c: (c, 0, 0, 0)
        nb_map = lambda h, c: (c % P, h, 0, 0)
        step_axis = 1
    kern = functools.partial(_attn_kernel, TQ=TQ, TK=TK, G=G, lend_a=lend_a, lend_b=lend_b,
                             step_axis=step_axis)
    R = G * TQ
    vmem = (2 * (2 * TQ * G * HEAD_DIM * 2 + 2 * L * HEAD_DIM * 2 + NT * TQ * TK + G * TQ * 2 * TK * 4)
            + R * LANES * 14 + 6 * R * TK * 4)
    return pl.pallas_call(
        kern,
        grid=grid,
        in_specs=[pl.BlockSpec((TQ, G * HEAD_DIM), q_map),
                  kv_spec, kv_spec,
                  pl.BlockSpec((1, NT, TQ, TK), mask_map),
                  pl.BlockSpec((1, G, TQ, 2 * TK), nb_map)],
        out_specs=pl.BlockSpec((TQ, G * HEAD_DIM), q_map),
        out_shape=jax.ShapeDtypeStruct((Mq, HD), BF16),
        scratch_shapes=[pltpu.VMEM((R, HEAD_DIM), BF16),
                        pltpu.VMEM((R, LANES), F32),
                        pltpu.VMEM((R, LANES), F32),
                        pltpu.VMEM((R, HEAD_DIM), F32)],
        compiler_params=_params(("parallel", "arbitrary"), vmem),
        name="dsa_attention",
    )(q, k, v, mask, nb)


def _attn_prompt_kernel(q_ref, k_ref, v_ref, mask_ref, nb_ref, o_ref,
                        q2_ref, m_ref, acc_ref, sa_ref, sb_ref, p_ref):
    TQ, TK, G = Q_STEP, FAR_TILE, GROUP
    R = G * TQ
    NW = 2 * TQ
    c2 = pl.program_id(1)
    far_len = TQ * (c2 - 1)
    nfar = jnp.maximum((far_len + TK - 1) // TK, 0)
    nt_dims = (((1,), (1,)), ((), ()))

    for g in range(G):
        q2_ref[g * TQ:(g + 1) * TQ, :] = q_ref[:, g * HEAD_DIM:(g + 1) * HEAD_DIM]
    m_ref[...] = jnp.full((1, R), NEG, F32)
    acc_ref[...] = jnp.zeros(acc_ref.shape, F32)

    def far_start(j):
        return pl.multiple_of(jnp.maximum(far_len - TK * j, 0), LANES)

    def logits(start, width):
        return lax.dot_general(k_ref[pl.ds(start, width), :], q2_ref[...], nt_dims,
                               preferred_element_type=F32)

    def fold8(x, op):
        y = x[:SUBLANES]
        for i in range(1, x.shape[0] // SUBLANES):
            y = op(y, x[i * SUBLANES:(i + 1) * SUBLANES])
        return y

    def update(s_ref, start, width, biased):
        t0 = start // LANES
        CH = 64
        mx = jnp.full((SUBLANES, R), NEG, F32)
        for r in range(width // CH):
            rows = slice(r * CH, (r + 1) * CH)
            off = (r * CH) % LANES
            madd = mask_ref[0, t0 + (r * CH) // LANES, off:off + CH, :].astype(F32)
            x = s_ref[rows, :] + jnp.concatenate([madd] * G, axis=1)
            if biased:
                x = x + nb_ref[rows, :]
            s_ref[rows, :] = x
            mx = jnp.maximum(mx, fold8(x, jnp.maximum))
        m_old = m_ref[...]
        m_new = jnp.maximum(m_old, jnp.max(mx, axis=0, keepdims=True))
        alpha = jnp.exp2(m_old - m_new)
        for r in range(width // CH):
            rows = slice(r * CH, (r + 1) * CH)
            p_ref[rows, :] = jnp.exp2((s_ref[rows, :] - m_new).astype(BF16))
        vt = jnp.concatenate([v_ref[t0 + i] for i in range(width // LANES)], axis=1)
        acc_ref[...] = alpha * acc_ref[...] + jnp.dot(
            vt, p_ref[0:width, :], preferred_element_type=F32)
        m_ref[...] = m_new

    sa_ref[...] = logits(far_start(0), TK)
    near = pl.multiple_of(TQ * c2 + FRONT_PAD - TQ, LANES)
    sb_ref[0:NW, :] = logits(near, NW)
    update(sb_ref, near, NW, True)

    def pair(j):
        sb_ref[...] = logits(far_start(j + 1), TK)
        update(sa_ref, far_start(j), TK, False)
        sa_ref[...] = logits(far_start(j + 2), TK)
        update(sb_ref, far_start(j + 1), TK, False)

    def quad(i, _):
        pair(4 * i)
        pair(4 * i + 2)
        return 0

    nquad = nfar // 4
    lax.fori_loop(0, nquad, quad, 0)

    def rest(i, _):
        pair(4 * nquad + 2 * i)
        return 0

    lax.fori_loop(0, (nfar - 4 * nquad + 1) // 2, rest, 0)

    o = acc_ref[0:HEAD_DIM, :] / acc_ref[HEAD_DIM:HEAD_DIM + 1, :]
    for g in range(G):
        o_ref[:, g * HEAD_DIM:(g + 1) * HEAD_DIM] = o[:, g * TQ:(g + 1) * TQ].T.astype(o_ref.dtype)


def _attention_prompt(q, k, v, mask, nb):
    TQ, TK, G = Q_STEP, FAR_TILE, GROUP
    HD = q.shape[1]
    KV = HD // (G * HEAD_DIM)
    steps = mask.shape[0]
    T = steps * TQ
    ntile = mask.shape[1]
    Lp = k.shape[0]
    assert Lp == ntile * LANES and Lp >= FRONT_PAD + T and TQ == LANES
    R = G * TQ
    VR = HEAD_DIM + ONES_ROWS
    vt = v.reshape(ntile, LANES, KV, HEAD_DIM).transpose(2, 0, 3, 1)
    vt = jnp.concatenate([vt, jnp.ones((KV, ntile, ONES_ROWS, LANES), BF16)], axis=2)
    nbt = nb.reshape(KV, G, TQ, 2 * TQ).transpose(0, 3, 1, 2).reshape(KV, 2 * TQ, R)
    vmem = (2 * (2 * TQ * G * HEAD_DIM * 2 + 2 * Lp * HEAD_DIM * 2 + ntile * TQ * LANES * 2
                 + 2 * TQ * R * 4)
            + R * LANES * 6 + 2 * R * TK * 4 + 5 * R * TK * 4)
    q_map = lambda h, c: (c, h)
    return pl.pallas_call(
        _attn_prompt_kernel,
        grid=(KV, steps),
        in_specs=[pl.BlockSpec((TQ, G * HEAD_DIM), q_map),
                  pl.BlockSpec((Lp, HEAD_DIM), lambda h, c: (0, h)),
                  pl.BlockSpec((None, ntile, VR, LANES), lambda h, c: (h, 0, 0, 0)),
                  pl.BlockSpec((1, ntile, LANES, TQ), lambda h, c: (c, 0, 0, 0)),
                  pl.BlockSpec((None, 2 * TQ, R), lambda h, c: (h, 0, 0))],
        out_specs=pl.BlockSpec((TQ, G * HEAD_DIM), q_map),
        out_shape=jax.ShapeDtypeStruct((T, HD), BF16),
        scratch_shapes=[pltpu.VMEM((R, HEAD_DIM), BF16),
                        pltpu.VMEM((1, R), F32),
                        pltpu.VMEM((VR, R), F32),
                        pltpu.VMEM((TK, R), F32),
                        pltpu.VMEM((TK, R), F32),
                        pltpu.VMEM((TK, R), BF16)],
        compiler_params=_params(("parallel", "arbitrary"), vmem),
        name="dsa_attention_prompt",
    )(q, k, vt, mask, nbt)


def _t5_bucket_np(rel):
    nb = N_BUCKETS // 2
    max_exact = nb // 2
    side = np.where(rel > 0, nb, 0)
    n = np.abs(rel)
    nf = np.maximum(n, 1).astype(np.float32)
    large = max_exact + (np.log(nf / np.float32(max_exact))
                         / np.float32(math.log(REL_MAX_DIST / max_exact))
                         * np.float32(nb - max_exact)).astype(np.int32)
    large = np.minimum(large, nb - 1)
    return side + np.where(n < max_exact, n, large)


def _near_bias(rel_bias, rel, far_rel):
    far = int(_t5_bucket_np(np.array(far_rel)))
    assert far_rel < 0 and far == int(_t5_bucket_np(np.array(-10 ** 6)))
    rb = rel_bias.astype(F32) * LOG2E
    tab = rb[_t5_bucket_np(rel)] - rb[far][None, None, None, :]
    return tab.transpose(0, 3, 1, 2)


def _near_bias_stream(rel_bias, TQ, TK, phases):
    t = np.arange(TQ)[:, None]
    j = np.arange(2 * TK)[None, :]
    rel = np.stack([j - TK - ph + TQ - t for ph in phases])
    return _near_bias(rel_bias, rel, TQ - TK - 2)


def _near_bias_prompt(rel_bias):
    TQ = Q_STEP
    rel = np.arange(2 * TQ)[None, :] - TQ - np.arange(TQ)[:, None]
    return _near_bias(rel_bias, rel[None], -TQ - 1)[0]


def _conv_kernel(x_ref, halo_ref, buf_ref, w_ref, b_ref, xc_ref, xcb_ref, ext_ref, *, tt):
    first = pl.program_id(1) == 0
    ext_ref[0:SUBLANES, :] = jnp.where(first, buf_ref[...], halo_ref[...])
    ext_ref[SUBLANES:, :] = x_ref[...]
    y = jnp.broadcast_to(b_ref[...], x_ref.shape)
    for j in range(CONV_W):
        off = SUBLANES - (CONV_W - 1) + j
        y = y + ext_ref[off:off + tt, :] * w_ref[j:j + 1, :]
    xc_ref[...] = y
    xcb_ref[...] = y.astype(BF16)


def _conv(x, buf8, w, b, *, row0, B, T, tt):
    C = x.shape[1]
    assert row0 % tt == 0 and T % tt == 0 and tt % SUBLANES == 0
    nt = T // tt
    hb = tt // SUBLANES
    blk0 = row0 // tt
    main_map = lambda b, i: (blk0 + b * nt + i, 0)
    halo_map = lambda b, i: (jnp.maximum((blk0 + b * nt + i) * hb - 1, 0), 0)
    out_map = lambda b, i: (b * nt + i, 0)
    return pl.pallas_call(
        functools.partial(_conv_kernel, tt=tt),
        grid=(B, nt),
        in_specs=[pl.BlockSpec((tt, C), main_map),
                  pl.BlockSpec((SUBLANES, C), halo_map),
                  pl.BlockSpec((None, SUBLANES, C), lambda b, i: (b, 0, 0)),
                  pl.BlockSpec((CONV_W, C), lambda b, i: (0, 0)),
                  pl.BlockSpec((1, C), lambda b, i: (0, 0))],
        out_specs=[pl.BlockSpec((tt, C), out_map), pl.BlockSpec((tt, C), out_map)],
        out_shape=[jax.ShapeDtypeStruct((B * T, C), F32),
                   jax.ShapeDtypeStruct((B * T, C), BF16)],
        scratch_shapes=[pltpu.VMEM((tt + SUBLANES, C), F32)],
        compiler_params=_params(("parallel", "parallel"), 2 * tt * C * 10 + tt * C * 12),
        name="causal_conv",
    )(x, x, buf8, w, b.reshape(1, C))


def _scan_kernel(a_ref, u_ref, g_ref, h0_ref, y_ref, hlast_ref, h_ref, hs_ref, *, tt):
    i = pl.program_id(1)

    @pl.when(i == 0)
    def _():
        h_ref[...] = h0_ref[...]

    def step(t, h):
        h = a_ref[pl.ds(t, 1), :] * h + u_ref[pl.ds(t, 1), :]
        hs_ref[pl.ds(t, 1), :] = h
        return h

    h = lax.fori_loop(0, tt, step, h_ref[...])
    h_ref[...] = h
    hlast_ref[...] = h
    y_ref[...] = (hs_ref[...] * jax.nn.gelu(g_ref[...])).astype(y_ref.dtype)


def _scan(a, u, g, h0, *, row0, B, T, tt):
    C = a.shape[1]
    assert row0 % tt == 0 and T % tt == 0
    nt = T // tt
    blk0 = row0 // tt
    blk = pl.BlockSpec((tt, C), lambda b, i: (b * nt + i, 0))
    g_blk = pl.BlockSpec((tt, C), lambda b, i: (blk0 + b * nt + i, 0))
    vec = pl.BlockSpec((None, 1, C), lambda b, i: (b, 0, 0))
    return pl.pallas_call(
        functools.partial(_scan_kernel, tt=tt),
        grid=(B, nt),
        in_specs=[blk, blk, g_blk, vec],
        out_specs=[blk, vec],
        out_shape=[jax.ShapeDtypeStruct((B * T, C), BF16),
                   jax.ShapeDtypeStruct((B, 1, C), F32)],
        scratch_shapes=[pltpu.VMEM((1, C), F32), pltpu.VMEM((tt, C), F32)],
        compiler_params=_params(("parallel", "arbitrary"), 2 * tt * C * 14 + tt * C * 12),
        name="rglru_scan",
    )(a, u, g, h0)


GATE_TN = 256


def _gate_window(rb, C):
    raw = [((j * GATE_TN) // rb * rb) // LANES * LANES for j in range(C // GATE_TN)]
    ends = [((j * GATE_TN + GATE_TN - 1) // rb + 1) * rb for j in range(C // GATE_TN)]
    kw = -(-max(e - s for s, e in zip(raw, ends)) // LANES) * LANES
    kw = min(kw, C)
    starts = [min(s, C - kw) for s in raw]
    assert all(s + kw >= e for s, e in zip(starts, ends))
    return starts, kw


def _blockdiag_tiles(w):
    nblk, rb, _ = w.shape
    C = nblk * rb
    starts, kw = _gate_window(rb, C)
    dense = jax.scipy.linalg.block_diag(*[w[i] for i in range(nblk)])
    tiles = [dense[s:s + kw, j * GATE_TN:(j + 1) * GATE_TN] for j, s in enumerate(starts)]
    return jnp.stack(tiles).astype(BF16)


def _gates_kernel(x_ref, wa_ref, wx_ref, xc_ref, ba_ref, bx_ref, lam_ref, a_ref, u_ref, *, rb, kw):
    C = x_ref.shape[1]
    j = pl.program_id(1)
    start = jnp.minimum(((j * GATE_TN) // rb * rb) // LANES * LANES, C - kw)
    x = x_ref[:, pl.ds(pl.multiple_of(start, LANES), kw)]
    accs = [jnp.dot(x, w[0], preferred_element_type=F32) for w in (wa_ref, wx_ref)]
    a, u = _ep_rglru_gates(accs, [xc_ref[...]], [ba_ref[...], bx_ref[...], lam_ref[...]])
    a_ref[...] = a
    u_ref[...] = u


def _rglru_gates(xcb, xc, wa_t, wx_t, ba, bx, lam, *, rb, tm):
    rows, C = xc.shape
    nt, kw, _ = wa_t.shape
    tile = lambda: pl.BlockSpec((tm, GATE_TN), lambda i, j: (i, j))
    vec = lambda: pl.BlockSpec((1, GATE_TN), lambda i, j: (0, j))
    wspec = lambda: pl.BlockSpec((1, kw, GATE_TN), lambda i, j: (j, 0, 0))
    vmem = 2 * (tm * C * 2 + 2 * kw * GATE_TN * 2 + 3 * tm * GATE_TN * 4) + 8 * tm * GATE_TN * 4
    return pl.pallas_call(
        functools.partial(_gates_kernel, rb=rb, kw=kw),
        grid=(rows // tm, nt),
        in_specs=[pl.BlockSpec((tm, C), lambda i, j: (i, 0)), wspec(), wspec(), tile(),
                  vec(), vec(), vec()],
        out_specs=[tile(), tile()],
        out_shape=[jax.ShapeDtypeStruct((rows, C), F32)] * 2,
        compiler_params=_params(("parallel", "arbitrary"), vmem),
        name="rglru_gates",
    )(xcb, wa_t, wx_t, xc, ba.reshape(1, C), bx.reshape(1, C), lam.reshape(1, C))


def _pick(n, cands):
    for c in cands:
        if n % c == 0:
            return c
    raise ValueError(f"no tile for {n}")


def _layer(x, hist, p, rel_bias, dims):
    Tp, Bs, Ts, past = dims
    M, D = x.shape
    Ms = Bs * Ts
    k_past, v_past, ki_past, conv_buf, h0 = hist
    KV = k_past.shape[2]
    HQ = KV * GROUP * HEAD_DIM
    KVD = KV * HEAD_DIM
    DI = ki_past.shape[-1]
    C = p["conv_w"].shape[-1]
    n_in = p["w_in"].shape[1]
    HI = (n_in - HQ - 2 * KVD - DI - 2 * C - 2 * D) // (DI + 1)
    assert DI == LANES
    sizes = (HQ, KVD, KVD, HI * DI, DI, HI, C, C, D, D)
    offs = np.concatenate([[0], np.cumsum(sizes)])
    assert offs[-1] == n_in
    w_in = p["w_in"]

    def wslice(i, j=None):
        j = i if j is None else j
        return w_in[:, offs[i]:offs[j + 1]].astype(BF16)

    tm = _pick(M, (768, 512, 256, 128, 64))
    TK = KEY_TILE

    h = _rmsnorm(x, p["norm_mix"], tm)

    def proj(w, ep, dt, tn, vecs=(), name="in_proj"):
        return _matmul(h, [w], ep, [dt], tm=tm, tn=tn, vecs=vecs, name=name)[0]

    q = proj(wslice(0), _ep_headnorm(HEAD_DIM ** -0.5 * LOG2E), BF16, _pick(HQ, (512, 256, 128)),
             vecs=[jnp.tile(p["q_norm"], HQ // HEAD_DIM)], name="in_proj_q")
    k = proj(wslice(1), _ep_headnorm(1.0), F32, _pick(KVD, (512, 256, 128)),
             vecs=[jnp.tile(p["k_norm"], KV)], name="in_proj_k")
    v = proj(wslice(2), _ep_identity, F32, _pick(KVD, (512, 256, 128)), name="in_proj_v")
    qi = proj(wslice(3), _ep_identity, BF16, _pick(HI * DI, (512, 256, 128)), name="in_proj_qi")
    kw_w = jnp.pad(wslice(4, 5), ((0, 0), (0, 2 * LANES - DI - HI)))
    kiwi = proj(kw_w, _ep_identity, F32, 2 * LANES, name="in_proj_ki")
    ki, wi = kiwi[:, :DI], kiwi[:, DI:DI + HI]
    tn_c = _pick(C, (768, 384, 128))
    xr = proj(wslice(6), _ep_identity, F32, tn_c, name="in_proj_xr")
    gr = proj(wslice(7), _ep_identity, F32, tn_c, name="in_proj_gr")
    tn_d = _pick(D, (512, 256, 128))
    ga = proj(wslice(8), _ep_identity, F32, tn_d, name="in_proj_ga")
    gb = proj(wslice(9), _ep_identity, F32, tn_d, name="in_proj_gb")

    k_bf, v_bf, ki_bf = k.astype(BF16), v.astype(BF16), ki.astype(BF16)

    assert Tp % Q_STEP == 0
    back_p = -Tp % FAR_TILE
    n_sel_p = min(TOPK_MAX, Tp // 4)
    mask_p = _select_prompt(qi, wi, jnp.pad(ki_bf[:Tp], ((0, back_p), (0, 0))), n_sel_p, Tp)
    kv_pad = lambda a: jnp.pad(a[:Tp], ((FRONT_PAD, back_p), (0, 0)))
    o_p = _attention_prompt(q, kv_pad(k_bf), kv_pad(v_bf), mask_p, _near_bias_prompt(rel_bias))

    Ls = past + Ts
    nt_s = -(-Ls // TK)
    pad_s = nt_s * TK - Ls

    def with_cache(cache, new):
        new = new[Tp:].reshape(Bs, Ts, -1)
        parts = [cache.reshape(Bs, past, -1).astype(BF16), new]
        if pad_s:
            parts.append(jnp.zeros((Bs, pad_s, new.shape[-1]), BF16))
        return jnp.concatenate(parts, axis=1)

    n_sel_s = min(TOPK_MAX, Ls // 4)
    mask_s = _select(qi[Tp:], wi[Tp:], with_cache(ki_past, ki_bf), TQ=Ts, NT=nt_s,
                     lend_a=0, lend_b=Ls, n_sel=n_sel_s, batched=True)
    nb_s = _near_bias_stream(rel_bias, Ts, TK, [Ls - ((Ls - 1) // TK) * TK])
    o_s = _attention(q[Tp:], with_cache(k_past, k_bf), with_cache(v_past, v_bf), mask_s, nb_s,
                     TQ=Ts, lend_a=0, lend_b=Ls, batched=True)
    o_a = jnp.concatenate([o_p, o_s], axis=0)

    assert C % GATE_TN == 0
    wa_t = _blockdiag_tiles(p["rg_wa"])
    wx_t = _blockdiag_tiles(p["rg_wx"])

    def griffin(row0, B, T, buf, h_init):
        tt = _pick(math.gcd(T, row0) if row0 else T, (128, 64, 32, 16, 8))
        buf8 = jnp.pad(buf.astype(F32), ((0, 0), (SUBLANES - (CONV_W - 1), 0), (0, 0)))
        xc, xcb = _conv(xr, buf8, p["conv_w"], p["conv_b"], row0=row0, B=B, T=T, tt=tt)
        rows = B * T
        tmr = _pick(rows, (512, 256, 128, 64, 32))
        a, u = _rglru_gates(xcb, xc, wa_t, wx_t, p["rg_ba"], p["rg_bx"], p["rg_lambda"],
                            rb=p["rg_wa"].shape[1], tm=tmr)
        y, h_last = _scan(a, u, gr, h_init.reshape(B, 1, C), row0=row0, B=B, T=T, tt=tt)
        tail = xr[row0:row0 + rows].reshape(B, T, C)[:, -(CONV_W - 1):]
        conv_new = jnp.concatenate([buf.astype(F32), tail], axis=1)[:, -(CONV_W - 1):]
        return y, conv_new, h_last.reshape(B, C)

    y_p, conv_p, h_p = griffin(0, 1, Tp, jnp.zeros((1, CONV_W - 1, C), F32), jnp.zeros((1, C), F32))
    y_s, conv_s, h_s = griffin(Tp, Bs, Ts, conv_buf, h0)
    o_b = jnp.concatenate([y_p, y_s], axis=0)

    part = _matmul(o_a, [p["w_out_attn"].astype(BF16)], _ep_gate, [F32], tm=tm, tn=tn_d,
                   exts=[ga], name="out_attn")[0]
    merged = _matmul(o_b, [p["w_out_rg"].astype(BF16)], _ep_gate_add, [BF16], tm=tm, tn=tn_d,
                     exts=[gb, part], name="out_rg")[0]
    x1 = _matmul(merged, [p["w_o"].astype(BF16)], _ep_residual, [F32], tm=tm, tn=tn_d,
                 exts=[x], name="w_o")[0]
    hf = _rmsnorm(x1, p["norm_ffn"], tm)
    FF = p["ffn_w1"].shape[1]
    tn_f = _pick(FF, (512, 256, 128))
    act = _matmul(hf, [p["ffn_w1"].astype(BF16), p["ffn_w3"].astype(BF16)], _ep_swiglu, [BF16],
                  tm=tm, tn=tn_f, name="ffn_up")[0]
    tk_f = _pick(FF, (5504, 2816, 2048, 1024, 512, 256, 128))
    x2 = _matmul(act, [p["ffn_w2"].astype(BF16)], _ep_residual, [F32],
                 tm=tm, tn=tn_d, tk=tk_f, exts=[x1],
                 name="ffn_down")[0]

    new_p = (k[:Tp].reshape(1, Tp, KV, HEAD_DIM), v[:Tp].reshape(1, Tp, KV, HEAD_DIM),
             ki[:Tp].reshape(1, Tp, DI), conv_p, h_p)
    new_s = (k[Tp:].reshape(Bs, Ts, KV, HEAD_DIM), v[Tp:].reshape(Bs, Ts, KV, HEAD_DIM),
             ki[Tp:].reshape(Bs, Ts, DI), conv_s, h_s)
    return x2, new_p, new_s


def kernel(x_prompt, x_sample, cache_k, cache_v, cache_kidx, state_conv, state_rglru, norm_mix, w_in, q_norm, k_norm, rel_bias, conv_w, conv_b, rg_wa, rg_ba, rg_wx, rg_bx, rg_lambda, w_out_attn, w_out_rg, w_o, norm_ffn, ffn_w1, ffn_w3, ffn_w2):
    Bp, Tp, D = x_prompt.shape
    Bs, Ts, _ = x_sample.shape
    assert Bp == 1 and Tp % CHUNK == 0
    depth = w_in.shape[0]
    past = cache_k.shape[2]
    x = jnp.concatenate([x_prompt.reshape(Tp, D), x_sample.reshape(Bs * Ts, D)], axis=0)
    outs_p, outs_s = [], []
    for l in range(depth):
        p = dict(norm_mix=norm_mix[l], w_in=w_in[l], q_norm=q_norm[l], k_norm=k_norm[l],
                 conv_w=conv_w[l], conv_b=conv_b[l], rg_wa=rg_wa[l], rg_ba=rg_ba[l],
                 rg_wx=rg_wx[l], rg_bx=rg_bx[l], rg_lambda=rg_lambda[l],
                 w_out_attn=w_out_attn[l], w_out_rg=w_out_rg[l], w_o=w_o[l],
                 norm_ffn=norm_ffn[l], ffn_w1=ffn_w1[l], ffn_w3=ffn_w3[l], ffn_w2=ffn_w2[l])
        hist = (cache_k[l], cache_v[l], cache_kidx[l], state_conv[l], state_rglru[l])
        x, new_p, new_s = _layer(x, hist, p, rel_bias, (Tp, Bs, Ts, past))
        outs_p.append(new_p)
        outs_s.append(new_s)
    stack = lambda outs, i: jnp.stack([o[i] for o in outs])
    return (x[:Tp].reshape(1, Tp, D), x[Tp:].reshape(Bs, Ts, D),
            *[stack(outs_p, i) for i in range(5)],
            *[stack(outs_s, i) for i in range(5)])
```

```python
import functools
import math

import numpy as np
import jax
import jax.numpy as jnp
from jax import lax
from jax.experimental import pallas as pl
from jax.experimental.pallas import tpu as pltpu

F32 = jnp.float32
BF16 = jnp.bfloat16

CHUNK = 64
HEAD_DIM = 128
GROUP = 4
TOPK_MAX = 256
N_BUCKETS = 32
REL_MAX_DIST = 128
RG_C = 8.0
CONV_W = 4
EPS = 1e-6

LANES = 128
SUBLANES = 8
KEY_TILE = 256
Q_STEP = 2 * CHUNK
FAR_TILE = 512
FRONT_PAD = FAR_TILE
ONES_ROWS = 16
LOG2E = math.log2(math.e)
VMEM_CAP = 56 << 20
NEG = -1e30


def _params(sem, vmem_bytes):
    limit = min(max(int(vmem_bytes) + (6 << 20), 24 << 20), VMEM_CAP)
    return pltpu.CompilerParams(dimension_semantics=sem, vmem_limit_bytes=limit)


def _rmsnorm_kernel(x_ref, g_ref, o_ref):
    x = x_ref[...]
    ms = jnp.mean(x * x, axis=-1, keepdims=True)
    o_ref[...] = (x * lax.rsqrt(ms + EPS) * g_ref[...]).astype(o_ref.dtype)


def _rmsnorm(x, g, tm):
    M, D = x.shape
    tm = min(tm, 256)
    return pl.pallas_call(
        _rmsnorm_kernel,
        grid=(M // tm,),
        in_specs=[pl.BlockSpec((tm, D), lambda i: (i, 0)),
                  pl.BlockSpec((1, D), lambda i: (0, 0))],
        out_specs=pl.BlockSpec((tm, D), lambda i: (i, 0)),
        out_shape=jax.ShapeDtypeStruct((M, D), BF16),
        compiler_params=_params(("parallel",), 2 * tm * D * 6 + 2 * tm * D * 4),
        name="rmsnorm",
    )(x, g.reshape(1, D))


def _mm_kernel(*refs, n_w, n_ext, n_vec, n_out, nk, epilogue):
    x_ref = refs[0]
    w_refs = refs[1:1 + n_w]
    p = 1 + n_w
    ext_refs = refs[p:p + n_ext]
    p += n_ext
    vec_refs = refs[p:p + n_vec]
    p += n_vec
    out_refs = refs[p:p + n_out]
    acc_refs = refs[p + n_out:]

    def finish(accs):
        res = epilogue(accs, [e[...] for e in ext_refs], [v[...] for v in vec_refs])
        for o, r in zip(out_refs, res):
            o[...] = r.astype(o.dtype)

    if nk == 1:
        finish([jnp.dot(x_ref[...], w[...], preferred_element_type=F32) for w in w_refs])
        return

    k = pl.program_id(2)

    @pl.when(k == 0)
    def _():
        for a in acc_refs:
            a[...] = jnp.zeros_like(a)

    for a, w in zip(acc_refs, w_refs):
        a[...] += jnp.dot(x_ref[...], w[...], preferred_element_type=F32)

    @pl.when(k == nk - 1)
    def _():
        finish([a[...] for a in acc_refs])


def _matmul(x, ws, epilogue, out_dtypes, *, tm, tn, tk=None, exts=(), vecs=(), name="matmul"):
    M, K = x.shape
    n = ws[0].shape[1]
    tk = K if tk is None else tk
    nk = K // tk
    assert M % tm == 0 and n % tn == 0 and K % tk == 0
    x_map = lambda j, i, k: (i, k)
    w_map = lambda j, i, k: (k, j)
    mn_map = lambda j, i, k: (i, j)
    in_specs = [pl.BlockSpec((tm, tk), x_map)]
    in_specs += [pl.BlockSpec((tk, tn), w_map) for _ in ws]
    in_specs += [pl.BlockSpec((tm, tn), mn_map) for _ in exts]
    in_specs += [pl.BlockSpec((1, tn), lambda j, i, k: (0, j)) for _ in vecs]
    out_specs = [pl.BlockSpec((tm, tn), mn_map) for _ in out_dtypes]
    out_shape = [jax.ShapeDtypeStruct((M, n), dt) for dt in out_dtypes]
    scratch = [pltpu.VMEM((tm, tn), F32) for _ in ws] if nk > 1 else []
    vmem = (2 * (tm * tk * 2 + len(ws) * tk * tn * 2 + (len(exts) + len(out_dtypes)) * tm * tn * 4)
            + 3 * len(ws) * tm * tn * 4)
    kern = functools.partial(_mm_kernel, n_w=len(ws), n_ext=len(exts), n_vec=len(vecs),
                             n_out=len(out_dtypes), nk=nk, epilogue=epilogue)
    outs = pl.pallas_call(
        kern,
        grid=(n // tn, M // tm, nk),
        in_specs=in_specs,
        out_specs=out_specs,
        out_shape=out_shape,
        scratch_shapes=scratch,
        compiler_params=_params(("parallel", "parallel", "arbitrary"), vmem),
        name=name,
    )(x, *ws, *exts, *[v.reshape(1, -1) for v in vecs])
    return outs


def _ep_identity(accs, exts, vecs):
    return (accs[0],)


def _ep_headnorm(scale):
    def ep(accs, exts, vecs):
        a, g = accs[0], vecs[0]
        outs = []
        for j in range(a.shape[1] // HEAD_DIM):
            aj = a[:, j * HEAD_DIM:(j + 1) * HEAD_DIM]
            ms = jnp.mean(aj * aj, axis=-1, keepdims=True)
            outs.append(aj * lax.rsqrt(ms + EPS) * g[:, j * HEAD_DIM:(j + 1) * HEAD_DIM])
        y = jnp.concatenate(outs, axis=1)
        return (y * scale if scale != 1.0 else y,)
    return ep


def _ep_gate(accs, exts, vecs):
    return (jax.nn.sigmoid(exts[0]) * accs[0],)


def _ep_gate_add(accs, exts, vecs):
    return (exts[1] + jax.nn.sigmoid(exts[0]) * accs[0],)


def _ep_residual(accs, exts, vecs):
    return (exts[0] + accs[0],)


def _ep_swiglu(accs, exts, vecs):
    return (jax.nn.silu(accs[0]) * accs[1],)


def _softplus(x):
    return jnp.maximum(x, 0.0) + jnp.log1p(jnp.exp(-jnp.abs(x)))


def _ep_rglru_gates(accs, exts, vecs):
    xc = exts[0]
    r = jax.nn.sigmoid(accs[0] + vecs[0])
    i = jax.nn.sigmoid(accs[1] + vecs[1])
    log_a = -RG_C * r * _softplus(-vecs[2])
    a = jnp.exp(log_a)
    u = jnp.sqrt(1.0 - a * a) * (i * xc)
    return (a, u)


INT_MAX = 2 ** 31 - 1


def _open_rows(lo, hi):
    return jnp.max(jnp.where(lo < hi, 1.0, 0.0))


def _to_key(s):
    b = pltpu.bitcast(s, jnp.int32)
    return b ^ ((b >> 31) & jnp.int32(0x7FFFFFFF))


def _from_key(k):
    return pltpu.bitcast(k ^ ((k >> 31) & jnp.int32(0x7FFFFFFF)), F32)


def _kth_largest_key(count_ge, lo, hi, ksel, n_valid):
    def body(st):
        lo, hi, c_lo, _, it = st
        active = lo < hi
        mid_k = (lo >> 1) + (hi >> 1) + ((lo | hi) & 1)
        mid_f = _to_key(0.5 * _from_key(lo) + 0.5 * _from_key(hi))
        mid_f = jnp.minimum(jnp.maximum(mid_f, lo + 1), hi)
        mid = jnp.where((it & 1) == 0, mid_f, mid_k)
        c = count_ge(mid)
        ge = c >= ksel
        lo_n = jnp.where(ge, mid, lo)
        c_n = jnp.where(ge, c, c_lo)
        hi_n = jnp.where(c == ksel, mid, jnp.where(ge, hi, mid - 1))
        lo = jnp.where(active, lo_n, lo)
        c_lo = jnp.where(active, c_n, c_lo)
        hi = jnp.where(active, hi_n, hi)
        return lo, hi, c_lo, _open_rows(lo, hi), it + 1

    tau, _, c_tau, _, _ = lax.while_loop(lambda st: st[3] > 0.5, body,
                                         (lo, hi, n_valid, _open_rows(lo, hi), jnp.int32(0)))
    return tau, c_tau


def _tie_cut(count_tie_le, need, n_keys):
    def body(st):
        lo, hi, _ = st
        active = lo < hi
        mid = (lo + hi) >> 1
        ok = count_tie_le(mid) >= need
        hi = jnp.where(active, jnp.where(ok, mid, hi), hi)
        lo = jnp.where(active, jnp.where(ok, lo, mid + 1), lo)
        return lo, hi, _open_rows(lo, hi)

    lo = jnp.zeros(need.shape, jnp.int32)
    hi = jnp.full(need.shape, n_keys - 1, jnp.int32)
    cut, _, _ = lax.while_loop(lambda st: st[2] > 0.5, body, (lo, hi, _open_rows(lo, hi)))
    return cut


def _threshold_and_cut(key_ref, cut_ref, nt, lo, hi, ksel, n_valid):
    _, TQ, TK = key_ref.shape
    nl = TK // LANES
    lane = lax.broadcasted_iota(jnp.int32, (TQ, LANES), 1)

    def count(indicator):
        def body(j, c):
            kk = key_ref[j]
            for l in range(nl):
                c = c + indicator(kk[:, l * LANES:(l + 1) * LANES], j * TK + l * LANES)
            return c
        c = lax.fori_loop(0, nt, body, jnp.zeros((TQ, LANES), F32))
        return jnp.broadcast_to(jnp.sum(c, axis=1, keepdims=True), (TQ, LANES))

    tau, c_tau = _kth_largest_key(
        lambda mid: count(lambda kk, base: jnp.where(kk >= mid, 1.0, 0.0)), lo, hi, ksel, n_valid)
    cut_ref[...] = jnp.full((TQ, LANES), INT_MAX, jnp.int32)

    tied = jnp.max(jnp.where(c_tau > ksel, 1.0, 0.0)) > 0.5

    @pl.when(tied)
    def _():
        need = ksel - count(lambda kk, base: jnp.where(kk > tau, 1.0, 0.0))
        cut_ref[...] = _tie_cut(
            lambda mid: count(lambda kk, base: jnp.where(
                kk == tau, jnp.where(lane + base <= mid, 1.0, 0.0), 0.0)),
            need, nt * TK)

    return tau, cut_ref[...], tied


def _selected(kk, base, tau, cut):
    lane = lax.broadcasted_iota(jnp.int32, kk.shape, 1)
    tie = jnp.where(kk == tau, jnp.where(lane + base <= cut, 1, 0), 0)
    return jnp.where(kk > tau, 1, tie)


def _write_selection(write_tile, nt, tied):
    @pl.when(tied)
    def _():
        lax.fori_loop(0, nt, lambda j, _: write_tile(j, True) or 0, 0)

    @pl.when(jnp.logical_not(tied))
    def _():
        lax.fori_loop(0, nt, lambda j, _: write_tile(j, False) or 0, 0)
def _select_kernel(qi_ref, wi_ref, ki_ref, mask_ref, q2_ref, w2_ref, key_ref, cut_ref, *,
                   TQ, TK, NT, HI, lend_a, lend_b, n_sel, wscale):
    nl = TK // LANES
    lend = lend_a * pl.program_id(0) + lend_b
    nt = (lend + TK - 1) // TK
    ksel = jnp.minimum(n_sel, lend).astype(F32)

    wi = wi_ref[...] * wscale
    for h in range(HI):
        q2_ref[h * TQ:(h + 1) * TQ, :] = qi_ref[:, h * LANES:(h + 1) * LANES]
        w2_ref[h * TQ:(h + 1) * TQ, :] = jnp.broadcast_to(wi[:, h:h + 1], (TQ, LANES))

    lane = lax.broadcasted_iota(jnp.int32, (TQ, TK), 1)
    int_min = jnp.int32(-2 ** 31)

    def score_tile(j, carry):
        smin, smax = carry
        start = pl.multiple_of(j * TK, TK)
        kt = ki_ref[pl.ds(start, TK), :]
        s = lax.dot_general(q2_ref[...], kt, (((1,), (1,)), ((), ())),
                            preferred_element_type=F32)
        cols = []
        for l in range(nl):
            acc = jnp.zeros((TQ, LANES), F32)
            for h in range(HI):
                acc = acc + (jnp.maximum(s[h * TQ:(h + 1) * TQ, l * LANES:(l + 1) * LANES], 0.0)
                             * w2_ref[h * TQ:(h + 1) * TQ, :])
            cols.append(acc)
        sc = jnp.concatenate(cols, axis=1)
        valid = (lane + j * TK) < lend
        key_ref[j] = jnp.where(valid, _to_key(sc), int_min)
        lo_s = jnp.where(valid, sc, jnp.inf)
        hi_s = jnp.where(valid, sc, -jnp.inf)
        for l in range(nl):
            smin = jnp.minimum(smin, lo_s[:, l * LANES:(l + 1) * LANES])
            smax = jnp.maximum(smax, hi_s[:, l * LANES:(l + 1) * LANES])
        return smin, smax

    smin, smax = lax.fori_loop(
        0, nt, score_tile,
        (jnp.full((TQ, LANES), jnp.inf, F32), jnp.full((TQ, LANES), -jnp.inf, F32)))
    lo = _to_key(jnp.broadcast_to(jnp.min(smin, axis=1, keepdims=True), (TQ, LANES)))
    hi = _to_key(jnp.broadcast_to(jnp.max(smax, axis=1, keepdims=True), (TQ, LANES)))

    n_valid = jnp.full((TQ, LANES), lend, jnp.int32).astype(F32)
    tau, cut, tied = _threshold_and_cut(key_ref, cut_ref, nt, lo, hi,
                                        jnp.full((TQ, LANES), ksel, F32), n_valid)

    def write(j, exact_ties):
        kk = key_ref[j]
        sel = []
        for l in range(nl):
            slab = kk[:, l * LANES:(l + 1) * LANES]
            sel.append(_selected(slab, j * TK + l * LANES, tau, cut) if exact_ties
                       else jnp.where(slab >= tau, 1, 0))
        mask_ref[0, j] = jnp.concatenate(sel, axis=1).astype(jnp.int8)

    _write_selection(write, nt, tied)

    def clear(j, _):
        mask_ref[0, j] = jnp.zeros((TQ, TK), jnp.int8)
        return 0

    lax.fori_loop(nt, NT, clear, 0)


def _select(qi, wi, ki, *, TQ, NT, lend_a, lend_b, n_sel, batched):
    TK = KEY_TILE
    Mq = qi.shape[0]
    HI = wi.shape[1]
    steps = Mq // TQ
    if batched:
        ki_spec = pl.BlockSpec((None, NT * TK, LANES), lambda i: (i, 0, 0))
    else:
        ki_spec = pl.BlockSpec((NT * TK, LANES), lambda i: (0, 0))
    kern = functools.partial(_select_kernel, TQ=TQ, TK=TK, NT=NT, HI=HI, lend_a=lend_a,
                             lend_b=lend_b, n_sel=n_sel,
                             wscale=float(HI ** -0.5 * LANES ** -0.5))
    vmem = (2 * (TQ * HI * LANES * 2 + NT * TK * LANES * 2 + NT * TQ * TK)
            + HI * TQ * LANES * 6 + NT * TQ * TK * 4 + 3 * HI * TQ * TK * 4)
    return pl.pallas_call(
        kern,
        grid=(steps,),
        in_specs=[pl.BlockSpec((TQ, HI * LANES), lambda i: (i, 0)),
                  pl.BlockSpec((TQ, HI), lambda i: (i, 0)),
                  ki_spec],
        out_specs=pl.BlockSpec((1, NT, TQ, TK), lambda i: (i, 0, 0, 0)),
        out_shape=jax.ShapeDtypeStruct((steps, NT, TQ, TK), jnp.int8),
        scratch_shapes=[pltpu.VMEM((HI * TQ, LANES), BF16),
                        pltpu.VMEM((HI * TQ, LANES), F32),
                        pltpu.VMEM((NT, TQ, TK), jnp.int32),
                        pltpu.VMEM((TQ, LANES), jnp.int32)],
        compiler_params=_params(("parallel",), vmem),
        name="dsa_select",
    )(qi, wi, ki)


def _select_prompt_kernel(qi_ref, wi_ref, ki_ref, mask_ref, q2_ref, w2_ref, key_ref, acc_ref,
                          cut_ref, *,
                          HI, HG, NT, n_sel, wscale):
    TQ, TK = Q_STEP, FAR_TILE
    nl = TK // LANES
    pad_tiles = FRONT_PAD // LANES
    RB = 64
    c2 = pl.program_id(0)
    row = lax.broadcasted_iota(jnp.int32, (TQ, LANES), 0)
    lend = jnp.where(row < CHUNK, c2 * TQ + CHUNK, c2 * TQ + TQ)
    nt = (c2 * TQ + TQ + TK - 1) // TK
    ksel = jnp.minimum(n_sel, lend).astype(F32)

    wi = wi_ref[...] * wscale
    for h in range(HI):
        q2_ref[h * TQ:(h + 1) * TQ, :] = qi_ref[:, h * LANES:(h + 1) * LANES]
        w2_ref[h * TQ:(h + 1) * TQ, :] = jnp.broadcast_to(wi[:, h:h + 1], (TQ, LANES))

    lane = lax.broadcasted_iota(jnp.int32, (TQ, TK), 1)
    lend_t = jnp.concatenate([lend] * nl, axis=1)
    int_min = jnp.int32(-2 ** 31)

    def score_tile(j, carry):
        smin, smax = carry
        start = pl.multiple_of(j * TK, TK)
        kt = ki_ref[pl.ds(start, TK), :]
        for hg in range(HI // HG):
            s = lax.dot_general(q2_ref[hg * HG * TQ:(hg + 1) * HG * TQ, :], kt,
                                (((1,), (1,)), ((), ())), preferred_element_type=F32)
            for r in range(TQ // RB):
                cs = [None] * nl
                for h in range(HG):
                    r0 = h * TQ + r * RB
                    w = w2_ref[(hg * HG) * TQ + r0:(hg * HG) * TQ + r0 + RB, :]
                    for l in range(nl):
                        term = jnp.maximum(s[r0:r0 + RB, l * LANES:(l + 1) * LANES], 0.0) * w
                        cs[l] = term if cs[l] is None else cs[l] + term
                for l in range(nl):
                    if hg == 0:
                        acc_ref[r * RB:(r + 1) * RB, l * LANES:(l + 1) * LANES] = cs[l]
                    else:
                        acc_ref[r * RB:(r + 1) * RB, l * LANES:(l + 1) * LANES] += cs[l]
        sc = acc_ref[...]
        valid = (lane + j * TK) < lend_t
        key_ref[j] = jnp.where(valid, _to_key(sc), int_min)
        lo_s = jnp.where(valid, sc, jnp.inf)
        hi_s = jnp.where(valid, sc, -jnp.inf)
        for l in range(nl):
            smin = jnp.minimum(smin, lo_s[:, l * LANES:(l + 1) * LANES])
            smax = jnp.maximum(smax, hi_s[:, l * LANES:(l + 1) * LANES])
        return smin, smax

    smin, smax = lax.fori_loop(
        0, nt, score_tile,
        (jnp.full((TQ, LANES), jnp.inf, F32), jnp.full((TQ, LANES), -jnp.inf, F32)))
    lo = _to_key(jnp.broadcast_to(jnp.min(smin, axis=1, keepdims=True), (TQ, LANES)))
    hi = _to_key(jnp.broadcast_to(jnp.max(smax, axis=1, keepdims=True), (TQ, LANES)))

    tau, cut, tied = _threshold_and_cut(key_ref, cut_ref, nt, lo, hi, ksel, lend.astype(F32))

    neg_tile = jnp.full((TQ, LANES), NEG, BF16)
    for i in range(pad_tiles):
        mask_ref[0, i] = neg_tile

    def write(j, exact_ties):
        kk = key_ref[j]
        for i in range(nl):
            slab = kk[:, i * LANES:(i + 1) * LANES]
            keep = (_selected(slab, j * TK + i * LANES, tau, cut) > 0) if exact_ties else slab >= tau
            mask_ref[0, pad_tiles + nl * j + i] = jnp.where(keep, 0.0, NEG).T.astype(BF16)

    _write_selection(write, nt, tied)

    def clear(j, _):
        for i in range(nl):
            mask_ref[0, pad_tiles + nl * j + i] = neg_tile
        return 0

    lax.fori_loop(nt, NT, clear, 0)


def _select_prompt(qi, wi, ki, n_sel, T):
    TQ, TK = Q_STEP, FAR_TILE
    HI = wi.shape[1]
    HG = 4 if HI % 4 == 0 else 1
    NT = ki.shape[0] // TK
    steps = T // TQ
    ntile = (FRONT_PAD + NT * TK) // LANES
    kern = functools.partial(_select_prompt_kernel, HI=HI, HG=HG, NT=NT, n_sel=n_sel,
                             wscale=float(HI ** -0.5 * LANES ** -0.5))
    vmem = (2 * (TQ * HI * LANES * 2 + NT * TK * LANES * 2 + ntile * TQ * LANES * 2)
            + HI * TQ * LANES * 6 + NT * TQ * TK * 4 + TQ * TK * 4 + 4 * HG * TQ * TK * 4)
    return pl.pallas_call(
        kern,
        grid=(steps,),
        in_specs=[pl.BlockSpec((TQ, HI * LANES), lambda i: (i, 0)),
                  pl.BlockSpec((TQ, HI), lambda i: (i, 0)),
                  pl.BlockSpec((NT * TK, LANES), lambda i: (0, 0))],
        out_specs=pl.BlockSpec((1, ntile, TQ, LANES), lambda i: (i, 0, 0, 0)),
        out_shape=jax.ShapeDtypeStruct((steps, ntile, TQ, LANES), BF16),
        scratch_shapes=[pltpu.VMEM((HI * TQ, LANES), BF16),
                        pltpu.VMEM((HI * TQ, LANES), F32),
                        pltpu.VMEM((NT, TQ, TK), jnp.int32),
                        pltpu.VMEM((TQ, TK), F32),
                        pltpu.VMEM((TQ, LANES), jnp.int32)],
        compiler_params=_params(("parallel",), vmem),
        name="dsa_select_prompt",
    )(qi, wi, ki)


def _attn_kernel(q_ref, k_ref, v_ref, mask_ref, nb_ref, o_ref, q2_ref, m_ref, l_ref, acc_ref, *,
                 TQ, TK, G, lend_a, lend_b, step_axis):
    nl = TK // LANES
    R = G * TQ
    lend = lend_a * pl.program_id(step_axis) + lend_b
    jl = (lend - 1) // TK

    for g in range(G):
        q2_ref[g * TQ:(g + 1) * TQ, :] = q_ref[:, g * HEAD_DIM:(g + 1) * HEAD_DIM]
    m_ref[...] = jnp.full((R, LANES), NEG, F32)
    l_ref[...] = jnp.zeros((R, LANES), F32)
    acc_ref[...] = jnp.zeros((R, HEAD_DIM), F32)

    def tile(j, half):
        start = pl.multiple_of(j * TK, TK)
        kt = k_ref[pl.ds(start, TK), :]
        vt = v_ref[pl.ds(start, TK), :]
        s = lax.dot_general(q2_ref[...], kt, (((1,), (1,)), ((), ())),
                            preferred_element_type=F32)
        madd = jnp.where(mask_ref[0, j].astype(jnp.int32) != 0, 0.0, NEG)
        rows = []
        for g in range(G):
            sg = s[g * TQ:(g + 1) * TQ, :]
            if half is not None:
                sg = sg + nb_ref[0, g, :, half * TK:(half + 1) * TK]
            rows.append(sg + madd)
        s = jnp.concatenate(rows, axis=0)
        m_old = m_ref[...]
        m_cur = s[:, :LANES]
        for l in range(1, nl):
            m_cur = jnp.maximum(m_cur, s[:, l * LANES:(l + 1) * LANES])
        m_new = jnp.maximum(m_old, jnp.broadcast_to(jnp.max(m_cur, axis=1, keepdims=True), (R, LANES)))
        alpha = jnp.exp2(m_old - m_new)
        p = jnp.exp2(s - jnp.concatenate([m_new] * nl, axis=1))
        psum = p[:, :LANES]
        for l in range(1, nl):
            psum = psum + p[:, l * LANES:(l + 1) * LANES]
        l_ref[...] = alpha * l_ref[...] + jnp.broadcast_to(
            jnp.sum(psum, axis=1, keepdims=True), (R, LANES))
        acc_ref[...] = alpha * acc_ref[...] + jnp.dot(
            p.astype(BF16), vt, preferred_element_type=F32)
        m_ref[...] = m_new

    def far(j, _):
        tile(j, None)
        return 0

    lax.fori_loop(0, jnp.maximum(jl - 1, 0), far, 0)

    @pl.when(jl >= 1)
    def _():
        tile(jl - 1, 0)

    tile(jl, 1)

    o = acc_ref[...] / l_ref[...]
    for g in range(G):
        o_ref[:, g * HEAD_DIM:(g + 1) * HEAD_DIM] = o[g * TQ:(g + 1) * TQ, :].astype(o_ref.dtype)


def _attention(q, k, v, mask, nb, *, TQ, lend_a, lend_b, batched):
    TK = KEY_TILE
    G = GROUP
    Mq, HD = q.shape
    KV = HD // (G * HEAD_DIM)
    steps = Mq // TQ
    NT = mask.shape[1]
    L = k.shape[-2]
    P = nb.shape[0]
    if batched:
        grid = (steps, KV)
        q_map = lambda b, h: (b, h)
        kv_spec = pl.BlockSpec((None, L, HEAD_DIM), lambda b, h: (b, 0, h))
        mask_map = lambda b, h: (b, 0, 0, 0)
        nb_map = lambda b, h: (0, h, 0, 0)
        step_axis = 0
    else:
        grid = (KV, steps)
        q_map = lambda h, c: (c, h)
        kv_spec = pl.BlockSpec((L, HEAD_DIM), lambda h, c: (0, h))
        mask_map = lambda h, c: (c, 0, 0, 0)
        nb_map = lambda h, c: (c % P, h, 0, 0)
        step_axis = 1
    kern = functools.partial(_attn_kernel, TQ=TQ, TK=TK, G=G, lend_a=lend_a, lend_b=lend_b,
                             step_axis=step_axis)
    R = G * TQ
    vmem = (2 * (2 * TQ * G * HEAD_DIM * 2 + 2 * L * HEAD_DIM * 2 + NT * TQ * TK + G * TQ * 2 * TK * 4)
            + R * LANES * 14 + 6 * R * TK * 4)
    return pl.pallas_call(
        kern,
        grid=grid,
        in_specs=[pl.BlockSpec((TQ, G * HEAD_DIM), q_map),
                  kv_spec, kv_spec,
                  pl.BlockSpec((1, NT, TQ, TK), mask_map),
                  pl.BlockSpec((1, G, TQ, 2 * TK), nb_map)],
        out_specs=pl.BlockSpec((TQ, G * HEAD_DIM), q_map),
        out_shape=jax.ShapeDtypeStruct((Mq, HD), BF16),
        scratch_shapes=[pltpu.VMEM((R, HEAD_DIM), BF16),
                        pltpu.VMEM((R, LANES), F32),
                        pltpu.VMEM((R, LANES), F32),
                        pltpu.VMEM((R, HEAD_DIM), F32)],
        compiler_params=_params(("parallel", "arbitrary"), vmem),
        name="dsa_attention",
    )(q, k, v, mask, nb)


def _attn_prompt_kernel(q_ref, k_ref, v_ref, mask_ref, nb_ref, o_ref,
                        q2_ref, m_ref, acc_ref, sa_ref, sb_ref, p_ref):
    TQ, TK, G = Q_STEP, FAR_TILE, GROUP
    R = G * TQ
    NW = 2 * TQ
    c2 = pl.program_id(1)
    far_len = TQ * (c2 - 1)
    nfar = jnp.maximum((far_len + TK - 1) // TK, 0)
    nt_dims = (((1,), (1,)), ((), ()))

    for g in range(G):
        q2_ref[g * TQ:(g + 1) * TQ, :] = q_ref[:, g * HEAD_DIM:(g + 1) * HEAD_DIM]
    m_ref[...] = jnp.full((1, R), NEG, F32)
    acc_ref[...] = jnp.zeros(acc_ref.shape, F32)

    def far_start(j):
        return pl.multiple_of(jnp.maximum(far_len - TK * j, 0), LANES)

    def logits(start, width):
        return lax.dot_general(k_ref[pl.ds(start, width), :], q2_ref[...], nt_dims,
                               preferred_element_type=F32)

    def fold8(x, op):
        y = x[:SUBLANES]
        for i in range(1, x.shape[0] // SUBLANES):
            y = op(y, x[i * SUBLANES:(i + 1) * SUBLANES])
        return y

    def update(s_ref, start, width, biased):
        t0 = start // LANES
        CH = 64
        mx = jnp.full((SUBLANES, R), NEG, F32)
        for r in range(width // CH):
            rows = slice(r * CH, (r + 1) * CH)
            off = (r * CH) % LANES
            madd = mask_ref[0, t0 + (r * CH) // LANES, off:off + CH, :].astype(F32)
            x = s_ref[rows, :] + jnp.concatenate([madd] * G, axis=1)
            if biased:
                x = x + nb_ref[rows, :]
            s_ref[rows, :] = x
            mx = jnp.maximum(mx, fold8(x, jnp.maximum))
        m_old = m_ref[...]
        m_new = jnp.maximum(m_old, jnp.max(mx, axis=0, keepdims=True))
        alpha = jnp.exp2(m_old - m_new)
        for r in range(width // CH):
            rows = slice(r * CH, (r + 1) * CH)
            p_ref[rows, :] = jnp.exp2((s_ref[rows, :] - m_new).astype(BF16))
        vt = jnp.concatenate([v_ref[t0 + i] for i in range(width // LANES)], axis=1)
        acc_ref[...] = alpha * acc_ref[...] + jnp.dot(
            vt, p_ref[0:width, :], preferred_element_type=F32)
        m_ref[...] = m_new

    sa_ref[...] = logits(far_start(0), TK)
    near = pl.multiple_of(TQ * c2 + FRONT_PAD - TQ, LANES)
    sb_ref[0:NW, :] = logits(near, NW)
    update(sb_ref, near, NW, True)

    def pair(j):
        sb_ref[...] = logits(far_start(j + 1), TK)
        update(sa_ref, far_start(j), TK, False)
        sa_ref[...] = logits(far_start(j + 2), TK)
        update(sb_ref, far_start(j + 1), TK, False)

    def quad(i, _):
        pair(4 * i)
        pair(4 * i + 2)
        return 0

    nquad = nfar // 4
    lax.fori_loop(0, nquad, quad, 0)

    def rest(i, _):
        pair(4 * nquad + 2 * i)
        return 0

    lax.fori_loop(0, (nfar - 4 * nquad + 1) // 2, rest, 0)

    o = acc_ref[0:HEAD_DIM, :] / acc_ref[HEAD_DIM:HEAD_DIM + 1, :]
    for g in range(G):
        o_ref[:, g * HEAD_DIM:(g + 1) * HEAD_DIM] = o[:, g * TQ:(g + 1) * TQ].T.astype(o_ref.dtype)


def _attention_prompt(q, k, v, mask, nb):
    TQ, TK, G = Q_STEP, FAR_TILE, GROUP
    HD = q.shape[1]
    KV = HD // (G * HEAD_DIM)
    steps = mask.shape[0]
    T = steps * TQ
    ntile = mask.shape[1]
    Lp = k.shape[0]
    assert Lp == ntile * LANES and Lp >= FRONT_PAD + T and TQ == LANES
    R = G * TQ
    VR = HEAD_DIM + ONES_ROWS
    vt = v.reshape(ntile, LANES, KV, HEAD_DIM).transpose(2, 0, 3, 1)
    vt = jnp.concatenate([vt, jnp.ones((KV, ntile, ONES_ROWS, LANES), BF16)], axis=2)
    nbt = nb.reshape(KV, G, TQ, 2 * TQ).transpose(0, 3, 1, 2).reshape(KV, 2 * TQ, R)
    vmem = (2 * (2 * TQ * G * HEAD_DIM * 2 + 2 * Lp * HEAD_DIM * 2 + ntile * TQ * LANES * 2
                 + 2 * TQ * R * 4)
            + R * LANES * 6 + 2 * R * TK * 4 + 5 * R * TK * 4)
    q_map = lambda h, c: (c, h)
    return pl.pallas_call(
        _attn_prompt_kernel,
        grid=(KV, steps),
        in_specs=[pl.BlockSpec((TQ, G * HEAD_DIM), q_map),
                  pl.BlockSpec((Lp, HEAD_DIM), lambda h, c: (0, h)),
                  pl.BlockSpec((None, ntile, VR, LANES), lambda h, c: (h, 0, 0, 0)),
                  pl.BlockSpec((1, ntile, LANES, TQ), lambda h, c: (c, 0, 0, 0)),
                  pl.BlockSpec((None, 2 * TQ, R), lambda h, c: (h, 0, 0))],
        out_specs=pl.BlockSpec((TQ, G * HEAD_DIM), q_map),
        out_shape=jax.ShapeDtypeStruct((T, HD), BF16),
        scratch_shapes=[pltpu.VMEM((R, HEAD_DIM), BF16),
                        pltpu.VMEM((1, R), F32),
                        pltpu.VMEM((VR, R), F32),
                        pltpu.VMEM((TK, R), F32),
                        pltpu.VMEM((TK, R), F32),
                        pltpu.VMEM((TK, R), BF16)],
        compiler_params=_params(("parallel", "arbitrary"), vmem),
        name="dsa_attention_prompt",
    )(q, k, vt, mask, nbt)


def _t5_bucket_np(rel):
    nb = N_BUCKETS // 2
    max_exact = nb // 2
    side = np.where(rel > 0, nb, 0)
    n = np.abs(rel)
    nf = np.maximum(n, 1).astype(np.float32)
    large = max_exact + (np.log(nf / np.float32(max_exact))
                         / np.float32(math.log(REL_MAX_DIST / max_exact))
                         * np.float32(nb - max_exact)).astype(np.int32)
    large = np.minimum(large, nb - 1)
    return side + np.where(n < max_exact, n, large)


def _near_bias(rel_bias, rel, far_rel):
    far = int(_t5_bucket_np(np.array(far_rel)))
    assert far_rel < 0 and far == int(_t5_bucket_np(np.array(-10 ** 6)))
    rb = rel_bias.astype(F32) * LOG2E
    tab = rb[_t5_bucket_np(rel)] - rb[far][None, None, None, :]
    return tab.transpose(0, 3, 1, 2)


def _near_bias_stream(rel_bias, TQ, TK, phases):
    t = np.arange(TQ)[:, None]
    j = np.arange(2 * TK)[None, :]
    rel = np.stack([j - TK - ph + TQ - t for ph in phases])
    return _near_bias(rel_bias, rel, TQ - TK - 2)


def _near_bias_prompt(rel_bias):
    TQ = Q_STEP
    rel = np.arange(2 * TQ)[None, :] - TQ - np.arange(TQ)[:, None]
    return _near_bias(rel_bias, rel[None], -TQ - 1)[0]


def _conv_kernel(x_ref, halo_ref, buf_ref, w_ref, b_ref, xc_ref, xcb_ref, ext_ref, *, tt):
    first = pl.program_id(1) == 0
    ext_ref[0:SUBLANES, :] = jnp.where(first, buf_ref[...], halo_ref[...])
    ext_ref[SUBLANES:, :] = x_ref[...]
    y = jnp.broadcast_to(b_ref[...], x_ref.shape)
    for j in range(CONV_W):
        off = SUBLANES - (CONV_W - 1) + j
        y = y + ext_ref[off:off + tt, :] * w_ref[j:j + 1, :]
    xc_ref[...] = y
    xcb_ref[...] = y.astype(BF16)


def _conv(x, buf8, w, b, *, row0, B, T, tt):
    C = x.shape[1]
    assert row0 % tt == 0 and T % tt == 0 and tt % SUBLANES == 0
    nt = T // tt
    hb = tt // SUBLANES
    blk0 = row0 // tt
    main_map = lambda b, i: (blk0 + b * nt + i, 0)
    halo_map = lambda b, i: (jnp.maximum((blk0 + b * nt + i) * hb - 1, 0), 0)
    out_map = lambda b, i: (b * nt + i, 0)
    return pl.pallas_call(
        functools.partial(_conv_kernel, tt=tt),
        grid=(B, nt),
        in_specs=[pl.BlockSpec((tt, C), main_map),
                  pl.BlockSpec((SUBLANES, C), halo_map),
                  pl.BlockSpec((None, SUBLANES, C), lambda b, i: (b, 0, 0)),
                  pl.BlockSpec((CONV_W, C), lambda b, i: (0, 0)),
                  pl.BlockSpec((1, C), lambda b, i: (0, 0))],
        out_specs=[pl.BlockSpec((tt, C), out_map), pl.BlockSpec((tt, C), out_map)],
        out_shape=[jax.ShapeDtypeStruct((B * T, C), F32),
                   jax.ShapeDtypeStruct((B * T, C), BF16)],
        scratch_shapes=[pltpu.VMEM((tt + SUBLANES, C), F32)],
        compiler_params=_params(("parallel", "parallel"), 2 * tt * C * 10 + tt * C * 12),
        name="causal_conv",
    )(x, x, buf8, w, b.reshape(1, C))


def _scan_kernel(a_ref, u_ref, g_ref, h0_ref, y_ref, hlast_ref, h_ref, hs_ref, *, tt):
    i = pl.program_id(1)

    @pl.when(i == 0)
    def _():
        h_ref[...] = h0_ref[...]

    def step(t, h):
        h = a_ref[pl.ds(t, 1), :] * h + u_ref[pl.ds(t, 1), :]
        hs_ref[pl.ds(t, 1), :] = h
        return h

    h = lax.fori_loop(0, tt, step, h_ref[...])
    h_ref[...] = h
    hlast_ref[...] = h
    y_ref[...] = (hs_ref[...] * jax.nn.gelu(g_ref[...])).astype(y_ref.dtype)


def _scan(a, u, g, h0, *, row0, B, T, tt):
    C = a.shape[1]
    assert row0 % tt == 0 and T % tt == 0
    nt = T // tt
    blk0 = row0 // tt
    blk = pl.BlockSpec((tt, C), lambda b, i: (b * nt + i, 0))
    g_blk = pl.BlockSpec((tt, C), lambda b, i: (blk0 + b * nt + i, 0))
    vec = pl.BlockSpec((None, 1, C), lambda b, i: (b, 0, 0))
    return pl.pallas_call(
        functools.partial(_scan_kernel, tt=tt),
        grid=(B, nt),
        in_specs=[blk, blk, g_blk, vec],
        out_specs=[blk, vec],
        out_shape=[jax.ShapeDtypeStruct((B * T, C), BF16),
                   jax.ShapeDtypeStruct((B, 1, C), F32)],
        scratch_shapes=[pltpu.VMEM((1, C), F32), pltpu.VMEM((tt, C), F32)],
        compiler_params=_params(("parallel", "arbitrary"), 2 * tt * C * 14 + tt * C * 12),
        name="rglru_scan",
    )(a, u, g, h0)


GATE_TN = 256


def _gate_window(rb, C):
    raw = [((j * GATE_TN) // rb * rb) // LANES * LANES for j in range(C // GATE_TN)]
    ends = [((j * GATE_TN + GATE_TN - 1) // rb + 1) * rb for j in range(C // GATE_TN)]
    kw = -(-max(e - s for s, e in zip(raw, ends)) // LANES) * LANES
    kw = min(kw, C)
    starts = [min(s, C - kw) for s in raw]
    assert all(s + kw >= e for s, e in zip(starts, ends))
    return starts, kw


def _blockdiag_tiles(w):
    nblk, rb, _ = w.shape
    C = nblk * rb
    starts, kw = _gate_window(rb, C)
    dense = jax.scipy.linalg.block_diag(*[w[i] for i in range(nblk)])
    tiles = [dense[s:s + kw, j * GATE_TN:(j + 1) * GATE_TN] for j, s in enumerate(starts)]
    return jnp.stack(tiles).astype(BF16)


def _gates_kernel(x_ref, wa_ref, wx_ref, xc_ref, ba_ref, bx_ref, lam_ref, a_ref, u_ref, *, rb, kw):
    C = x_ref.shape[1]
    j = pl.program_id(1)
    start = jnp.minimum(((j * GATE_TN) // rb * rb) // LANES * LANES, C - kw)
    x = x_ref[:, pl.ds(pl.multiple_of(start, LANES), kw)]
    accs = [jnp.dot(x, w[0], preferred_element_type=F32) for w in (wa_ref, wx_ref)]
    a, u = _ep_rglru_gates(accs, [xc_ref[...]], [ba_ref[...], bx_ref[...], lam_ref[...]])
    a_ref[...] = a
    u_ref[...] = u


def _rglru_gates(xcb, xc, wa_t, wx_t, ba, bx, lam, *, rb, tm):
    rows, C = xc.shape
    nt, kw, _ = wa_t.shape
    tile = lambda: pl.BlockSpec((tm, GATE_TN), lambda i, j: (i, j))
    vec = lambda: pl.BlockSpec((1, GATE_TN), lambda i, j: (0, j))
    wspec = lambda: pl.BlockSpec((1, kw, GATE_TN), lambda i, j: (j, 0, 0))
    vmem = 2 * (tm * C * 2 + 2 * kw * GATE_TN * 2 + 3 * tm * GATE_TN * 4) + 8 * tm * GATE_TN * 4
    return pl.pallas_call(
        functools.partial(_gates_kernel, rb=rb, kw=kw),
        grid=(rows // tm, nt),
        in_specs=[pl.BlockSpec((tm, C), lambda i, j: (i, 0)), wspec(), wspec(), tile(),
                  vec(), vec(), vec()],
        out_specs=[tile(), tile()],
        out_shape=[jax.ShapeDtypeStruct((rows, C), F32)] * 2,
        compiler_params=_params(("parallel", "arbitrary"), vmem),
        name="rglru_gates",
    )(xcb, wa_t, wx_t, xc, ba.reshape(1, C), bx.reshape(1, C), lam.reshape(1, C))


def _pick(n, cands):
    for c in cands:
        if n % c == 0:
            return c
    raise ValueError(f"no tile for {n}")


def _layer(x, hist, p, rel_bias, dims):
    Tp, Bs, Ts, past = dims
    M, D = x.shape
    Ms = Bs * Ts
    k_past, v_past, ki_past, conv_buf, h0 = hist
    KV = k_past.shape[2]
    HQ = KV * GROUP * HEAD_DIM
    KVD = KV * HEAD_DIM
    DI = ki_past.shape[-1]
    C = p["conv_w"].shape[-1]
    n_in = p["w_in"].shape[1]
    HI = (n_in - HQ - 2 * KVD - DI - 2 * C - 2 * D) // (DI + 1)
    assert DI == LANES
    sizes = (HQ, KVD, KVD, HI * DI, DI, HI, C, C, D, D)
    offs = np.concatenate([[0], np.cumsum(sizes)])
    assert offs[-1] == n_in
    w_in = p["w_in"]

    def wslice(i, j=None):
        j = i if j is None else j
        return w_in[:, offs[i]:offs[j + 1]].astype(BF16)

    tm = _pick(M, (768, 512, 256, 128, 64))
    TK = KEY_TILE

    h = _rmsnorm(x, p["norm_mix"], tm)

    def proj(w, ep, dt, tn, vecs=(), name="in_proj"):
        return _matmul(h, [w], ep, [dt], tm=tm, tn=tn, vecs=vecs, name=name)[0]

    wide = (1024, 512, 256, 128)
    q = proj(wslice(0), _ep_headnorm(HEAD_DIM ** -0.5 * LOG2E), BF16, _pick(HQ, wide),
             vecs=[jnp.tile(p["q_norm"], HQ // HEAD_DIM)], name="in_proj_q")
    k = proj(wslice(1), _ep_headnorm(1.0), F32, _pick(KVD, wide),
             vecs=[jnp.tile(p["k_norm"], KV)], name="in_proj_k")
    v = proj(wslice(2), _ep_identity, F32, _pick(KVD, wide), name="in_proj_v")
    qi = proj(wslice(3), _ep_identity, BF16, _pick(HI * DI, wide), name="in_proj_qi")
    kw_w = jnp.pad(wslice(4, 5), ((0, 0), (0, 2 * LANES - DI - HI)))
    kiwi = proj(kw_w, _ep_identity, F32, 2 * LANES, name="in_proj_ki")
    ki, wi = kiwi[:, :DI], kiwi[:, DI:DI + HI]
    tn_c = _pick(C, (768, 384, 128))
    xr = proj(wslice(6), _ep_identity, F32, tn_c, name="in_proj_xr")
    gr = proj(wslice(7), _ep_identity, F32, tn_c, name="in_proj_gr")
    tn_d = _pick(D, (512, 256, 128))
    tn_w = _pick(D, wide)
    ga = proj(wslice(8), _ep_identity, F32, tn_w, name="in_proj_ga")
    gb = proj(wslice(9), _ep_identity, F32, tn_w, name="in_proj_gb")

    k_bf, v_bf, ki_bf = k.astype(BF16), v.astype(BF16), ki.astype(BF16)

    assert Tp % Q_STEP == 0
    back_p = -Tp % FAR_TILE
    n_sel_p = min(TOPK_MAX, Tp // 4)
    mask_p = _select_prompt(qi, wi, jnp.pad(ki_bf[:Tp], ((0, back_p), (0, 0))), n_sel_p, Tp)
    kv_pad = lambda a: jnp.pad(a[:Tp], ((FRONT_PAD, back_p), (0, 0)))
    o_p = _attention_prompt(q, kv_pad(k_bf), kv_pad(v_bf), mask_p, _near_bias_prompt(rel_bias))

    Ls = past + Ts
    nt_s = -(-Ls // TK)
    pad_s = nt_s * TK - Ls

    def with_cache(cache, new):
        new = new[Tp:].reshape(Bs, Ts, -1)
        parts = [cache.reshape(Bs, past, -1).astype(BF16), new]
        if pad_s:
            parts.append(jnp.zeros((Bs, pad_s, new.shape[-1]), BF16))
        return jnp.concatenate(parts, axis=1)

    n_sel_s = min(TOPK_MAX, Ls // 4)
    mask_s = _select(qi[Tp:], wi[Tp:], with_cache(ki_past, ki_bf), TQ=Ts, NT=nt_s,
                     lend_a=0, lend_b=Ls, n_sel=n_sel_s, batched=True)
    nb_s = _near_bias_stream(rel_bias, Ts, TK, [Ls - ((Ls - 1) // TK) * TK])
    o_s = _attention(q[Tp:], with_cache(k_past, k_bf), with_cache(v_past, v_bf), mask_s, nb_s,
                     TQ=Ts, lend_a=0, lend_b=Ls, batched=True)
    o_a = jnp.concatenate([o_p, o_s], axis=0)

    assert C % GATE_TN == 0
    wa_t = _blockdiag_tiles(p["rg_wa"])
    wx_t = _blockdiag_tiles(p["rg_wx"])

    def griffin(row0, B, T, buf, h_init):
        tt = _pick(math.gcd(T, row0) if row0 else T, (128, 64, 32, 16, 8))
        buf8 = jnp.pad(buf.astype(F32), ((0, 0), (SUBLANES - (CONV_W - 1), 0), (0, 0)))
        xc, xcb = _conv(xr, buf8, p["conv_w"], p["conv_b"], row0=row0, B=B, T=T, tt=tt)
        rows = B * T
        tmr = _pick(rows, (512, 256, 128, 64, 32))
        a, u = _rglru_gates(xcb, xc, wa_t, wx_t, p["rg_ba"], p["rg_bx"], p["rg_lambda"],
                            rb=p["rg_wa"].shape[1], tm=tmr)
        y, h_last = _scan(a, u, gr, h_init.reshape(B, 1, C), row0=row0, B=B, T=T, tt=tt)
        tail = xr[row0:row0 + rows].reshape(B, T, C)[:, -(CONV_W - 1):]
        conv_new = jnp.concatenate([buf.astype(F32), tail], axis=1)[:, -(CONV_W - 1):]
        return y, conv_new, h_last.reshape(B, C)

    y_p, conv_p, h_p = griffin(0, 1, Tp, jnp.zeros((1, CONV_W - 1, C), F32), jnp.zeros((1, C), F32))
    y_s, conv_s, h_s = griffin(Tp, Bs, Ts, conv_buf, h0)
    o_b = jnp.concatenate([y_p, y_s], axis=0)

    part = _matmul(o_a, [p["w_out_attn"].astype(BF16)], _ep_gate, [F32], tm=tm, tn=tn_w,
                   exts=[ga], name="out_attn")[0]
    merged = _matmul(o_b, [p["w_out_rg"].astype(BF16)], _ep_gate_add, [BF16], tm=tm, tn=tn_d,
                     exts=[gb, part], name="out_rg")[0]
    x1 = _matmul(merged, [p["w_o"].astype(BF16)], _ep_residual, [F32], tm=tm, tn=tn_w,
                 exts=[x], name="w_o")[0]
    hf = _rmsnorm(x1, p["norm_ffn"], tm)
    FF = p["ffn_w1"].shape[1]
    tn_f = _pick(FF, (512, 256, 128))
    act = _matmul(hf, [p["ffn_w1"].astype(BF16), p["ffn_w3"].astype(BF16)], _ep_swiglu, [BF16],
                  tm=tm, tn=tn_f, name="ffn_up")[0]
    tk_f = _pick(FF, (5504, 2816, 2048, 1024, 512, 256, 128))
    x2 = _matmul(act, [p["ffn_w2"].astype(BF16)], _ep_residual, [F32],
                 tm=tm, tn=tn_d, tk=tk_f, exts=[x1],
                 name="ffn_down")[0]

    new_p = (k[:Tp].reshape(1, Tp, KV, HEAD_DIM), v[:Tp].reshape(1, Tp, KV, HEAD_DIM),
             ki[:Tp].reshape(1, Tp, DI), conv_p, h_p)
    new_s = (k[Tp:].reshape(Bs, Ts, KV, HEAD_DIM), v[Tp:].reshape(Bs, Ts, KV, HEAD_DIM),
             ki[Tp:].reshape(Bs, Ts, DI), conv_s, h_s)
    return x2, new_p, new_s


def kernel(x_prompt, x_sample, cache_k, cache_v, cache_kidx, state_conv, state_rglru, norm_mix, w_in, q_norm, k_norm, rel_bias, conv_w, conv_b, rg_wa, rg_ba, rg_wx, rg_bx, rg_lambda, w_out_attn, w_out_rg, w_o, norm_ffn, ffn_w1, ffn_w3, ffn_w2):
    Bp, Tp, D = x_prompt.shape
    Bs, Ts, _ = x_sample.shape
    assert Bp == 1 and Tp % CHUNK == 0
    depth = w_in.shape[0]
    past = cache_k.shape[2]
    x = jnp.concatenate([x_prompt.reshape(Tp, D), x_sample.reshape(Bs * Ts, D)], axis=0)
    outs_p, outs_s = [], []
    for l in range(depth):
        p = dict(norm_mix=norm_mix[l], w_in=w_in[l], q_norm=q_norm[l], k_norm=k_norm[l],
                 conv_w=conv_w[l], conv_b=conv_b[l], rg_wa=rg_wa[l], rg_ba=rg_ba[l],
                 rg_wx=rg_wx[l], rg_bx=rg_bx[l], rg_lambda=rg_lambda[l],
                 w_out_attn=w_out_attn[l], w_out_rg=w_out_rg[l], w_o=w_o[l],
                 norm_ffn=norm_ffn[l], ffn_w1=ffn_w1[l], ffn_w3=ffn_w3[l], ffn_w2=ffn_w2[l])
        hist = (cache_k[l], cache_v[l], cache_kidx[l], state_conv[l], state_rglru[l])
        x, new_p, new_s = _layer(x, hist, p, rel_bias, (Tp, Bs, Ts, past))
        outs_p.append(new_p)
        outs_s.append(new_s)
    stack = lambda outs, i: jnp.stack([o[i] for o in outs])
    return (x[:Tp].reshape(1, Tp, D), x[Tp:].reshape(Bs, Ts, D),
            *[stack(outs_p, i) for i in range(5)],
            *[stack(outs_s, i) for i in range(5)])
```

```python
import functools
import math

import numpy as np
import jax
import jax.numpy as jnp
from jax import lax
from jax.experimental import pallas as pl
from jax.experimental.pallas import tpu as pltpu

F32 = jnp.float32
BF16 = jnp.bfloat16

CHUNK = 64
HEAD_DIM = 128
GROUP = 4
TOPK_MAX = 256
N_BUCKETS = 32
REL_MAX_DIST = 128
RG_C = 8.0
CONV_W = 4
EPS = 1e-6

LANES = 128
SUBLANES = 8
KEY_TILE = 256
Q_STEP = 2 * CHUNK
FAR_TILE = 512
FRONT_PAD = FAR_TILE
ONES_ROWS = 16
LOG2E = math.log2(math.e)
VMEM_CAP = 56 << 20
NEG = -1e30


def _params(sem, vmem_bytes):
    limit = min(max(int(vmem_bytes) + (6 << 20), 24 << 20), VMEM_CAP)
    return pltpu.CompilerParams(dimension_semantics=sem, vmem_limit_bytes=limit)


def _rmsnorm_kernel(x_ref, g_ref, o_ref):
    x = x_ref[...]
    ms = jnp.mean(x * x, axis=-1, keepdims=True)
    o_ref[...] = (x * lax.rsqrt(ms + EPS) * g_ref[...]).astype(o_ref.dtype)


def _rmsnorm(x, g, tm):
    M, D = x.shape
    tm = min(tm, 256)
    return pl.pallas_call(
        _rmsnorm_kernel,
        grid=(M // tm,),
        in_specs=[pl.BlockSpec((tm, D), lambda i: (i, 0)),
                  pl.BlockSpec((1, D), lambda i: (0, 0))],
        out_specs=pl.BlockSpec((tm, D), lambda i: (i, 0)),
        out_shape=jax.ShapeDtypeStruct((M, D), BF16),
        compiler_params=_params(("parallel",), 2 * tm * D * 6 + 2 * tm * D * 4),
        name="rmsnorm",
    )(x, g.reshape(1, D))


def _mm_kernel(*refs, n_w, n_ext, n_vec, n_out, nk, epilogue):
    x_ref = refs[0]
    w_refs = refs[1:1 + n_w]
    p = 1 + n_w
    ext_refs = refs[p:p + n_ext]
    p += n_ext
    vec_refs = refs[p:p + n_vec]
    p += n_vec
    out_refs = refs[p:p + n_out]
    acc_refs = refs[p + n_out:]

    def finish(accs):
        res = epilogue(accs, [e[...] for e in ext_refs], [v[...] for v in vec_refs])
        for o, r in zip(out_refs, res):
            o[...] = r.astype(o.dtype)

    if nk == 1:
        finish([jnp.dot(x_ref[...], w[...], preferred_element_type=F32) for w in w_refs])
        return

    k = pl.program_id(2)

    @pl.when(k == 0)
    def _():
        for a in acc_refs:
            a[...] = jnp.zeros_like(a)

    for a, w in zip(acc_refs, w_refs):
        a[...] += jnp.dot(x_ref[...], w[...], preferred_element_type=F32)

    @pl.when(k == nk - 1)
    def _():
        finish([a[...] for a in acc_refs])


def _matmul(x, ws, epilogue, out_dtypes, *, tm, tn, tk=None, exts=(), vecs=(), name="matmul"):
    M, K = x.shape
    n = ws[0].shape[1]
    tk = K if tk is None else tk
    nk = K // tk
    assert M % tm == 0 and n % tn == 0 and K % tk == 0
    x_map = lambda j, i, k: (i, k)
    w_map = lambda j, i, k: (k, j)
    mn_map = lambda j, i, k: (i, j)
    in_specs = [pl.BlockSpec((tm, tk), x_map)]
    in_specs += [pl.BlockSpec((tk, tn), w_map) for _ in ws]
    in_specs += [pl.BlockSpec((tm, tn), mn_map) for _ in exts]
    in_specs += [pl.BlockSpec((1, tn), lambda j, i, k: (0, j)) for _ in vecs]
    out_specs = [pl.BlockSpec((tm, tn), mn_map) for _ in out_dtypes]
    out_shape = [jax.ShapeDtypeStruct((M, n), dt) for dt in out_dtypes]
    scratch = [pltpu.VMEM((tm, tn), F32) for _ in ws] if nk > 1 else []
    vmem = (2 * (tm * tk * 2 + len(ws) * tk * tn * 2 + (len(exts) + len(out_dtypes)) * tm * tn * 4)
            + 3 * len(ws) * tm * tn * 4)
    kern = functools.partial(_mm_kernel, n_w=len(ws), n_ext=len(exts), n_vec=len(vecs),
                             n_out=len(out_dtypes), nk=nk, epilogue=epilogue)
    outs = pl.pallas_call(
        kern,
        grid=(n // tn, M // tm, nk),
        in_specs=in_specs,
        out_specs=out_specs,
        out_shape=out_shape,
        scratch_shapes=scratch,
        compiler_params=_params(("parallel", "parallel", "arbitrary"), vmem),
        name=name,
    )(x, *ws, *exts, *[v.reshape(1, -1) for v in vecs])
    return outs


def _ep_identity(accs, exts, vecs):
    return (accs[0],)


def _ep_headnorm(scale):
    def ep(accs, exts, vecs):
        a, g = accs[0], vecs[0]
        outs = []
        for j in range(a.shape[1] // HEAD_DIM):
            aj = a[:, j * HEAD_DIM:(j + 1) * HEAD_DIM]
            ms = jnp.mean(aj * aj, axis=-1, keepdims=True)
            outs.append(aj * lax.rsqrt(ms + EPS) * g[:, j * HEAD_DIM:(j + 1) * HEAD_DIM])
        y = jnp.concatenate(outs, axis=1)
        return (y * scale if scale != 1.0 else y,)
    return ep


def _ep_gate(accs, exts, vecs):
    return (jax.nn.sigmoid(exts[0]) * accs[0],)


def _ep_gate_add(accs, exts, vecs):
    return (exts[1] + jax.nn.sigmoid(exts[0]) * accs[0],)


def _ep_residual(accs, exts, vecs):
    return (exts[0] + accs[0],)


def _ep_swiglu(accs, exts, vecs):
    return (jax.nn.silu(accs[0]) * accs[1],)


def _softplus(x):
    return jnp.maximum(x, 0.0) + jnp.log1p(jnp.exp(-jnp.abs(x)))


def _ep_rglru_gates(accs, exts, vecs):
    xc = exts[0]
    r = jax.nn.sigmoid(accs[0] + vecs[0])
    i = jax.nn.sigmoid(accs[1] + vecs[1])
    log_a = -RG_C * r * _softplus(-vecs[2])
    a = jnp.exp(log_a)
    u = jnp.sqrt(1.0 - a * a) * (i * xc)
    return (a, u)


INT_MAX = 2 ** 31 - 1


def _open_rows(lo, hi):
    return jnp.max(jnp.where(lo < hi, 1.0, 0.0))


def _to_key(s):
    b = pltpu.bitcast(s, jnp.int32)
    return b ^ ((b >> 31) & jnp.int32(0x7FFFFFFF))


def _from_key(k):
    return pltpu.bitcast(k ^ ((k >> 31) & jnp.int32(0x7FFFFFFF)), F32)


def _kth_largest_key(count_ge, lo, hi, ksel, n_valid):
    def body(st):
        lo, hi, c_lo, _, it = st
        active = lo < hi
        mid_k = (lo >> 1) + (hi >> 1) + ((lo | hi) & 1)
        mid_f = _to_key(0.5 * _from_key(lo) + 0.5 * _from_key(hi))
        mid_f = jnp.minimum(jnp.maximum(mid_f, lo + 1), hi)
        mid = jnp.where((it & 1) == 0, mid_f, mid_k)
        c = count_ge(mid)
        ge = c >= ksel
        lo_n = jnp.where(ge, mid, lo)
        c_n = jnp.where(ge, c, c_lo)
        hi_n = jnp.where(c == ksel, mid, jnp.where(ge, hi, mid - 1))
        lo = jnp.where(active, lo_n, lo)
        c_lo = jnp.where(active, c_n, c_lo)
        hi = jnp.where(active, hi_n, hi)
        return lo, hi, c_lo, _open_rows(lo, hi), it + 1

    tau, _, c_tau, _, _ = lax.while_loop(lambda st: st[3] > 0.5, body,
                                         (lo, hi, n_valid, _open_rows(lo, hi), jnp.int32(0)))
    return tau, c_tau


def _tie_cut(count_tie_le, need, n_keys):
    def body(st):
        lo, hi, _ = st
        active = lo < hi
        mid = (lo + hi) >> 1
        ok = count_tie_le(mid) >= need
        hi = jnp.where(active, jnp.where(ok, mid, hi), hi)
        lo = jnp.where(active, jnp.where(ok, lo, mid + 1), lo)
        return lo, hi, _open_rows(lo, hi)

    lo = jnp.zeros(need.shape, jnp.int32)
    hi = jnp.full(need.shape, n_keys - 1, jnp.int32)
    cut, _, _ = lax.while_loop(lambda st: st[2] > 0.5, body, (lo, hi, _open_rows(lo, hi)))
    return cut


def _threshold_and_cut(key_ref, cut_ref, nt, lo, hi, ksel, n_valid):
    _, TQ, TK = key_ref.shape
    nl = TK // LANES
    lane = lax.broadcasted_iota(jnp.int32, (TQ, LANES), 1)

    def count(indicator):
        def body(j, c):
            kk = key_ref[j]
            for l in range(nl):
                c = c + indicator(kk[:, l * LANES:(l + 1) * LANES], j * TK + l * LANES)
            return c
        c = lax.fori_loop(0, nt, body, jnp.zeros((TQ, LANES), F32))
        return jnp.broadcast_to(jnp.sum(c, axis=1, keepdims=True), (TQ, LANES))

    tau, c_tau = _kth_largest_key(
        lambda mid: count(lambda kk, base: jnp.where(kk >= mid, 1.0, 0.0)), lo, hi, ksel, n_valid)
    cut_ref[...] = jnp.full((TQ, LANES), INT_MAX, jnp.int32)

    tied = jnp.max(jnp.where(c_tau > ksel, 1.0, 0.0)) > 0.5

    @pl.when(tied)
    def _():
        need = ksel - count(lambda kk, base: jnp.where(kk > tau, 1.0, 0.0))
        cut_ref[...] = _tie_cut(
            lambda mid: count(lambda kk, base: jnp.where(
                kk == tau, jnp.where(lane + base <= mid, 1.0, 0.0), 0.0)),
            need, nt * TK)

    return tau, cut_ref[...], tied


def _selected(kk, base, tau, cut):
    lane = lax.broadcasted_iota(jnp.int32, kk.shape, 1)
    tie = jnp.where(kk == tau, jnp.where(lane + base <= cut, 1, 0), 0)
    return jnp.where(kk > tau, 1, tie)


def _write_selection(write_tile, nt, tied):
    @pl.when(tied)
    def _():
        lax.fori_loop(0, nt, lambda j, _: write_tile(j, True) or 0, 0)

    @pl.when(jnp.logical_not(tied))
    def _():
        lax.fori_loop(0, nt, lambda j, _: write_tile(j, False) or 0, 0)
def _select_kernel(qi_ref, wi_ref, ki_ref, mask_ref, q2_ref, w2_ref, key_ref, cut_ref, *,
                   TQ, TK, NT, HI, lend_a, lend_b, n_sel, wscale):
    nl = TK // LANES
    lend = lend_a * pl.program_id(0) + lend_b
    nt = (lend + TK - 1) // TK
    ksel = jnp.minimum(n_sel, lend).astype(F32)

    wi = wi_ref[...] * wscale
    for h in range(HI):
        q2_ref[h * TQ:(h + 1) * TQ, :] = qi_ref[:, h * LANES:(h + 1) * LANES]
        w2_ref[h * TQ:(h + 1) * TQ, :] = jnp.broadcast_to(wi[:, h:h + 1], (TQ, LANES))

    lane = lax.broadcasted_iota(jnp.int32, (TQ, TK), 1)
    int_min = jnp.int32(-2 ** 31)

    def score_tile(j, carry):
        smin, smax = carry
        start = pl.multiple_of(j * TK, TK)
        kt = ki_ref[pl.ds(start, TK), :]
        s = lax.dot_general(q2_ref[...], kt, (((1,), (1,)), ((), ())),
                            preferred_element_type=F32)
        cols = []
        for l in range(nl):
            acc = jnp.zeros((TQ, LANES), F32)
            for h in range(HI):
                acc = acc + (jnp.maximum(s[h * TQ:(h + 1) * TQ, l * LANES:(l + 1) * LANES], 0.0)
                             * w2_ref[h * TQ:(h + 1) * TQ, :])
            cols.append(acc)
        sc = jnp.concatenate(cols, axis=1)
        valid = (lane + j * TK) < lend
        key_ref[j] = jnp.where(valid, _to_key(sc), int_min)
        lo_s = jnp.where(valid, sc, jnp.inf)
        hi_s = jnp.where(valid, sc, -jnp.inf)
        for l in range(nl):
            smin = jnp.minimum(smin, lo_s[:, l * LANES:(l + 1) * LANES])
            smax = jnp.maximum(smax, hi_s[:, l * LANES:(l + 1) * LANES])
        return smin, smax

    smin, smax = lax.fori_loop(
        0, nt, score_tile,
        (jnp.full((TQ, LANES), jnp.inf, F32), jnp.full((TQ, LANES), -jnp.inf, F32)))
    lo = _to_key(jnp.broadcast_to(jnp.min(smin, axis=1, keepdims=True), (TQ, LANES)))
    hi = _to_key(jnp.broadcast_to(jnp.max(smax, axis=1, keepdims=True), (TQ, LANES)))

    n_valid = jnp.full((TQ, LANES), lend, jnp.int32).astype(F32)
    tau, cut, tied = _threshold_and_cut(key_ref, cut_ref, nt, lo, hi,
                                        jnp.full((TQ, LANES), ksel, F32), n_valid)

    def write(j, exact_ties):
        kk = key_ref[j]
        sel = []
        for l in range(nl):
            slab = kk[:, l * LANES:(l + 1) * LANES]
            sel.append(_selected(slab, j * TK + l * LANES, tau, cut) if exact_ties
                       else jnp.where(slab >= tau, 1, 0))
        mask_ref[0, j] = jnp.concatenate(sel, axis=1).astype(jnp.int8)

    _write_selection(write, nt, tied)

    def clear(j, _):
        mask_ref[0, j] = jnp.zeros((TQ, TK), jnp.int8)
        return 0

    lax.fori_loop(nt, NT, clear, 0)


def _select(qi, wi, ki, *, TQ, NT, lend_a, lend_b, n_sel, batched):
    TK = KEY_TILE
    Mq = qi.shape[0]
    HI = wi.shape[1]
    steps = Mq // TQ
    if batched:
        ki_spec = pl.BlockSpec((None, NT * TK, LANES), lambda i: (i, 0, 0))
    else:
        ki_spec = pl.BlockSpec((NT * TK, LANES), lambda i: (0, 0))
    kern = functools.partial(_select_kernel, TQ=TQ, TK=TK, NT=NT, HI=HI, lend_a=lend_a,
                             lend_b=lend_b, n_sel=n_sel,
                             wscale=float(HI ** -0.5 * LANES ** -0.5))
    vmem = (2 * (TQ * HI * LANES * 2 + NT * TK * LANES * 2 + NT * TQ * TK)
            + HI * TQ * LANES * 6 + NT * TQ * TK * 4 + 3 * HI * TQ * TK * 4)
    return pl.pallas_call(
        kern,
        grid=(steps,),
        in_specs=[pl.BlockSpec((TQ, HI * LANES), lambda i: (i, 0)),
                  pl.BlockSpec((TQ, HI), lambda i: (i, 0)),
                  ki_spec],
        out_specs=pl.BlockSpec((1, NT, TQ, TK), lambda i: (i, 0, 0, 0)),
        out_shape=jax.ShapeDtypeStruct((steps, NT, TQ, TK), jnp.int8),
        scratch_shapes=[pltpu.VMEM((HI * TQ, LANES), BF16),
                        pltpu.VMEM((HI * TQ, LANES), F32),
                        pltpu.VMEM((NT, TQ, TK), jnp.int32),
                        pltpu.VMEM((TQ, LANES), jnp.int32)],
        compiler_params=_params(("parallel",), vmem),
        name="dsa_select",
    )(qi, wi, ki)


def _select_prompt_kernel(qi_ref, wi_ref, ki_ref, mask_ref, q2_ref, w2_ref, key_ref, acc_ref,
                          cut_ref, *,
                          HI, HG, NT, n_sel, wscale):
    TQ, TK = Q_STEP, FAR_TILE
    nl = TK // LANES
    pad_tiles = FRONT_PAD // LANES
    RB = 64
    c2 = pl.program_id(0)
    row = lax.broadcasted_iota(jnp.int32, (TQ, LANES), 0)
    lend = jnp.where(row < CHUNK, c2 * TQ + CHUNK, c2 * TQ + TQ)
    nt = (c2 * TQ + TQ + TK - 1) // TK
    ksel = jnp.minimum(n_sel, lend).astype(F32)

    wi = wi_ref[...] * wscale
    for h in range(HI):
        q2_ref[h * TQ:(h + 1) * TQ, :] = qi_ref[:, h * LANES:(h + 1) * LANES]
        w2_ref[h * TQ:(h + 1) * TQ, :] = jnp.broadcast_to(wi[:, h:h + 1], (TQ, LANES))

    lane = lax.broadcasted_iota(jnp.int32, (TQ, TK), 1)
    lend_t = jnp.concatenate([lend] * nl, axis=1)
    int_min = jnp.int32(-2 ** 31)

    def score_tile(j, carry):
        smin, smax = carry
        start = pl.multiple_of(j * TK, TK)
        kt = ki_ref[pl.ds(start, TK), :]
        for hg in range(HI // HG):
            s = lax.dot_general(q2_ref[hg * HG * TQ:(hg + 1) * HG * TQ, :], kt,
                                (((1,), (1,)), ((), ())), preferred_element_type=F32)
            for r in range(TQ // RB):
                cs = [None] * nl
                for h in range(HG):
                    r0 = h * TQ + r * RB
                    w = w2_ref[(hg * HG) * TQ + r0:(hg * HG) * TQ + r0 + RB, :]
                    for l in range(nl):
                        term = jnp.maximum(s[r0:r0 + RB, l * LANES:(l + 1) * LANES], 0.0) * w
                        cs[l] = term if cs[l] is None else cs[l] + term
                for l in range(nl):
                    if hg == 0:
                        acc_ref[r * RB:(r + 1) * RB, l * LANES:(l + 1) * LANES] = cs[l]
                    else:
                        acc_ref[r * RB:(r + 1) * RB, l * LANES:(l + 1) * LANES] += cs[l]
        sc = acc_ref[...]
        valid = (lane + j * TK) < lend_t
        key_ref[j] = jnp.where(valid, _to_key(sc), int_min)
        lo_s = jnp.where(valid, sc, jnp.inf)
        hi_s = jnp.where(valid, sc, -jnp.inf)
        for l in range(nl):
            smin = jnp.minimum(smin, lo_s[:, l * LANES:(l + 1) * LANES])
            smax = jnp.maximum(smax, hi_s[:, l * LANES:(l + 1) * LANES])
        return smin, smax

    smin, smax = lax.fori_loop(
        0, nt, score_tile,
        (jnp.full((TQ, LANES), jnp.inf, F32), jnp.full((TQ, LANES), -jnp.inf, F32)))
    lo = _to_key(jnp.broadcast_to(jnp.min(smin, axis=1, keepdims=True), (TQ, LANES)))
    hi = _to_key(jnp.broadcast_to(jnp.max(smax, axis=1, keepdims=True), (TQ, LANES)))

    tau, cut, tied = _threshold_and_cut(key_ref, cut_ref, nt, lo, hi, ksel, lend.astype(F32))

    neg_tile = jnp.full((TQ, LANES), NEG, BF16)
    for i in range(pad_tiles):
        mask_ref[0, i] = neg_tile

    def write(j, exact_ties):
        kk = key_ref[j]
        for i in range(nl):
            slab = kk[:, i * LANES:(i + 1) * LANES]
            keep = (_selected(slab, j * TK + i * LANES, tau, cut) > 0) if exact_ties else slab >= tau
            mask_ref[0, pad_tiles + nl * j + i] = jnp.where(keep, 0.0, NEG).T.astype(BF16)

    _write_selection(write, nt, tied)

    def clear(j, _):
        for i in range(nl):
            mask_ref[0, pad_tiles + nl * j + i] = neg_tile
        return 0

    lax.fori_loop(nt, NT, clear, 0)


def _select_prompt(qi, wi, ki, n_sel, T):
    TQ, TK = Q_STEP, FAR_TILE
    HI = wi.shape[1]
    HG = 4 if HI % 4 == 0 else 1
    NT = ki.shape[0] // TK
    steps = T // TQ
    ntile = (FRONT_PAD + NT * TK) // LANES
    kern = functools.partial(_select_prompt_kernel, HI=HI, HG=HG, NT=NT, n_sel=n_sel,
                             wscale=float(HI ** -0.5 * LANES ** -0.5))
    vmem = (2 * (TQ * HI * LANES * 2 + NT * TK * LANES * 2 + ntile * TQ * LANES * 2)
            + HI * TQ * LANES * 6 + NT * TQ * TK * 4 + TQ * TK * 4 + 4 * HG * TQ * TK * 4)
    return pl.pallas_call(
        kern,
        grid=(steps,),
        in_specs=[pl.BlockSpec((TQ, HI * LANES), lambda i: (i, 0)),
                  pl.BlockSpec((TQ, HI), lambda i: (i, 0)),
                  pl.BlockSpec((NT * TK, LANES), lambda i: (0, 0))],
        out_specs=pl.BlockSpec((1, ntile, TQ, LANES), lambda i: (i, 0, 0, 0)),
        out_shape=jax.ShapeDtypeStruct((steps, ntile, TQ, LANES), BF16),
        scratch_shapes=[pltpu.VMEM((HI * TQ, LANES), BF16),
                        pltpu.VMEM((HI * TQ, LANES), F32),
                        pltpu.VMEM((NT, TQ, TK), jnp.int32),
                        pltpu.VMEM((TQ, TK), F32),
                        pltpu.VMEM((TQ, LANES), jnp.int32)],
        compiler_params=_params(("parallel",), vmem),
        name="dsa_select_prompt",
    )(qi, wi, ki)


def _attn_kernel(q_ref, k_ref, v_ref, mask_ref, nb_ref, o_ref, q2_ref, m_ref, l_ref, acc_ref, *,
                 TQ, TK, G, lend_a, lend_b, step_axis):
    nl = TK // LANES
    R = G * TQ
    lend = lend_a * pl.program_id(step_axis) + lend_b
    jl = (lend - 1) // TK

    for g in range(G):
        q2_ref[g * TQ:(g + 1) * TQ, :] = q_ref[:, g * HEAD_DIM:(g + 1) * HEAD_DIM]
    m_ref[...] = jnp.full((R, LANES), NEG, F32)
    l_ref[...] = jnp.zeros((R, LANES), F32)
    acc_ref[...] = jnp.zeros((R, HEAD_DIM), F32)

    def tile(j, half):
        start = pl.multiple_of(j * TK, TK)
        kt = k_ref[pl.ds(start, TK), :]
        vt = v_ref[pl.ds(start, TK), :]
        s = lax.dot_general(q2_ref[...], kt, (((1,), (1,)), ((), ())),
                            preferred_element_type=F32)
        madd = jnp.where(mask_ref[0, j].astype(jnp.int32) != 0, 0.0, NEG)
        rows = []
        for g in range(G):
            sg = s[g * TQ:(g + 1) * TQ, :]
            if half is not None:
                sg = sg + nb_ref[0, g, :, half * TK:(half + 1) * TK]
            rows.append(sg + madd)
        s = jnp.concatenate(rows, axis=0)
        m_old = m_ref[...]
        m_cur = s[:, :LANES]
        for l in range(1, nl):
            m_cur = jnp.maximum(m_cur, s[:, l * LANES:(l + 1) * LANES])
        m_new = jnp.maximum(m_old, jnp.broadcast_to(jnp.max(m_cur, axis=1, keepdims=True), (R, LANES)))
        alpha = jnp.exp2(m_old - m_new)
        p = jnp.exp2(s - jnp.concatenate([m_new] * nl, axis=1))
        psum = p[:, :LANES]
        for l in range(1, nl):
            psum = psum + p[:, l * LANES:(l + 1) * LANES]
        l_ref[...] = alpha * l_ref[...] + jnp.broadcast_to(
            jnp.sum(psum, axis=1, keepdims=True), (R, LANES))
        acc_ref[...] = alpha * acc_ref[...] + jnp.dot(
            p.astype(BF16), vt, preferred_element_type=F32)
        m_ref[...] = m_new

    def far(j, _):
        tile(j, None)
        return 0

    lax.fori_loop(0, jnp.maximum(jl - 1, 0), far, 0)

    @pl.when(jl >= 1)
    def _():
        tile(jl - 1, 0)

    tile(jl, 1)

    o = acc_ref[...] / l_ref[...]
    for g in range(G):
        o_ref[:, g * HEAD_DIM:(g + 1) * HEAD_DIM] = o[g * TQ:(g + 1) * TQ, :].astype(o_ref.dtype)


def _attention(q, k, v, mask, nb, *, TQ, lend_a, lend_b, batched):
    TK = KEY_TILE
    G = GROUP
    Mq, HD = q.shape
    KV = HD // (G * HEAD_DIM)
    steps = Mq // TQ
    NT = mask.shape[1]
    L = k.shape[-2]
    P = nb.shape[0]
    if batched:
        grid = (steps, KV)
        q_map = lambda b, h: (b, h)
        kv_spec = pl.BlockSpec((None, L, HEAD_DIM), lambda b, h: (b, 0, h))
        mask_map = lambda b, h: (b, 0, 0, 0)
        nb_map = lambda b, h: (0, h, 0, 0)
        step_axis = 0
    else:
        grid = (KV, steps)
        q_map = lambda h, c: (c, h)
        kv_spec = pl.BlockSpec((L, HEAD_DIM), lambda h, c: (0, h))
        mask_map = lambda h, c: (c, 0, 0, 0)
        nb_map = lambda h, c: (c % P, h, 0, 0)
        step_axis = 1
    kern = functools.partial(_attn_kernel, TQ=TQ, TK=TK, G=G, lend_a=lend_a, lend_b=lend_b,
                             step_axis=step_axis)
    R = G * TQ
    vmem = (2 * (2 * TQ * G * HEAD_DIM * 2 + 2 * L * HEAD_DIM * 2 + NT * TQ * TK + G * TQ * 2 * TK * 4)
            + R * LANES * 14 + 6 * R * TK * 4)
    return pl.pallas_call(
        kern,
        grid=grid,
        in_specs=[pl.BlockSpec((TQ, G * HEAD_DIM), q_map),
                  kv_spec, kv_spec,
                  pl.BlockSpec((1, NT, TQ, TK), mask_map),
                  pl.BlockSpec((1, G, TQ, 2 * TK), nb_map)],
        out_specs=pl.BlockSpec((TQ, G * HEAD_DIM), q_map),
        out_shape=jax.ShapeDtypeStruct((Mq, HD), BF16),
        scratch_shapes=[pltpu.VMEM((R, HEAD_DIM), BF16),
                        pltpu.VMEM((R, LANES), F32),
                        pltpu.VMEM((R, LANES), F32),
                        pltpu.VMEM((R, HEAD_DIM), F32)],
        compiler_params=_params(("parallel", "arbitrary"), vmem),
        name="dsa_attention",
    )(q, k, v, mask, nb)


def _attn_prompt_kernel(q_ref, k_ref, v_ref, mask_ref, nb_ref, o_ref,
                        q2_ref, m_ref, acc_ref, sa_ref, sb_ref, p_ref):
    TQ, TK, G = Q_STEP, FAR_TILE, GROUP
    R = G * TQ
    NW = 2 * TQ
    c2 = pl.program_id(1)
    far_len = TQ * (c2 - 1)
    nfar = jnp.maximum((far_len + TK - 1) // TK, 0)
    nt_dims = (((1,), (1,)), ((), ()))

    for g in range(G):
        q2_ref[g * TQ:(g + 1) * TQ, :] = q_ref[:, g * HEAD_DIM:(g + 1) * HEAD_DIM]
    m_ref[...] = jnp.full((1, R), NEG, F32)
    acc_ref[...] = jnp.zeros(acc_ref.shape, F32)

    def far_start(j):
        return pl.multiple_of(jnp.maximum(far_len - TK * j, 0), LANES)

    def logits(start, width):
        return lax.dot_general(k_ref[pl.ds(start, width), :], q2_ref[...], nt_dims,
                               preferred_element_type=F32)

    def fold8(x, op):
        y = x[:SUBLANES]
        for i in range(1, x.shape[0] // SUBLANES):
            y = op(y, x[i * SUBLANES:(i + 1) * SUBLANES])
        return y

    def update(s_ref, start, width, biased):
        t0 = start // LANES
        CH = 64
        mx = jnp.full((SUBLANES, R), NEG, F32)
        for r in range(width // CH):
            rows = slice(r * CH, (r + 1) * CH)
            off = (r * CH) % LANES
            madd = mask_ref[0, t0 + (r * CH) // LANES, off:off + CH, :].astype(F32)
            x = s_ref[rows, :] + jnp.concatenate([madd] * G, axis=1)
            if biased:
                x = x + nb_ref[rows, :]
            s_ref[rows, :] = x
            mx = jnp.maximum(mx, fold8(x, jnp.maximum))
        m_old = m_ref[...]
        m_new = jnp.maximum(m_old, jnp.max(mx, axis=0, keepdims=True))
        alpha = jnp.exp2(m_old - m_new)
        for r in range(width // CH):
            rows = slice(r * CH, (r + 1) * CH)
            p_ref[rows, :] = jnp.exp2((s_ref[rows, :] - m_new).astype(BF16))
        vt = jnp.concatenate([v_ref[t0 + i] for i in range(width // LANES)], axis=1)
        acc_ref[...] = alpha * acc_ref[...] + jnp.dot(
            vt, p_ref[0:width, :], preferred_element_type=F32)
        m_ref[...] = m_new

    sa_ref[...] = logits(far_start(0), TK)
    near = pl.multiple_of(TQ * c2 + FRONT_PAD - TQ, LANES)
    sb_ref[0:NW, :] = logits(near, NW)
    update(sb_ref, near, NW, True)

    def pair(j):
        sb_ref[...] = logits(far_start(j + 1), TK)
        update(sa_ref, far_start(j), TK, False)
        sa_ref[...] = logits(far_start(j + 2), TK)
        update(sb_ref, far_start(j + 1), TK, False)

    def quad(i, _):
        pair(4 * i)
        pair(4 * i + 2)
        return 0

    nquad = nfar // 4
    lax.fori_loop(0, nquad, quad, 0)

    def rest(i, _):
        pair(4 * nquad + 2 * i)
        return 0

    lax.fori_loop(0, (nfar - 4 * nquad + 1) // 2, rest, 0)

    o = acc_ref[0:HEAD_DIM, :] / acc_ref[HEAD_DIM:HEAD_DIM + 1, :]
    for g in range(G):
        o_ref[:, g * HEAD_DIM:(g + 1) * HEAD_DIM] = o[:, g * TQ:(g + 1) * TQ].T.astype(o_ref.dtype)


def _attention_prompt(q, k, v, mask, nb):
    TQ, TK, G = Q_STEP, FAR_TILE, GROUP
    HD = q.shape[1]
    KV = HD // (G * HEAD_DIM)
    steps = mask.shape[0]
    T = steps * TQ
    ntile = mask.shape[1]
    Lp = k.shape[0]
    assert Lp == ntile * LANES and Lp >= FRONT_PAD + T and TQ == LANES
    R = G * TQ
    VR = HEAD_DIM + ONES_ROWS
    vt = v.reshape(ntile, LANES, KV, HEAD_DIM).transpose(2, 0, 3, 1)
    vt = jnp.concatenate([vt, jnp.ones((KV, ntile, ONES_ROWS, LANES), BF16)], axis=2)
    nbt = nb.reshape(KV, G, TQ, 2 * TQ).transpose(0, 3, 1, 2).reshape(KV, 2 * TQ, R)
    vmem = (2 * (2 * TQ * G * HEAD_DIM * 2 + 2 * Lp * HEAD_DIM * 2 + ntile * TQ * LANES * 2
                 + 2 * TQ * R * 4)
            + R * LANES * 6 + 2 * R * TK * 4 + 5 * R * TK * 4)
    q_map = lambda h, c: (c, h)
    return pl.pallas_call(
        _attn_prompt_kernel,
        grid=(KV, steps),
        in_specs=[pl.BlockSpec((TQ, G * HEAD_DIM), q_map),
                  pl.BlockSpec((Lp, HEAD_DIM), lambda h, c: (0, h)),
                  pl.BlockSpec((None, ntile, VR, LANES), lambda h, c: (h, 0, 0, 0)),
                  pl.BlockSpec((1, ntile, LANES, TQ), lambda h, c: (c, 0, 0, 0)),
                  pl.BlockSpec((None, 2 * TQ, R), lambda h, c: (h, 0, 0))],
        out_specs=pl.BlockSpec((TQ, G * HEAD_DIM), q_map),
        out_shape=jax.ShapeDtypeStruct((T, HD), BF16),
        scratch_shapes=[pltpu.VMEM((R, HEAD_DIM), BF16),
                        pltpu.VMEM((1, R), F32),
                        pltpu.VMEM((VR, R), F32),
                        pltpu.VMEM((TK, R), F32),
                        pltpu.VMEM((TK, R), F32),
                        pltpu.VMEM((TK, R), BF16)],
        compiler_params=_params(("parallel", "arbitrary"), vmem),
        name="dsa_attention_prompt",
    )(q, k, vt, mask, nbt)


def _t5_bucket_np(rel):
    nb = N_BUCKETS // 2
    max_exact = nb // 2
    side = np.where(rel > 0, nb, 0)
    n = np.abs(rel)
    nf = np.maximum(n, 1).astype(np.float32)
    large = max_exact + (np.log(nf / np.float32(max_exact))
                         / np.float32(math.log(REL_MAX_DIST / max_exact))
                         * np.float32(nb - max_exact)).astype(np.int32)
    large = np.minimum(large, nb - 1)
    return side + np.where(n < max_exact, n, large)


def _near_bias(rel_bias, rel, far_rel):
    far = int(_t5_bucket_np(np.array(far_rel)))
    assert far_rel < 0 and far == int(_t5_bucket_np(np.array(-10 ** 6)))
    rb = rel_bias.astype(F32) * LOG2E
    tab = rb[_t5_bucket_np(rel)] - rb[far][None, None, None, :]
    return tab.transpose(0, 3, 1, 2)


def _near_bias_stream(rel_bias, TQ, TK, phases):
    t = np.arange(TQ)[:, None]
    j = np.arange(2 * TK)[None, :]
    rel = np.stack([j - TK - ph + TQ - t for ph in phases])
    return _near_bias(rel_bias, rel, TQ - TK - 2)


def _near_bias_prompt(rel_bias):
    TQ = Q_STEP
    rel = np.arange(2 * TQ)[None, :] - TQ - np.arange(TQ)[:, None]
    return _near_bias(rel_bias, rel[None], -TQ - 1)[0]


def _conv_kernel(x_ref, halo_ref, buf_ref, w_ref, b_ref, xc_ref, xcb_ref, ext_ref, *, tt):
    first = pl.program_id(1) == 0
    ext_ref[0:SUBLANES, :] = jnp.where(first, buf_ref[...], halo_ref[...])
    ext_ref[SUBLANES:, :] = x_ref[...]
    y = jnp.broadcast_to(b_ref[...], x_ref.shape)
    for j in range(CONV_W):
        off = SUBLANES - (CONV_W - 1) + j
        y = y + ext_ref[off:off + tt, :] * w_ref[j:j + 1, :]
    xc_ref[...] = y
    xcb_ref[...] = y.astype(BF16)


def _conv(x, buf8, w, b, *, row0, B, T, tt):
    C = x.shape[1]
    assert row0 % tt == 0 and T % tt == 0 and tt % SUBLANES == 0
    nt = T // tt
    hb = tt // SUBLANES
    blk0 = row0 // tt
    main_map = lambda b, i: (blk0 + b * nt + i, 0)
    halo_map = lambda b, i: (jnp.maximum((blk0 + b * nt + i) * hb - 1, 0), 0)
    out_map = lambda b, i: (b * nt + i, 0)
    return pl.pallas_call(
        functools.partial(_conv_kernel, tt=tt),
        grid=(B, nt),
        in_specs=[pl.BlockSpec((tt, C), main_map),
                  pl.BlockSpec((SUBLANES, C), halo_map),
                  pl.BlockSpec((None, SUBLANES, C), lambda b, i: (b, 0, 0)),
                  pl.BlockSpec((CONV_W, C), lambda b, i: (0, 0)),
                  pl.BlockSpec((1, C), lambda b, i: (0, 0))],
        out_specs=[pl.BlockSpec((tt, C), out_map), pl.BlockSpec((tt, C), out_map)],
        out_shape=[jax.ShapeDtypeStruct((B * T, C), F32),
                   jax.ShapeDtypeStruct((B * T, C), BF16)],
        scratch_shapes=[pltpu.VMEM((tt + SUBLANES, C), F32)],
        compiler_params=_params(("parallel", "parallel"), 2 * tt * C * 10 + tt * C * 12),
        name="causal_conv",
    )(x, x, buf8, w, b.reshape(1, C))


def _scan_kernel(a_ref, u_ref, g_ref, h0_ref, y_ref, hlast_ref, h_ref, hs_ref, *, tt):
    i = pl.program_id(1)

    @pl.when(i == 0)
    def _():
        h_ref[...] = h0_ref[...]

    def step(t, h):
        h = a_ref[pl.ds(t, 1), :] * h + u_ref[pl.ds(t, 1), :]
        hs_ref[pl.ds(t, 1), :] = h
        return h

    h = lax.fori_loop(0, tt, step, h_ref[...])
    h_ref[...] = h
    hlast_ref[...] = h
    y_ref[...] = (hs_ref[...] * jax.nn.gelu(g_ref[...])).astype(y_ref.dtype)


def _scan(a, u, g, h0, *, row0, B, T, tt):
    C = a.shape[1]
    assert row0 % tt == 0 and T % tt == 0
    nt = T // tt
    blk0 = row0 // tt
    blk = pl.BlockSpec((tt, C), lambda b, i: (b * nt + i, 0))
    g_blk = pl.BlockSpec((tt, C), lambda b, i: (blk0 + b * nt + i, 0))
    vec = pl.BlockSpec((None, 1, C), lambda b, i: (b, 0, 0))
    return pl.pallas_call(
        functools.partial(_scan_kernel, tt=tt),
        grid=(B, nt),
        in_specs=[blk, blk, g_blk, vec],
        out_specs=[blk, vec],
        out_shape=[jax.ShapeDtypeStruct((B * T, C), BF16),
                   jax.ShapeDtypeStruct((B, 1, C), F32)],
        scratch_shapes=[pltpu.VMEM((1, C), F32), pltpu.VMEM((tt, C), F32)],
        compiler_params=_params(("parallel", "arbitrary"), 2 * tt * C * 14 + tt * C * 12),
        name="rglru_scan",
    )(a, u, g, h0)


GATE_TN = 256


def _gate_window(rb, C):
    raw = [((j * GATE_TN) // rb * rb) // LANES * LANES for j in range(C // GATE_TN)]
    ends = [((j * GATE_TN + GATE_TN - 1) // rb + 1) * rb for j in range(C // GATE_TN)]
    kw = -(-max(e - s for s, e in zip(raw, ends)) // LANES) * LANES
    kw = min(kw, C)
    starts = [min(s, C - kw) for s in raw]
    assert all(s + kw >= e for s, e in zip(starts, ends))
    return starts, kw


def _blockdiag_tiles(w):
    nblk, rb, _ = w.shape
    C = nblk * rb
    starts, kw = _gate_window(rb, C)
    dense = jax.scipy.linalg.block_diag(*[w[i] for i in range(nblk)])
    tiles = [dense[s:s + kw, j * GATE_TN:(j + 1) * GATE_TN] for j, s in enumerate(starts)]
    return jnp.stack(tiles).astype(BF16)


def _gates_kernel(x_ref, wa_ref, wx_ref, xc_ref, ba_ref, bx_ref, lam_ref, a_ref, u_ref, *, rb, kw):
    C = x_ref.shape[1]
    j = pl.program_id(1)
    start = jnp.minimum(((j * GATE_TN) // rb * rb) // LANES * LANES, C - kw)
    x = x_ref[:, pl.ds(pl.multiple_of(start, LANES), kw)]
    accs = [jnp.dot(x, w[0], preferred_element_type=F32) for w in (wa_ref, wx_ref)]
    a, u = _ep_rglru_gates(accs, [xc_ref[...]], [ba_ref[...], bx_ref[...], lam_ref[...]])
    a_ref[...] = a
    u_ref[...] = u


def _rglru_gates(xcb, xc, wa_t, wx_t, ba, bx, lam, *, rb, tm):
    rows, C = xc.shape
    nt, kw, _ = wa_t.shape
    tile = lambda: pl.BlockSpec((tm, GATE_TN), lambda i, j: (i, j))
    vec = lambda: pl.BlockSpec((1, GATE_TN), lambda i, j: (0, j))
    wspec = lambda: pl.BlockSpec((1, kw, GATE_TN), lambda i, j: (j, 0, 0))
    vmem = 2 * (tm * C * 2 + 2 * kw * GATE_TN * 2 + 3 * tm * GATE_TN * 4) + 8 * tm * GATE_TN * 4
    return pl.pallas_call(
        functools.partial(_gates_kernel, rb=rb, kw=kw),
        grid=(rows // tm, nt),
        in_specs=[pl.BlockSpec((tm, C), lambda i, j: (i, 0)), wspec(), wspec(), tile(),
                  vec(), vec(), vec()],
        out_specs=[tile(), tile()],
        out_shape=[jax.ShapeDtypeStruct((rows, C), F32)] * 2,
        compiler_params=_params(("parallel", "arbitrary"), vmem),
        name="rglru_gates",
    )(xcb, wa_t, wx_t, xc, ba.reshape(1, C), bx.reshape(1, C), lam.reshape(1, C))


def _pick(n, cands):
    for c in cands:
        if n % c == 0:
            return c
    raise ValueError(f"no tile for {n}")


def _layer(x, hist, p, rel_bias, dims):
    Tp, Bs, Ts, past = dims
    M, D = x.shape
    Ms = Bs * Ts
    k_past, v_past, ki_past, conv_buf, h0 = hist
    KV = k_past.shape[2]
    HQ = KV * GROUP * HEAD_DIM
    KVD = KV * HEAD_DIM
    DI = ki_past.shape[-1]
    C = p["conv_w"].shape[-1]
    n_in = p["w_in"].shape[1]
    HI = (n_in - HQ - 2 * KVD - DI - 2 * C - 2 * D) // (DI + 1)
    assert DI == LANES
    sizes = (HQ, KVD, KVD, HI * DI, DI, HI, C, C, D, D)
    offs = np.concatenate([[0], np.cumsum(sizes)])
    assert offs[-1] == n_in
    w_in = p["w_in"]

    def wslice(i, j=None):
        j = i if j is None else j
        return w_in[:, offs[i]:offs[j + 1]].astype(BF16)

    tm = _pick(M, (768, 512, 256, 128, 64))
    TK = KEY_TILE

    h = _rmsnorm(x, p["norm_mix"], tm)

    def proj(w, ep, dt, tn, vecs=(), name="in_proj"):
        return _matmul(h, [w], ep, [dt], tm=tm, tn=tn, vecs=vecs, name=name)[0]

    wide = (1024, 512, 256, 128)
    q = proj(wslice(0), _ep_headnorm(HEAD_DIM ** -0.5 * LOG2E), BF16, _pick(HQ, wide),
             vecs=[jnp.tile(p["q_norm"], HQ // HEAD_DIM)], name="in_proj_q")
    k = proj(wslice(1), _ep_headnorm(1.0), F32, _pick(KVD, wide),
             vecs=[jnp.tile(p["k_norm"], KV)], name="in_proj_k")
    v = proj(wslice(2), _ep_identity, F32, _pick(KVD, wide), name="in_proj_v")
    qi = proj(wslice(3), _ep_identity, BF16, _pick(HI * DI, wide), name="in_proj_qi")
    kw_w = jnp.pad(wslice(4, 5), ((0, 0), (0, 2 * LANES - DI - HI)))
    kiwi = proj(kw_w, _ep_identity, F32, 2 * LANES, name="in_proj_ki")
    ki, wi = kiwi[:, :DI], kiwi[:, DI:DI + HI]
    tn_c = _pick(C, (768, 384, 128))
    xr = proj(wslice(6), _ep_identity, F32, tn_c, name="in_proj_xr")
    gr = proj(wslice(7), _ep_identity, F32, tn_c, name="in_proj_gr")
    tn_d = _pick(D, (512, 256, 128))
    tn_w = _pick(D, wide)
    ga = proj(wslice(8), _ep_identity, F32, tn_w, name="in_proj_ga")
    gb = proj(wslice(9), _ep_identity, F32, tn_w, name="in_proj_gb")

    k_bf, v_bf, ki_bf = k.astype(BF16), v.astype(BF16), ki.astype(BF16)

    assert Tp % Q_STEP == 0
    back_p = -Tp % FAR_TILE
    n_sel_p = min(TOPK_MAX, Tp // 4)
    mask_p = _select_prompt(qi, wi, jnp.pad(ki_bf[:Tp], ((0, back_p), (0, 0))), n_sel_p, Tp)
    kv_pad = lambda a: jnp.pad(a[:Tp], ((FRONT_PAD, back_p), (0, 0)))
    o_p = _attention_prompt(q, kv_pad(k_bf), kv_pad(v_bf), mask_p, _near_bias_prompt(rel_bias))

    Ls = past + Ts
    nt_s = -(-Ls // TK)
    pad_s = nt_s * TK - Ls

    def with_cache(cache, new):
        new = new[Tp:].reshape(Bs, Ts, -1)
        parts = [cache.reshape(Bs, past, -1).astype(BF16), new]
        if pad_s:
            parts.append(jnp.zeros((Bs, pad_s, new.shape[-1]), BF16))
        return jnp.concatenate(parts, axis=1)

    n_sel_s = min(TOPK_MAX, Ls // 4)
    mask_s = _select(qi[Tp:], wi[Tp:], with_cache(ki_past, ki_bf), TQ=Ts, NT=nt_s,
                     lend_a=0, lend_b=Ls, n_sel=n_sel_s, batched=True)
    nb_s = _near_bias_stream(rel_bias, Ts, TK, [Ls - ((Ls - 1) // TK) * TK])
    o_s = _attention(q[Tp:], with_cache(k_past, k_bf), with_cache(v_past, v_bf), mask_s, nb_s,
                     TQ=Ts, lend_a=0, lend_b=Ls, batched=True)
    o_a = jnp.concatenate([o_p, o_s], axis=0)

    assert C % GATE_TN == 0
    wa_t = _blockdiag_tiles(p["rg_wa"])
    wx_t = _blockdiag_tiles(p["rg_wx"])

    def griffin(row0, B, T, buf, h_init):
        tt = _pick(math.gcd(T, row0) if row0 else T, (128, 64, 32, 16, 8))
        buf8 = jnp.pad(buf.astype(F32), ((0, 0), (SUBLANES - (CONV_W - 1), 0), (0, 0)))
        xc, xcb = _conv(xr, buf8, p["conv_w"], p["conv_b"], row0=row0, B=B, T=T, tt=tt)
        rows = B * T
        tmr = _pick(rows, (1024, 512, 256, 128, 64, 32))
        a, u = _rglru_gates(xcb, xc, wa_t, wx_t, p["rg_ba"], p["rg_bx"], p["rg_lambda"],
                            rb=p["rg_wa"].shape[1], tm=tmr)
        y, h_last = _scan(a, u, gr, h_init.reshape(B, 1, C), row0=row0, B=B, T=T, tt=tt)
        tail = xr[row0:row0 + rows].reshape(B, T, C)[:, -(CONV_W - 1):]
        conv_new = jnp.concatenate([buf.astype(F32), tail], axis=1)[:, -(CONV_W - 1):]
        return y, conv_new, h_last.reshape(B, C)

    y_p, conv_p, h_p = griffin(0, 1, Tp, jnp.zeros((1, CONV_W - 1, C), F32), jnp.zeros((1, C), F32))
    y_s, conv_s, h_s = griffin(Tp, Bs, Ts, conv_buf, h0)
    o_b = jnp.concatenate([y_p, y_s], axis=0)

    part = _matmul(o_a, [p["w_out_attn"].astype(BF16)], _ep_gate, [F32], tm=tm, tn=tn_w,
                   exts=[ga], name="out_attn")[0]
    merged = _matmul(o_b, [p["w_out_rg"].astype(BF16)], _ep_gate_add, [BF16],
                     tm=_pick(M, (512, 256, 128, 64)), tn=tn_w, exts=[gb, part], name="out_rg")[0]
    x1 = _matmul(merged, [p["w_o"].astype(BF16)], _ep_residual, [F32], tm=tm, tn=tn_w,
                 exts=[x], name="w_o")[0]
    hf = _rmsnorm(x1, p["norm_ffn"], tm)
    FF = p["ffn_w1"].shape[1]
    tn_f = _pick(FF, (512, 256, 128))
    act = _matmul(hf, [p["ffn_w1"].astype(BF16), p["ffn_w3"].astype(BF16)], _ep_swiglu, [BF16],
                  tm=tm, tn=tn_f, name="ffn_up")[0]
    tk_f = _pick(FF, (5504, 2816, 2048, 1024, 512, 256, 128))
    x2 = _matmul(act, [p["ffn_w2"].astype(BF16)], _ep_residual, [F32],
                 tm=tm, tn=tn_d, tk=tk_f, exts=[x1],
                 name="ffn_down")[0]

    new_p = (k[:Tp].reshape(1, Tp, KV, HEAD_DIM), v[:Tp].reshape(1, Tp, KV, HEAD_DIM),
             ki[:Tp].reshape(1, Tp, DI), conv_p, h_p)
    new_s = (k[Tp:].reshape(Bs, Ts, KV, HEAD_DIM), v[Tp:].reshape(Bs, Ts, KV, HEAD_DIM),
             ki[Tp:].reshape(Bs, Ts, DI), conv_s, h_s)
    return x2, new_p, new_s


def kernel(x_prompt, x_sample, cache_k, cache_v, cache_kidx, state_conv, state_rglru, norm_mix, w_in, q_norm, k_norm, rel_bias, conv_w, conv_b, rg_wa, rg_ba, rg_wx, rg_bx, rg_lambda, w_out_attn, w_out_rg, w_o, norm_ffn, ffn_w1, ffn_w3, ffn_w2):
    Bp, Tp, D = x_prompt.shape
    Bs, Ts, _ = x_sample.shape
    assert Bp == 1 and Tp % CHUNK == 0
    depth = w_in.shape[0]
    past = cache_k.shape[2]
    x = jnp.concatenate([x_prompt.reshape(Tp, D), x_sample.reshape(Bs * Ts, D)], axis=0)
    outs_p, outs_s = [], []
    for l in range(depth):
        p = dict(norm_mix=norm_mix[l], w_in=w_in[l], q_norm=q_norm[l], k_norm=k_norm[l],
                 conv_w=conv_w[l], conv_b=conv_b[l], rg_wa=rg_wa[l], rg_ba=rg_ba[l],
                 rg_wx=rg_wx[l], rg_bx=rg_bx[l], rg_lambda=rg_lambda[l],
                 w_out_attn=w_out_attn[l], w_out_rg=w_out_rg[l], w_o=w_o[l],
                 norm_ffn=norm_ffn[l], ffn_w1=ffn_w1[l], ffn_w3=ffn_w3[l], ffn_w2=ffn_w2[l])
        hist = (cache_k[l], cache_v[l], cache_kidx[l], state_conv[l], state_rglru[l])
        x, new_p, new_s = _layer(x, hist, p, rel_bias, (Tp, Bs, Ts, past))
        outs_p.append(new_p)
        outs_s.append(new_s)
    stack = lambda outs, i: jnp.stack([o[i] for o in outs])
    return (x[:Tp].reshape(1, Tp, D), x[Tp:].reshape(Bs, Ts, D),
            *[stack(outs_p, i) for i in range(5)],
            *[stack(outs_s, i) for i in range(5)])
```

```python
import functools
import math

import numpy as np
import jax
import jax.numpy as jnp
from jax import lax
from jax.experimental import pallas as pl
from jax.experimental.pallas import tpu as pltpu

F32 = jnp.float32
BF16 = jnp.bfloat16

CHUNK = 64
HEAD_DIM = 128
GROUP = 4
TOPK_MAX = 256
N_BUCKETS = 32
REL_MAX_DIST = 128
RG_C = 8.0
CONV_W = 4
EPS = 1e-6

LANES = 128
SUBLANES = 8
KEY_TILE = 256
Q_STEP = 2 * CHUNK
FAR_TILE = 512
FRONT_PAD = FAR_TILE
ONES_ROWS = 16
LOG2E = math.log2(math.e)
VMEM_CAP = 56 << 20
NEG = -1e30


def _params(sem, vmem_bytes):
    limit = min(max(int(vmem_bytes) + (6 << 20), 24 << 20), VMEM_CAP)
    return pltpu.CompilerParams(dimension_semantics=sem, vmem_limit_bytes=limit)


def _rmsnorm_kernel(x_ref, g_ref, o_ref):
    x = x_ref[...]
    ms = jnp.mean(x * x, axis=-1, keepdims=True)
    o_ref[...] = (x * lax.rsqrt(ms + EPS) * g_ref[...]).astype(o_ref.dtype)


def _rmsnorm(x, g, tm):
    M, D = x.shape
    tm = min(tm, 256)
    return pl.pallas_call(
        _rmsnorm_kernel,
        grid=(M // tm,),
        in_specs=[pl.BlockSpec((tm, D), lambda i: (i, 0)),
                  pl.BlockSpec((1, D), lambda i: (0, 0))],
        out_specs=pl.BlockSpec((tm, D), lambda i: (i, 0)),
        out_shape=jax.ShapeDtypeStruct((M, D), BF16),
        compiler_params=_params(("parallel",), 2 * tm * D * 6 + 2 * tm * D * 4),
        name="rmsnorm",
    )(x, g.reshape(1, D))


def _mm_kernel(*refs, n_w, n_ext, n_vec, n_out, nk, epilogue):
    x_ref = refs[0]
    w_refs = refs[1:1 + n_w]
    p = 1 + n_w
    ext_refs = refs[p:p + n_ext]
    p += n_ext
    vec_refs = refs[p:p + n_vec]
    p += n_vec
    out_refs = refs[p:p + n_out]
    acc_refs = refs[p + n_out:]

    def finish(accs):
        res = epilogue(accs, [e[...] for e in ext_refs], [v[...] for v in vec_refs])
        for o, r in zip(out_refs, res):
            o[...] = r.astype(o.dtype)

    if nk == 1:
        finish([jnp.dot(x_ref[...], w[...], preferred_element_type=F32) for w in w_refs])
        return

    k = pl.program_id(2)

    @pl.when(k == 0)
    def _():
        for a in acc_refs:
            a[...] = jnp.zeros_like(a)

    for a, w in zip(acc_refs, w_refs):
        a[...] += jnp.dot(x_ref[...], w[...], preferred_element_type=F32)

    @pl.when(k == nk - 1)
    def _():
        finish([a[...] for a in acc_refs])


def _matmul(x, ws, epilogue, out_dtypes, *, tm, tn, tk=None, exts=(), vecs=(), name="matmul"):
    M, K = x.shape
    n = ws[0].shape[1]
    tk = K if tk is None else tk
    nk = K // tk
    assert M % tm == 0 and n % tn == 0 and K % tk == 0
    x_map = lambda j, i, k: (i, k)
    w_map = lambda j, i, k: (k, j)
    mn_map = lambda j, i, k: (i, j)
    in_specs = [pl.BlockSpec((tm, tk), x_map)]
    in_specs += [pl.BlockSpec((tk, tn), w_map) for _ in ws]
    in_specs += [pl.BlockSpec((tm, tn), mn_map) for _ in exts]
    in_specs += [pl.BlockSpec((1, tn), lambda j, i, k: (0, j)) for _ in vecs]
    out_specs = [pl.BlockSpec((tm, tn), mn_map) for _ in out_dtypes]
    out_shape = [jax.ShapeDtypeStruct((M, n), dt) for dt in out_dtypes]
    scratch = [pltpu.VMEM((tm, tn), F32) for _ in ws] if nk > 1 else []
    vmem = (2 * (tm * tk * 2 + len(ws) * tk * tn * 2 + (len(exts) + len(out_dtypes)) * tm * tn * 4)
            + 3 * len(ws) * tm * tn * 4)
    kern = functools.partial(_mm_kernel, n_w=len(ws), n_ext=len(exts), n_vec=len(vecs),
                             n_out=len(out_dtypes), nk=nk, epilogue=epilogue)
    outs = pl.pallas_call(
        kern,
        grid=(n // tn, M // tm, nk),
        in_specs=in_specs,
        out_specs=out_specs,
        out_shape=out_shape,
        scratch_shapes=scratch,
        compiler_params=_params(("parallel", "parallel", "arbitrary"), vmem),
        name=name,
    )(x, *ws, *exts, *[v.reshape(1, -1) for v in vecs])
    return outs


def _ep_identity(accs, exts, vecs):
    return (accs[0],)


def _ep_headnorm(scale):
    def ep(accs, exts, vecs):
        a, g = accs[0], vecs[0]
        outs = []
        for j in range(a.shape[1] // HEAD_DIM):
            aj = a[:, j * HEAD_DIM:(j + 1) * HEAD_DIM]
            ms = jnp.mean(aj * aj, axis=-1, keepdims=True)
            outs.append(aj * lax.rsqrt(ms + EPS) * g[:, j * HEAD_DIM:(j + 1) * HEAD_DIM])
        y = jnp.concatenate(outs, axis=1)
        return (y * scale if scale != 1.0 else y,)
    return ep


def _ep_gate(accs, exts, vecs):
    return (jax.nn.sigmoid(exts[0]) * accs[0],)


def _ep_gate_add(accs, exts, vecs):
    return (exts[1] + jax.nn.sigmoid(exts[0]) * accs[0],)


def _ep_residual(accs, exts, vecs):
    return (exts[0] + accs[0],)


def _ep_swiglu(accs, exts, vecs):
    return (jax.nn.silu(accs[0]) * accs[1],)


def _softplus(x):
    return jnp.maximum(x, 0.0) + jnp.log1p(jnp.exp(-jnp.abs(x)))


def _ep_rglru_gates(accs, exts, vecs):
    xc = exts[0]
    r = jax.nn.sigmoid(accs[0] + vecs[0])
    i = jax.nn.sigmoid(accs[1] + vecs[1])
    log_a = -RG_C * r * _softplus(-vecs[2])
    a = jnp.exp(log_a)
    u = jnp.sqrt(1.0 - a * a) * (i * xc)
    return (a, u)


INT_MAX = 2 ** 31 - 1


def _open_rows(lo, hi):
    return jnp.max(jnp.where(lo < hi, 1.0, 0.0))


def _to_key(s):
    b = pltpu.bitcast(s, jnp.int32)
    return b ^ ((b >> 31) & jnp.int32(0x7FFFFFFF))


def _from_key(k):
    return pltpu.bitcast(k ^ ((k >> 31) & jnp.int32(0x7FFFFFFF)), F32)


def _kth_largest_key(count_ge, lo, hi, ksel, n_valid):
    def body(st):
        lo, hi, c_lo, _, it = st
        active = lo < hi
        mid_k = (lo >> 1) + (hi >> 1) + ((lo | hi) & 1)
        mid_f = _to_key(0.5 * _from_key(lo) + 0.5 * _from_key(hi))
        mid_f = jnp.minimum(jnp.maximum(mid_f, lo + 1), hi)
        mid = jnp.where((it & 1) == 0, mid_f, mid_k)
        c = count_ge(mid)
        ge = c >= ksel
        lo_n = jnp.where(ge, mid, lo)
        c_n = jnp.where(ge, c, c_lo)
        hi_n = jnp.where(c == ksel, mid, jnp.where(ge, hi, mid - 1))
        lo = jnp.where(active, lo_n, lo)
        c_lo = jnp.where(active, c_n, c_lo)
        hi = jnp.where(active, hi_n, hi)
        return lo, hi, c_lo, _open_rows(lo, hi), it + 1

    tau, _, c_tau, _, _ = lax.while_loop(lambda st: st[3] > 0.5, body,
                                         (lo, hi, n_valid, _open_rows(lo, hi), jnp.int32(0)))
    return tau, c_tau


def _tie_cut(count_tie_le, need, n_keys):
    def body(st):
        lo, hi, _ = st
        active = lo < hi
        mid = (lo + hi) >> 1
        ok = count_tie_le(mid) >= need
        hi = jnp.where(active, jnp.where(ok, mid, hi), hi)
        lo = jnp.where(active, jnp.where(ok, lo, mid + 1), lo)
        return lo, hi, _open_rows(lo, hi)

    lo = jnp.zeros(need.shape, jnp.int32)
    hi = jnp.full(need.shape, n_keys - 1, jnp.int32)
    cut, _, _ = lax.while_loop(lambda st: st[2] > 0.5, body, (lo, hi, _open_rows(lo, hi)))
    return cut


def _threshold_and_cut(key_ref, cut_ref, nt, lo, hi, ksel, n_valid):
    _, TQ, TK = key_ref.shape
    nl = TK // LANES
    lane = lax.broadcasted_iota(jnp.int32, (TQ, LANES), 1)

    def count(indicator):
        def body(j, c):
            kk = key_ref[j]
            for l in range(nl):
                c = c + indicator(kk[:, l * LANES:(l + 1) * LANES], j * TK + l * LANES)
            return c
        c = lax.fori_loop(0, nt, body, jnp.zeros((TQ, LANES), F32))
        return jnp.broadcast_to(jnp.sum(c, axis=1, keepdims=True), (TQ, LANES))

    tau, c_tau = _kth_largest_key(
        lambda mid: count(lambda kk, base: jnp.where(kk >= mid, 1.0, 0.0)), lo, hi, ksel, n_valid)
    cut_ref[...] = jnp.full((TQ, LANES), INT_MAX, jnp.int32)

    tied = jnp.max(jnp.where(c_tau > ksel, 1.0, 0.0)) > 0.5

    @pl.when(tied)
    def _():
        need = ksel - count(lambda kk, base: jnp.where(kk > tau, 1.0, 0.0))
        cut_ref[...] = _tie_cut(
            lambda mid: count(lambda kk, base: jnp.where(
                kk == tau, jnp.where(lane + base <= mid, 1.0, 0.0), 0.0)),
            need, nt * TK)

    return tau, cut_ref[...], tied


def _selected(kk, base, tau, cut):
    lane = lax.broadcasted_iota(jnp.int32, kk.shape, 1)
    tie = jnp.where(kk == tau, jnp.where(lane + base <= cut, 1, 0), 0)
    return jnp.where(kk > tau, 1, tie)


def _write_selection(write_tile, nt, tied):
    @pl.when(tied)
    def _():
        lax.fori_loop(0, nt, lambda j, _: write_tile(j, True) or 0, 0)

    @pl.when(jnp.logical_not(tied))
    def _():
        lax.fori_loop(0, nt, lambda j, _: write_tile(j, False) or 0, 0)
def _select_kernel(qi_ref, wi_ref, ki_ref, mask_ref, q2_ref, w2_ref, key_ref, cut_ref, *,
                   TQ, TK, NT, HI, lend_a, lend_b, n_sel, wscale):
    nl = TK // LANES
    lend = lend_a * pl.program_id(0) + lend_b
    nt = (lend + TK - 1) // TK
    ksel = jnp.minimum(n_sel, lend).astype(F32)

    wi = wi_ref[...] * wscale
    for h in range(HI):
        q2_ref[h * TQ:(h + 1) * TQ, :] = qi_ref[:, h * LANES:(h + 1) * LANES]
        w2_ref[h * TQ:(h + 1) * TQ, :] = jnp.broadcast_to(wi[:, h:h + 1], (TQ, LANES))

    lane = lax.broadcasted_iota(jnp.int32, (TQ, TK), 1)
    int_min = jnp.int32(-2 ** 31)

    def score_tile(j, carry):
        smin, smax = carry
        start = pl.multiple_of(j * TK, TK)
        kt = ki_ref[pl.ds(start, TK), :]
        s = lax.dot_general(q2_ref[...], kt, (((1,), (1,)), ((), ())),
                            preferred_element_type=F32)
        cols = []
        for l in range(nl):
            acc = jnp.zeros((TQ, LANES), F32)
            for h in range(HI):
                acc = acc + (jnp.maximum(s[h * TQ:(h + 1) * TQ, l * LANES:(l + 1) * LANES], 0.0)
                             * w2_ref[h * TQ:(h + 1) * TQ, :])
            cols.append(acc)
        sc = jnp.concatenate(cols, axis=1)
        valid = (lane + j * TK) < lend
        key_ref[j] = jnp.where(valid, _to_key(sc), int_min)
        lo_s = jnp.where(valid, sc, jnp.inf)
        hi_s = jnp.where(valid, sc, -jnp.inf)
        for l in range(nl):
            smin = jnp.minimum(smin, lo_s[:, l * LANES:(l + 1) * LANES])
            smax = jnp.maximum(smax, hi_s[:, l * LANES:(l + 1) * LANES])
        return smin, smax

    smin, smax = lax.fori_loop(
        0, nt, score_tile,
        (jnp.full((TQ, LANES), jnp.inf, F32), jnp.full((TQ, LANES), -jnp.inf, F32)))
    lo = _to_key(jnp.broadcast_to(jnp.min(smin, axis=1, keepdims=True), (TQ, LANES)))
    hi = _to_key(jnp.broadcast_to(jnp.max(smax, axis=1, keepdims=True), (TQ, LANES)))

    n_valid = jnp.full((TQ, LANES), lend, jnp.int32).astype(F32)
    tau, cut, tied = _threshold_and_cut(key_ref, cut_ref, nt, lo, hi,
                                        jnp.full((TQ, LANES), ksel, F32), n_valid)

    def write(j, exact_ties):
        kk = key_ref[j]
        sel = []
        for l in range(nl):
            slab = kk[:, l * LANES:(l + 1) * LANES]
            sel.append(_selected(slab, j * TK + l * LANES, tau, cut) if exact_ties
                       else jnp.where(slab >= tau, 1, 0))
        mask_ref[0, j] = jnp.concatenate(sel, axis=1).astype(jnp.int8)

    _write_selection(write, nt, tied)

    def clear(j, _):
        mask_ref[0, j] = jnp.zeros((TQ, TK), jnp.int8)
        return 0

    lax.fori_loop(nt, NT, clear, 0)


def _select(qi, wi, ki, *, TQ, NT, lend_a, lend_b, n_sel, batched):
    TK = KEY_TILE
    Mq = qi.shape[0]
    HI = wi.shape[1]
    steps = Mq // TQ
    if batched:
        ki_spec = pl.BlockSpec((None, NT * TK, LANES), lambda i: (i, 0, 0))
    else:
        ki_spec = pl.BlockSpec((NT * TK, LANES), lambda i: (0, 0))
    kern = functools.partial(_select_kernel, TQ=TQ, TK=TK, NT=NT, HI=HI, lend_a=lend_a,
                             lend_b=lend_b, n_sel=n_sel,
                             wscale=float(HI ** -0.5 * LANES ** -0.5))
    vmem = (2 * (TQ * HI * LANES * 2 + NT * TK * LANES * 2 + NT * TQ * TK)
            + HI * TQ * LANES * 6 + NT * TQ * TK * 4 + 3 * HI * TQ * TK * 4)
    return pl.pallas_call(
        kern,
        grid=(steps,),
        in_specs=[pl.BlockSpec((TQ, HI * LANES), lambda i: (i, 0)),
                  pl.BlockSpec((TQ, HI), lambda i: (i, 0)),
                  ki_spec],
        out_specs=pl.BlockSpec((1, NT, TQ, TK), lambda i: (i, 0, 0, 0)),
        out_shape=jax.ShapeDtypeStruct((steps, NT, TQ, TK), jnp.int8),
        scratch_shapes=[pltpu.VMEM((HI * TQ, LANES), BF16),
                        pltpu.VMEM((HI * TQ, LANES), F32),
                        pltpu.VMEM((NT, TQ, TK), jnp.int32),
                        pltpu.VMEM((TQ, LANES), jnp.int32)],
        compiler_params=_params(("parallel",), vmem),
        name="dsa_select",
    )(qi, wi, ki)


def _select_prompt_kernel(qi_ref, wi_ref, ki_ref, mask_ref, q2_ref, w2_ref, key_ref, acc_ref,
                          cut_ref, *,
                          HI, HG, NT, n_sel, wscale):
    TQ, TK = Q_STEP, FAR_TILE
    nl = TK // LANES
    pad_tiles = FRONT_PAD // LANES
    RB = 64
    c2 = pl.program_id(0)
    row = lax.broadcasted_iota(jnp.int32, (TQ, LANES), 0)
    lend = jnp.where(row < CHUNK, c2 * TQ + CHUNK, c2 * TQ + TQ)
    nt = (c2 * TQ + TQ + TK - 1) // TK
    ksel = jnp.minimum(n_sel, lend).astype(F32)

    wi = wi_ref[...] * wscale
    for h in range(HI):
        q2_ref[h * TQ:(h + 1) * TQ, :] = qi_ref[:, h * LANES:(h + 1) * LANES]
        w2_ref[h * TQ:(h + 1) * TQ, :] = jnp.broadcast_to(wi[:, h:h + 1], (TQ, LANES))

    lane = lax.broadcasted_iota(jnp.int32, (TQ, TK), 1)
    lend_t = jnp.concatenate([lend] * nl, axis=1)
    int_min = jnp.int32(-2 ** 31)

    def score_tile(j, carry):
        smin, smax = carry
        start = pl.multiple_of(j * TK, TK)
        kt = ki_ref[pl.ds(start, TK), :]
        for hg in range(HI // HG):
            s = lax.dot_general(q2_ref[hg * HG * TQ:(hg + 1) * HG * TQ, :], kt,
                                (((1,), (1,)), ((), ())), preferred_element_type=F32)
            for r in range(TQ // RB):
                cs = [None] * nl
                for h in range(HG):
                    r0 = h * TQ + r * RB
                    w = w2_ref[(hg * HG) * TQ + r0:(hg * HG) * TQ + r0 + RB, :]
                    for l in range(nl):
                        term = jnp.maximum(s[r0:r0 + RB, l * LANES:(l + 1) * LANES], 0.0) * w
                        cs[l] = term if cs[l] is None else cs[l] + term
                for l in range(nl):
                    if hg == 0:
                        acc_ref[r * RB:(r + 1) * RB, l * LANES:(l + 1) * LANES] = cs[l]
                    else:
                        acc_ref[r * RB:(r + 1) * RB, l * LANES:(l + 1) * LANES] += cs[l]
        sc = acc_ref[...]
        valid = (lane + j * TK) < lend_t
        key_ref[j] = jnp.where(valid, _to_key(sc), int_min)
        lo_s = jnp.where(valid, sc, jnp.inf)
        hi_s = jnp.where(valid, sc, -jnp.inf)
        for l in range(nl):
            smin = jnp.minimum(smin, lo_s[:, l * LANES:(l + 1) * LANES])
            smax = jnp.maximum(smax, hi_s[:, l * LANES:(l + 1) * LANES])
        return smin, smax

    smin, smax = lax.fori_loop(
        0, nt, score_tile,
        (jnp.full((TQ, LANES), jnp.inf, F32), jnp.full((TQ, LANES), -jnp.inf, F32)))
    lo = _to_key(jnp.broadcast_to(jnp.min(smin, axis=1, keepdims=True), (TQ, LANES)))
    hi = _to_key(jnp.broadcast_to(jnp.max(smax, axis=1, keepdims=True), (TQ, LANES)))

    tau, cut, tied = _threshold_and_cut(key_ref, cut_ref, nt, lo, hi, ksel, lend.astype(F32))

    neg_tile = jnp.full((TQ, LANES), NEG, BF16)
    for i in range(pad_tiles):
        mask_ref[0, i] = neg_tile

    def write(j, exact_ties):
        kk = key_ref[j]
        for i in range(nl):
            slab = kk[:, i * LANES:(i + 1) * LANES]
            keep = (_selected(slab, j * TK + i * LANES, tau, cut) > 0) if exact_ties else slab >= tau
            mask_ref[0, pad_tiles + nl * j + i] = jnp.where(keep, 0.0, NEG).T.astype(BF16)

    _write_selection(write, nt, tied)

    def clear(j, _):
        for i in range(nl):
            mask_ref[0, pad_tiles + nl * j + i] = neg_tile
        return 0

    lax.fori_loop(nt, NT, clear, 0)


def _select_prompt(qi, wi, ki, n_sel, T):
    TQ, TK = Q_STEP, FAR_TILE
    HI = wi.shape[1]
    HG = 4 if HI % 4 == 0 else 1
    NT = ki.shape[0] // TK
    steps = T // TQ
    ntile = (FRONT_PAD + NT * TK) // LANES
    kern = functools.partial(_select_prompt_kernel, HI=HI, HG=HG, NT=NT, n_sel=n_sel,
                             wscale=float(HI ** -0.5 * LANES ** -0.5))
    vmem = (2 * (TQ * HI * LANES * 2 + NT * TK * LANES * 2 + ntile * TQ * LANES * 2)
            + HI * TQ * LANES * 6 + NT * TQ * TK * 4 + TQ * TK * 4 + 4 * HG * TQ * TK * 4)
    return pl.pallas_call(
        kern,
        grid=(steps,),
        in_specs=[pl.BlockSpec((TQ, HI * LANES), lambda i: (i, 0)),
                  pl.BlockSpec((TQ, HI), lambda i: (i, 0)),
                  pl.BlockSpec((NT * TK, LANES), lambda i: (0, 0))],
        out_specs=pl.BlockSpec((1, ntile, TQ, LANES), lambda i: (i, 0, 0, 0)),
        out_shape=jax.ShapeDtypeStruct((steps, ntile, TQ, LANES), BF16),
        scratch_shapes=[pltpu.VMEM((HI * TQ, LANES), BF16),
                        pltpu.VMEM((HI * TQ, LANES), F32),
                        pltpu.VMEM((NT, TQ, TK), jnp.int32),
                        pltpu.VMEM((TQ, TK), F32),
                        pltpu.VMEM((TQ, LANES), jnp.int32)],
        compiler_params=_params(("parallel",), vmem),
        name="dsa_select_prompt",
    )(qi, wi, ki)


def _attn_kernel(q_ref, k_ref, v_ref, mask_ref, nb_ref, o_ref, q2_ref, m_ref, l_ref, acc_ref, *,
                 TQ, TK, G, lend_a, lend_b, step_axis):
    nl = TK // LANES
    R = G * TQ
    lend = lend_a * pl.program_id(step_axis) + lend_b
    jl = (lend - 1) // TK

    for g in range(G):
        q2_ref[g * TQ:(g + 1) * TQ, :] = q_ref[:, g * HEAD_DIM:(g + 1) * HEAD_DIM]
    m_ref[...] = jnp.full((R, LANES), NEG, F32)
    l_ref[...] = jnp.zeros((R, LANES), F32)
    acc_ref[...] = jnp.zeros((R, HEAD_DIM), F32)

    def tile(j, half):
        start = pl.multiple_of(j * TK, TK)
        kt = k_ref[pl.ds(start, TK), :]
        vt = v_ref[pl.ds(start, TK), :]
        s = lax.dot_general(q2_ref[...], kt, (((1,), (1,)), ((), ())),
                            preferred_element_type=F32)
        madd = jnp.where(mask_ref[0, j].astype(jnp.int32) != 0, 0.0, NEG)
        rows = []
        for g in range(G):
            sg = s[g * TQ:(g + 1) * TQ, :]
            if half is not None:
                sg = sg + nb_ref[0, g, :, half * TK:(half + 1) * TK]
            rows.append(sg + madd)
        s = jnp.concatenate(rows, axis=0)
        m_old = m_ref[...]
        m_cur = s[:, :LANES]
        for l in range(1, nl):
            m_cur = jnp.maximum(m_cur, s[:, l * LANES:(l + 1) * LANES])
        m_new = jnp.maximum(m_old, jnp.broadcast_to(jnp.max(m_cur, axis=1, keepdims=True), (R, LANES)))
        alpha = jnp.exp2(m_old - m_new)
        p = jnp.exp2(s - jnp.concatenate([m_new] * nl, axis=1))
        psum = p[:, :LANES]
        for l in range(1, nl):
            psum = psum + p[:, l * LANES:(l + 1) * LANES]
        l_ref[...] = alpha * l_ref[...] + jnp.broadcast_to(
            jnp.sum(psum, axis=1, keepdims=True), (R, LANES))
        acc_ref[...] = alpha * acc_ref[...] + jnp.dot(
            p.astype(BF16), vt, preferred_element_type=F32)
        m_ref[...] = m_new

    def far(j, _):
        tile(j, None)
        return 0

    lax.fori_loop(0, jnp.maximum(jl - 1, 0), far, 0)

    @pl.when(jl >= 1)
    def _():
        tile(jl - 1, 0)

    tile(jl, 1)

    o = acc_ref[...] / l_ref[...]
    for g in range(G):
        o_ref[:, g * HEAD_DIM:(g + 1) * HEAD_DIM] = o[g * TQ:(g + 1) * TQ, :].astype(o_ref.dtype)


def _attention(q, k, v, mask, nb, *, TQ, lend_a, lend_b, batched):
    TK = KEY_TILE
    G = GROUP
    Mq, HD = q.shape
    KV = HD // (G * HEAD_DIM)
    steps = Mq // TQ
    NT = mask.shape[1]
    L = k.shape[-2]
    P = nb.shape[0]
    if batched:
        grid = (steps, KV)
        q_map = lambda b, h: (b, h)
        kv_spec = pl.BlockSpec((None, L, HEAD_DIM), lambda b, h: (b, 0, h))
        mask_map = lambda b, h: (b, 0, 0, 0)
        nb_map = lambda b, h: (0, h, 0, 0)
        step_axis = 0
    else:
        grid = (KV, steps)
        q_map = lambda h, c: (c, h)
        kv_spec = pl.BlockSpec((L, HEAD_DIM), lambda h, c: (0, h))
        mask_map = lambda h, c: (c, 0, 0, 0)
        nb_map = lambda h, c: (c % P, h, 0, 0)
        step_axis = 1
    kern = functools.partial(_attn_kernel, TQ=TQ, TK=TK, G=G, lend_a=lend_a, lend_b=lend_b,
                             step_axis=step_axis)
    R = G * TQ
    vmem = (2 * (2 * TQ * G * HEAD_DIM * 2 + 2 * L * HEAD_DIM * 2 + NT * TQ * TK + G * TQ * 2 * TK * 4)
            + R * LANES * 14 + 6 * R * TK * 4)
    return pl.pallas_call(
        kern,
        grid=grid,
        in_specs=[pl.BlockSpec((TQ, G * HEAD_DIM), q_map),
                  kv_spec, kv_spec,
                  pl.BlockSpec((1, NT, TQ, TK), mask_map),
                  pl.BlockSpec((1, G, TQ, 2 * TK), nb_map)],
        out_specs=pl.BlockSpec((TQ, G * HEAD_DIM), q_map),
        out_shape=jax.ShapeDtypeStruct((Mq, HD), BF16),
        scratch_shapes=[pltpu.VMEM((R, HEAD_DIM), BF16),
                        pltpu.VMEM((R, LANES), F32),
                        pltpu.VMEM((R, LANES), F32),
                        pltpu.VMEM((R, HEAD_DIM), F32)],
        compiler_params=_params(("parallel", "arbitrary"), vmem),
        name="dsa_attention",
    )(q, k, v, mask, nb)


def _attn_prompt_kernel(q_ref, k_ref, v_ref, mask_ref, nb_ref, o_ref,
                        q2_ref, m_ref, acc_ref, sa_ref, sb_ref, p_ref):
    TQ, TK, G = Q_STEP, FAR_TILE, GROUP
    R = G * TQ
    NW = 2 * TQ
    c2 = pl.program_id(1)
    far_len = TQ * (c2 - 1)
    nfar = jnp.maximum((far_len + TK - 1) // TK, 0)
    nt_dims = (((1,), (1,)), ((), ()))

    for g in range(G):
        q2_ref[g * TQ:(g + 1) * TQ, :] = q_ref[:, g * HEAD_DIM:(g + 1) * HEAD_DIM]
    m_ref[...] = jnp.full((1, R), NEG, F32)
    acc_ref[...] = jnp.zeros(acc_ref.shape, F32)

    def far_start(j):
        return pl.multiple_of(jnp.maximum(far_len - TK * j, 0), LANES)

    def logits(start, width):
        return lax.dot_general(k_ref[pl.ds(start, width), :], q2_ref[...], nt_dims,
                               preferred_element_type=F32)

    def fold8(x, op):
        y = x[:SUBLANES]
        for i in range(1, x.shape[0] // SUBLANES):
            y = op(y, x[i * SUBLANES:(i + 1) * SUBLANES])
        return y

    def update(s_ref, start, width, biased):
        t0 = start // LANES
        CH = 64
        mx = jnp.full((SUBLANES, R), NEG, F32)
        for r in range(width // CH):
            rows = slice(r * CH, (r + 1) * CH)
            off = (r * CH) % LANES
            madd = mask_ref[0, t0 + (r * CH) // LANES, off:off + CH, :].astype(F32)
            x = s_ref[rows, :] + jnp.concatenate([madd] * G, axis=1)
            if biased:
                x = x + nb_ref[rows, :]
            s_ref[rows, :] = x
            mx = jnp.maximum(mx, fold8(x, jnp.maximum))
        m_old = m_ref[...]
        m_new = jnp.maximum(m_old, jnp.max(mx, axis=0, keepdims=True))
        alpha = jnp.exp2(m_old - m_new)
        for r in range(width // CH):
            rows = slice(r * CH, (r + 1) * CH)
            p_ref[rows, :] = jnp.exp2((s_ref[rows, :] - m_new).astype(BF16))
        vt = jnp.concatenate([v_ref[t0 + i] for i in range(width // LANES)], axis=1)
        acc_ref[...] = alpha * acc_ref[...] + jnp.dot(
            vt, p_ref[0:width, :], preferred_element_type=F32)
        m_ref[...] = m_new

    sa_ref[...] = logits(far_start(0), TK)
    near = pl.multiple_of(TQ * c2 + FRONT_PAD - TQ, LANES)
    sb_ref[0:NW, :] = logits(near, NW)
    update(sb_ref, near, NW, True)

    def pair(j):
        sb_ref[...] = logits(far_start(j + 1), TK)
        update(sa_ref, far_start(j), TK, False)
        sa_ref[...] = logits(far_start(j + 2), TK)
        update(sb_ref, far_start(j + 1), TK, False)

    def quad(i, _):
        pair(4 * i)
        pair(4 * i + 2)
        return 0

    nquad = nfar // 4
    lax.fori_loop(0, nquad, quad, 0)

    def rest(i, _):
        pair(4 * nquad + 2 * i)
        return 0

    lax.fori_loop(0, (nfar - 4 * nquad + 1) // 2, rest, 0)

    o = acc_ref[0:HEAD_DIM, :] / acc_ref[HEAD_DIM:HEAD_DIM + 1, :]
    for g in range(G):
        o_ref[:, g * HEAD_DIM:(g + 1) * HEAD_DIM] = o[:, g * TQ:(g + 1) * TQ].T.astype(o_ref.dtype)


def _attention_prompt(q, k, v, mask, nb):
    TQ, TK, G = Q_STEP, FAR_TILE, GROUP
    HD = q.shape[1]
    KV = HD // (G * HEAD_DIM)
    steps = mask.shape[0]
    T = steps * TQ
    ntile = mask.shape[1]
    Lp = k.shape[0]
    assert Lp == ntile * LANES and Lp >= FRONT_PAD + T and TQ == LANES
    R = G * TQ
    VR = HEAD_DIM + ONES_ROWS
    vt = v.reshape(ntile, LANES, KV, HEAD_DIM).transpose(2, 0, 3, 1)
    vt = jnp.concatenate([vt, jnp.ones((KV, ntile, ONES_ROWS, LANES), BF16)], axis=2)
    nbt = nb.reshape(KV, G, TQ, 2 * TQ).transpose(0, 3, 1, 2).reshape(KV, 2 * TQ, R)
    vmem = (2 * (2 * TQ * G * HEAD_DIM * 2 + 2 * Lp * HEAD_DIM * 2 + ntile * TQ * LANES * 2
                 + 2 * TQ * R * 4)
            + R * LANES * 6 + 2 * R * TK * 4 + 5 * R * TK * 4)
    q_map = lambda h, c: (c, h)
    return pl.pallas_call(
        _attn_prompt_kernel,
        grid=(KV, steps),
        in_specs=[pl.BlockSpec((TQ, G * HEAD_DIM), q_map),
                  pl.BlockSpec((Lp, HEAD_DIM), lambda h, c: (0, h)),
                  pl.BlockSpec((None, ntile, VR, LANES), lambda h, c: (h, 0, 0, 0)),
                  pl.BlockSpec((1, ntile, LANES, TQ), lambda h, c: (c, 0, 0, 0)),
                  pl.BlockSpec((None, 2 * TQ, R), lambda h, c: (h, 0, 0))],
        out_specs=pl.BlockSpec((TQ, G * HEAD_DIM), q_map),
        out_shape=jax.ShapeDtypeStruct((T, HD), BF16),
        scratch_shapes=[pltpu.VMEM((R, HEAD_DIM), BF16),
                        pltpu.VMEM((1, R), F32),
                        pltpu.VMEM((VR, R), F32),
                        pltpu.VMEM((TK, R), F32),
                        pltpu.VMEM((TK, R), F32),
                        pltpu.VMEM((TK, R), BF16)],
        compiler_params=_params(("parallel", "arbitrary"), vmem),
        name="dsa_attention_prompt",
    )(q, k, vt, mask, nbt)


def _t5_bucket_np(rel):
    nb = N_BUCKETS // 2
    max_exact = nb // 2
    side = np.where(rel > 0, nb, 0)
    n = np.abs(rel)
    nf = np.maximum(n, 1).astype(np.float32)
    large = max_exact + (np.log(nf / np.float32(max_exact))
                         / np.float32(math.log(REL_MAX_DIST / max_exact))
                         * np.float32(nb - max_exact)).astype(np.int32)
    large = np.minimum(large, nb - 1)
    return side + np.where(n < max_exact, n, large)


def _near_bias(rel_bias, rel, far_rel):
    far = int(_t5_bucket_np(np.array(far_rel)))
    assert far_rel < 0 and far == int(_t5_bucket_np(np.array(-10 ** 6)))
    rb = rel_bias.astype(F32) * LOG2E
    tab = rb[_t5_bucket_np(rel)] - rb[far][None, None, None, :]
    return tab.transpose(0, 3, 1, 2)


def _near_bias_stream(rel_bias, TQ, TK, phases):
    t = np.arange(TQ)[:, None]
    j = np.arange(2 * TK)[None, :]
    rel = np.stack([j - TK - ph + TQ - t for ph in phases])
    return _near_bias(rel_bias, rel, TQ - TK - 2)


def _near_bias_prompt(rel_bias):
    TQ = Q_STEP
    rel = np.arange(2 * TQ)[None, :] - TQ - np.arange(TQ)[:, None]
    return _near_bias(rel_bias, rel[None], -TQ - 1)[0]


def _conv_kernel(x_ref, halo_ref, buf_ref, w_ref, b_ref, xc_ref, xcb_ref, ext_ref, *, tt):
    first = pl.program_id(1) == 0
    ext_ref[0:SUBLANES, :] = jnp.where(first, buf_ref[...], halo_ref[...])
    ext_ref[SUBLANES:, :] = x_ref[...]
    y = jnp.broadcast_to(b_ref[...], x_ref.shape)
    for j in range(CONV_W):
        off = SUBLANES - (CONV_W - 1) + j
        y = y + ext_ref[off:off + tt, :] * w_ref[j:j + 1, :]
    xc_ref[...] = y
    xcb_ref[...] = y.astype(BF16)


def _conv(x, buf8, w, b, *, row0, B, T, tt):
    C = x.shape[1]
    assert row0 % tt == 0 and T % tt == 0 and tt % SUBLANES == 0
    nt = T // tt
    hb = tt // SUBLANES
    blk0 = row0 // tt
    main_map = lambda b, i: (blk0 + b * nt + i, 0)
    halo_map = lambda b, i: (jnp.maximum((blk0 + b * nt + i) * hb - 1, 0), 0)
    out_map = lambda b, i: (b * nt + i, 0)
    return pl.pallas_call(
        functools.partial(_conv_kernel, tt=tt),
        grid=(B, nt),
        in_specs=[pl.BlockSpec((tt, C), main_map),
                  pl.BlockSpec((SUBLANES, C), halo_map),
                  pl.BlockSpec((None, SUBLANES, C), lambda b, i: (b, 0, 0)),
                  pl.BlockSpec((CONV_W, C), lambda b, i: (0, 0)),
                  pl.BlockSpec((1, C), lambda b, i: (0, 0))],
        out_specs=[pl.BlockSpec((tt, C), out_map), pl.BlockSpec((tt, C), out_map)],
        out_shape=[jax.ShapeDtypeStruct((B * T, C), F32),
                   jax.ShapeDtypeStruct((B * T, C), BF16)],
        scratch_shapes=[pltpu.VMEM((tt + SUBLANES, C), F32)],
        compiler_params=_params(("parallel", "parallel"), 2 * tt * C * 10 + tt * C * 12),
        name="causal_conv",
    )(x, x, buf8, w, b.reshape(1, C))


def _scan_kernel(a_ref, u_ref, g_ref, h0_ref, y_ref, hlast_ref, h_ref, hs_ref, *, tt):
    i = pl.program_id(1)

    @pl.when(i == 0)
    def _():
        h_ref[...] = h0_ref[...]

    def step(t, h):
        h = a_ref[pl.ds(t, 1), :] * h + u_ref[pl.ds(t, 1), :]
        hs_ref[pl.ds(t, 1), :] = h
        return h

    h = lax.fori_loop(0, tt, step, h_ref[...])
    h_ref[...] = h
    hlast_ref[...] = h
    y_ref[...] = (hs_ref[...] * jax.nn.gelu(g_ref[...])).astype(y_ref.dtype)


def _scan(a, u, g, h0, *, row0, B, T, tt):
    C = a.shape[1]
    assert row0 % tt == 0 and T % tt == 0
    nt = T // tt
    blk0 = row0 // tt
    blk = pl.BlockSpec((tt, C), lambda b, i: (b * nt + i, 0))
    g_blk = pl.BlockSpec((tt, C), lambda b, i: (blk0 + b * nt + i, 0))
    vec = pl.BlockSpec((None, 1, C), lambda b, i: (b, 0, 0))
    return pl.pallas_call(
        functools.partial(_scan_kernel, tt=tt),
        grid=(B, nt),
        in_specs=[blk, blk, g_blk, vec],
        out_specs=[blk, vec],
        out_shape=[jax.ShapeDtypeStruct((B * T, C), BF16),
                   jax.ShapeDtypeStruct((B, 1, C), F32)],
        scratch_shapes=[pltpu.VMEM((1, C), F32), pltpu.VMEM((tt, C), F32)],
        compiler_params=_params(("parallel", "arbitrary"), 2 * tt * C * 14 + tt * C * 12),
        name="rglru_scan",
    )(a, u, g, h0)


GATE_TN = 256


def _gate_window(rb, C):
    raw = [((j * GATE_TN) // rb * rb) // LANES * LANES for j in range(C // GATE_TN)]
    ends = [((j * GATE_TN + GATE_TN - 1) // rb + 1) * rb for j in range(C // GATE_TN)]
    kw = -(-max(e - s for s, e in zip(raw, ends)) // LANES) * LANES
    kw = min(kw, C)
    starts = [min(s, C - kw) for s in raw]
    assert all(s + kw >= e for s, e in zip(starts, ends))
    return starts, kw


def _blockdiag_tiles(w):
    nblk, rb, _ = w.shape
    C = nblk * rb
    starts, kw = _gate_window(rb, C)
    dense = jax.scipy.linalg.block_diag(*[w[i] for i in range(nblk)])
    tiles = [dense[s:s + kw, j * GATE_TN:(j + 1) * GATE_TN] for j, s in enumerate(starts)]
    return jnp.stack(tiles).astype(BF16)


def _gates_kernel(x_ref, wa_ref, wx_ref, xc_ref, ba_ref, bx_ref, lam_ref, a_ref, u_ref, *, rb, kw):
    C = x_ref.shape[1]
    j = pl.program_id(1)
    start = jnp.minimum(((j * GATE_TN) // rb * rb) // LANES * LANES, C - kw)
    x = x_ref[:, pl.ds(pl.multiple_of(start, LANES), kw)]
    accs = [jnp.dot(x, w[0], preferred_element_type=F32) for w in (wa_ref, wx_ref)]
    a, u = _ep_rglru_gates(accs, [xc_ref[...]], [ba_ref[...], bx_ref[...], lam_ref[...]])
    a_ref[...] = a
    u_ref[...] = u


def _rglru_gates(xcb, xc, wa_t, wx_t, ba, bx, lam, *, rb, tm):
    rows, C = xc.shape
    nt, kw, _ = wa_t.shape
    tile = lambda: pl.BlockSpec((tm, GATE_TN), lambda i, j: (i, j))
    vec = lambda: pl.BlockSpec((1, GATE_TN), lambda i, j: (0, j))
    wspec = lambda: pl.BlockSpec((1, kw, GATE_TN), lambda i, j: (j, 0, 0))
    vmem = 2 * (tm * C * 2 + 2 * kw * GATE_TN * 2 + 3 * tm * GATE_TN * 4) + 8 * tm * GATE_TN * 4
    return pl.pallas_call(
        functools.partial(_gates_kernel, rb=rb, kw=kw),
        grid=(rows // tm, nt),
        in_specs=[pl.BlockSpec((tm, C), lambda i, j: (i, 0)), wspec(), wspec(), tile(),
                  vec(), vec(), vec()],
        out_specs=[tile(), tile()],
        out_shape=[jax.ShapeDtypeStruct((rows, C), F32)] * 2,
        compiler_params=_params(("parallel", "arbitrary"), vmem),
        name="rglru_gates",
    )(xcb, wa_t, wx_t, xc, ba.reshape(1, C), bx.reshape(1, C), lam.reshape(1, C))


def _pick(n, cands):
    for c in cands:
        if n % c == 0:
            return c
    raise ValueError(f"no tile for {n}")


def _layer(x, hist, p, rel_bias, dims):
    Tp, Bs, Ts, past = dims
    M, D = x.shape
    Ms = Bs * Ts
    k_past, v_past, ki_past, conv_buf, h0 = hist
    KV = k_past.shape[2]
    HQ = KV * GROUP * HEAD_DIM
    KVD = KV * HEAD_DIM
    DI = ki_past.shape[-1]
    C = p["conv_w"].shape[-1]
    n_in = p["w_in"].shape[1]
    HI = (n_in - HQ - 2 * KVD - DI - 2 * C - 2 * D) // (DI + 1)
    assert DI == LANES
    sizes = (HQ, KVD, KVD, HI * DI, DI, HI, C, C, D, D)
    offs = np.concatenate([[0], np.cumsum(sizes)])
    assert offs[-1] == n_in
    w_in = p["w_in"]

    def wslice(i, j=None):
        j = i if j is None else j
        return w_in[:, offs[i]:offs[j + 1]].astype(BF16)

    tm = _pick(M, (768, 512, 256, 128, 64))
    TK = KEY_TILE

    h = _rmsnorm(x, p["norm_mix"], tm)

    def proj(w, ep, dt, tn, vecs=(), name="in_proj"):
        return _matmul(h, [w], ep, [dt], tm=tm, tn=tn, vecs=vecs, name=name)[0]

    wide = (1024, 512, 256, 128)
    q = proj(wslice(0), _ep_headnorm(HEAD_DIM ** -0.5 * LOG2E), BF16, _pick(HQ, wide),
             vecs=[jnp.tile(p["q_norm"], HQ // HEAD_DIM)], name="in_proj_q")
    k = proj(wslice(1), _ep_headnorm(1.0), F32, _pick(KVD, wide),
             vecs=[jnp.tile(p["k_norm"], KV)], name="in_proj_k")
    v = proj(wslice(2), _ep_identity, F32, _pick(KVD, wide), name="in_proj_v")
    qi = proj(wslice(3), _ep_identity, BF16, _pick(HI * DI, wide), name="in_proj_qi")
    kw_w = jnp.pad(wslice(4, 5), ((0, 0), (0, 2 * LANES - DI - HI)))
    kiwi = proj(kw_w, _ep_identity, F32, 2 * LANES, name="in_proj_ki")
    ki, wi = kiwi[:, :DI], kiwi[:, DI:DI + HI]
    tn_c = _pick(C, (768, 384, 128))
    xr = proj(wslice(6), _ep_identity, F32, tn_c, name="in_proj_xr")
    gr = proj(wslice(7), _ep_identity, F32, tn_c, name="in_proj_gr")
    tn_d = _pick(D, (512, 256, 128))
    tn_w = _pick(D, wide)
    ga = proj(wslice(8), _ep_identity, F32, tn_w, name="in_proj_ga")
    gb = proj(wslice(9), _ep_identity, F32, tn_w, name="in_proj_gb")

    k_bf, v_bf, ki_bf = k.astype(BF16), v.astype(BF16), ki.astype(BF16)

    assert Tp % Q_STEP == 0
    back_p = -Tp % FAR_TILE
    n_sel_p = min(TOPK_MAX, Tp // 4)
    mask_p = _select_prompt(qi, wi, jnp.pad(ki_bf[:Tp], ((0, back_p), (0, 0))), n_sel_p, Tp)
    kv_pad = lambda a: jnp.pad(a[:Tp], ((FRONT_PAD, back_p), (0, 0)))
    o_p = _attention_prompt(q, kv_pad(k_bf), kv_pad(v_bf), mask_p, _near_bias_prompt(rel_bias))

    Ls = past + Ts
    nt_s = -(-Ls // TK)
    pad_s = nt_s * TK - Ls

    def with_cache(cache, new):
        new = new[Tp:].reshape(Bs, Ts, -1)
        parts = [cache.reshape(Bs, past, -1).astype(BF16), new]
        if pad_s:
            parts.append(jnp.zeros((Bs, pad_s, new.shape[-1]), BF16))
        return jnp.concatenate(parts, axis=1)

    n_sel_s = min(TOPK_MAX, Ls // 4)
    mask_s = _select(qi[Tp:], wi[Tp:], with_cache(ki_past, ki_bf), TQ=Ts, NT=nt_s,
                     lend_a=0, lend_b=Ls, n_sel=n_sel_s, batched=True)
    nb_s = _near_bias_stream(rel_bias, Ts, TK, [Ls - ((Ls - 1) // TK) * TK])
    o_s = _attention(q[Tp:], with_cache(k_past, k_bf), with_cache(v_past, v_bf), mask_s, nb_s,
                     TQ=Ts, lend_a=0, lend_b=Ls, batched=True)
    o_a = jnp.concatenate([o_p, o_s], axis=0)

    assert C % GATE_TN == 0
    wa_t = _blockdiag_tiles(p["rg_wa"])
    wx_t = _blockdiag_tiles(p["rg_wx"])

    def griffin(row0, B, T, buf, h_init):
        tt = _pick(math.gcd(T, row0) if row0 else T, (128, 64, 32, 16, 8))
        buf8 = jnp.pad(buf.astype(F32), ((0, 0), (SUBLANES - (CONV_W - 1), 0), (0, 0)))
        xc, xcb = _conv(xr, buf8, p["conv_w"], p["conv_b"], row0=row0, B=B, T=T, tt=tt)
        rows = B * T
        tmr = _pick(rows, (1024, 512, 256, 128, 64, 32))
        a, u = _rglru_gates(xcb, xc, wa_t, wx_t, p["rg_ba"], p["rg_bx"], p["rg_lambda"],
                            rb=p["rg_wa"].shape[1], tm=tmr)
        y, h_last = _scan(a, u, gr, h_init.reshape(B, 1, C), row0=row0, B=B, T=T, tt=tt)
        tail = xr[row0:row0 + rows].reshape(B, T, C)[:, -(CONV_W - 1):]
        conv_new = jnp.concatenate([buf.astype(F32), tail], axis=1)[:, -(CONV_W - 1):]
        return y, conv_new, h_last.reshape(B, C)

    y_p, conv_p, h_p = griffin(0, 1, Tp, jnp.zeros((1, CONV_W - 1, C), F32), jnp.zeros((1, C), F32))
    y_s, conv_s, h_s = griffin(Tp, Bs, Ts, conv_buf, h0)
    o_b = jnp.concatenate([y_p, y_s], axis=0)

    part = _matmul(o_a, [p["w_out_attn"].astype(BF16)], _ep_gate, [F32], tm=tm, tn=tn_w,
                   exts=[ga], name="out_attn")[0]
    merged = _matmul(o_b, [p["w_out_rg"].astype(BF16)], _ep_gate_add, [BF16],
                     tm=_pick(M, (512, 256, 128, 64)), tn=tn_w, exts=[gb, part], name="out_rg")[0]
    x1 = _matmul(merged, [p["w_o"].astype(BF16)], _ep_residual, [F32], tm=tm, tn=tn_w,
                 exts=[x], name="w_o")[0]
    hf = _rmsnorm(x1, p["norm_ffn"], tm)
    FF = p["ffn_w1"].shape[1]
    tn_f = _pick(FF, (512, 256, 128))
    act = _matmul(hf, [p["ffn_w1"].astype(BF16), p["ffn_w3"].astype(BF16)], _ep_swiglu, [BF16],
                  tm=tm, tn=tn_f, name="ffn_up")[0]
    x2 = _matmul(act, [p["ffn_w2"].astype(BF16)], _ep_residual, [F32],
                 tm=_pick(M, (384, 256, 128, 64)), tn=tn_d, exts=[x1], name="ffn_down")[0]

    new_p = (k[:Tp].reshape(1, Tp, KV, HEAD_DIM), v[:Tp].reshape(1, Tp, KV, HEAD_DIM),
             ki[:Tp].reshape(1, Tp, DI), conv_p, h_p)
    new_s = (k[Tp:].reshape(Bs, Ts, KV, HEAD_DIM), v[Tp:].reshape(Bs, Ts, KV, HEAD_DIM),
             ki[Tp:].reshape(Bs, Ts, DI), conv_s, h_s)
    return x2, new_p, new_s


def kernel(x_prompt, x_sample, cache_k, cache_v, cache_kidx, state_conv, state_rglru, norm_mix, w_in, q_norm, k_norm, rel_bias, conv_w, conv_b, rg_wa, rg_ba, rg_wx, rg_bx, rg_lambda, w_out_attn, w_out_rg, w_o, norm_ffn, ffn_w1, ffn_w3, ffn_w2):
    Bp, Tp, D = x_prompt.shape
    Bs, Ts, _ = x_sample.shape
    assert Bp == 1 and Tp % CHUNK == 0
    depth = w_in.shape[0]
    past = cache_k.shape[2]
    x = jnp.concatenate([x_prompt.reshape(Tp, D), x_sample.reshape(Bs * Ts, D)], axis=0)
    outs_p, outs_s = [], []
    for l in range(depth):
        p = dict(norm_mix=norm_mix[l], w_in=w_in[l], q_norm=q_norm[l], k_norm=k_norm[l],
                 conv_w=conv_w[l], conv_b=conv_b[l], rg_wa=rg_wa[l], rg_ba=rg_ba[l],
                 rg_wx=rg_wx[l], rg_bx=rg_bx[l], rg_lambda=rg_lambda[l],
                 w_out_attn=w_out_attn[l], w_out_rg=w_out_rg[l], w_o=w_o[l],
                 norm_ffn=norm_ffn[l], ffn_w1=ffn_w1[l], ffn_w3=ffn_w3[l], ffn_w2=ffn_w2[l])
        hist = (cache_k[l], cache_v[l], cache_kidx[l], state_conv[l], state_rglru[l])
        x, new_p, new_s = _layer(x, hist, p, rel_bias, (Tp, Bs, Ts, past))
        outs_p.append(new_p)
        outs_s.append(new_s)
    stack = lambda outs, i: jnp.stack([o[i] for o in outs])
    return (x[:Tp].reshape(1, Tp, D), x[Tp:].reshape(Bs, Ts, D),
            *[stack(outs_p, i) for i in range(5)],
            *[stack(outs_s, i) for i in range(5)])
```

```python
import functools
import math

import numpy as np
import jax
import jax.numpy as jnp
from jax import lax
from jax.experimental import pallas as pl
from jax.experimental.pallas import tpu as pltpu

F32 = jnp.float32
BF16 = jnp.bfloat16

CHUNK = 64
HEAD_DIM = 128
GROUP = 4
TOPK_MAX = 256
N_BUCKETS = 32
REL_MAX_DIST = 128
RG_C = 8.0
CONV_W = 4
EPS = 1e-6

LANES = 128
SUBLANES = 8
KEY_TILE = 256
Q_STEP = 2 * CHUNK
FAR_TILE = 512
FRONT_PAD = FAR_TILE
ONES_ROWS = 16
LOG2E = math.log2(math.e)
VMEM_CAP = 56 << 20
NEG = -1e30


def _params(sem, vmem_bytes):
    limit = min(max(int(vmem_bytes) + (6 << 20), 24 << 20), VMEM_CAP)
    return pltpu.CompilerParams(dimension_semantics=sem, vmem_limit_bytes=limit)


def _rmsnorm_kernel(x_ref, g_ref, o_ref):
    x = x_ref[...]
    ms = jnp.mean(x * x, axis=-1, keepdims=True)
    o_ref[...] = (x * lax.rsqrt(ms + EPS) * g_ref[...]).astype(o_ref.dtype)


def _rmsnorm(x, g, tm):
    M, D = x.shape
    tm = min(tm, 256)
    return pl.pallas_call(
        _rmsnorm_kernel,
        grid=(M // tm,),
        in_specs=[pl.BlockSpec((tm, D), lambda i: (i, 0)),
                  pl.BlockSpec((1, D), lambda i: (0, 0))],
        out_specs=pl.BlockSpec((tm, D), lambda i: (i, 0)),
        out_shape=jax.ShapeDtypeStruct((M, D), BF16),
        compiler_params=_params(("parallel",), 2 * tm * D * 6 + 2 * tm * D * 4),
        name="rmsnorm",
    )(x, g.reshape(1, D))


def _mm_kernel(*refs, n_w, n_ext, n_vec, n_out, nk, epilogue):
    x_ref = refs[0]
    w_refs = refs[1:1 + n_w]
    p = 1 + n_w
    ext_refs = refs[p:p + n_ext]
    p += n_ext
    vec_refs = refs[p:p + n_vec]
    p += n_vec
    out_refs = refs[p:p + n_out]
    acc_refs = refs[p + n_out:]

    def finish(accs):
        res = epilogue(accs, [e[...] for e in ext_refs], [v[...] for v in vec_refs])
        for o, r in zip(out_refs, res):
            o[...] = r.astype(o.dtype)

    if nk == 1:
        finish([jnp.dot(x_ref[...], w[...], preferred_element_type=F32) for w in w_refs])
        return

    k = pl.program_id(2)

    @pl.when(k == 0)
    def _():
        for a in acc_refs:
            a[...] = jnp.zeros_like(a)

    for a, w in zip(acc_refs, w_refs):
        a[...] += jnp.dot(x_ref[...], w[...], preferred_element_type=F32)

    @pl.when(k == nk - 1)
    def _():
        finish([a[...] for a in acc_refs])


def _matmul(x, ws, epilogue, out_dtypes, *, tm, tn, tk=None, exts=(), vecs=(), name="matmul"):
    M, K = x.shape
    n = ws[0].shape[1]
    tk = K if tk is None else tk
    nk = K // tk
    assert M % tm == 0 and n % tn == 0 and K % tk == 0
    x_map = lambda j, i, k: (i, k)
    w_map = lambda j, i, k: (k, j)
    mn_map = lambda j, i, k: (i, j)
    in_specs = [pl.BlockSpec((tm, tk), x_map)]
    in_specs += [pl.BlockSpec((tk, tn), w_map) for _ in ws]
    in_specs += [pl.BlockSpec((tm, tn), mn_map) for _ in exts]
    in_specs += [pl.BlockSpec((1, tn), lambda j, i, k: (0, j)) for _ in vecs]
    out_specs = [pl.BlockSpec((tm, tn), mn_map) for _ in out_dtypes]
    out_shape = [jax.ShapeDtypeStruct((M, n), dt) for dt in out_dtypes]
    scratch = [pltpu.VMEM((tm, tn), F32) for _ in ws] if nk > 1 else []
    vmem = (2 * (tm * tk * 2 + len(ws) * tk * tn * 2 + (len(exts) + len(out_dtypes)) * tm * tn * 4)
            + 3 * len(ws) * tm * tn * 4)
    kern = functools.partial(_mm_kernel, n_w=len(ws), n_ext=len(exts), n_vec=len(vecs),
                             n_out=len(out_dtypes), nk=nk, epilogue=epilogue)
    outs = pl.pallas_call(
        kern,
        grid=(n // tn, M // tm, nk),
        in_specs=in_specs,
        out_specs=out_specs,
        out_shape=out_shape,
        scratch_shapes=scratch,
        compiler_params=_params(("parallel", "parallel", "arbitrary"), vmem),
        name=name,
    )(x, *ws, *exts, *[v.reshape(1, -1) for v in vecs])
    return outs


def _ep_identity(accs, exts, vecs):
    return (accs[0],)


def _ep_headnorm(scale):
    def ep(accs, exts, vecs):
        a, g = accs[0], vecs[0]
        outs = []
        for j in range(a.shape[1] // HEAD_DIM):
            aj = a[:, j * HEAD_DIM:(j + 1) * HEAD_DIM]
            ms = jnp.mean(aj * aj, axis=-1, keepdims=True)
            outs.append(aj * lax.rsqrt(ms + EPS) * g[:, j * HEAD_DIM:(j + 1) * HEAD_DIM])
        y = jnp.concatenate(outs, axis=1)
        return (y * scale if scale != 1.0 else y,)
    return ep


def _ep_gate(accs, exts, vecs):
    return (jax.nn.sigmoid(exts[0]) * accs[0],)


def _ep_gate_add(accs, exts, vecs):
    return (exts[1] + jax.nn.sigmoid(exts[0]) * accs[0],)


def _ep_residual(accs, exts, vecs):
    return (exts[0] + accs[0],)


def _ep_swiglu(accs, exts, vecs):
    return (jax.nn.silu(accs[0]) * accs[1],)


def _softplus(x):
    return jnp.maximum(x, 0.0) + jnp.log1p(jnp.exp(-jnp.abs(x)))


def _ep_rglru_gates(accs, exts, vecs):
    xc = exts[0]
    r = jax.nn.sigmoid(accs[0] + vecs[0])
    i = jax.nn.sigmoid(accs[1] + vecs[1])
    log_a = -RG_C * r * _softplus(-vecs[2])
    a = jnp.exp(log_a)
    u = jnp.sqrt(1.0 - a * a) * (i * xc)
    return (a, u)


INT_MAX = 2 ** 31 - 1


def _open_rows(lo, hi):
    return jnp.max(jnp.where(lo < hi, 1.0, 0.0))


def _to_key(s):
    b = pltpu.bitcast(s, jnp.int32)
    return b ^ ((b >> 31) & jnp.int32(0x7FFFFFFF))


def _from_key(k):
    return pltpu.bitcast(k ^ ((k >> 31) & jnp.int32(0x7FFFFFFF)), F32)


def _kth_largest_key(count_ge, lo, hi, ksel, n_valid):
    def body(st):
        lo, hi, c_lo, _, it = st
        active = lo < hi
        mid_k = (lo >> 1) + (hi >> 1) + ((lo | hi) & 1)
        mid_f = _to_key(0.5 * _from_key(lo) + 0.5 * _from_key(hi))
        mid_f = jnp.minimum(jnp.maximum(mid_f, lo + 1), hi)
        mid = jnp.where((it & 1) == 0, mid_f, mid_k)
        c = count_ge(mid)
        ge = c >= ksel
        lo_n = jnp.where(ge, mid, lo)
        c_n = jnp.where(ge, c, c_lo)
        hi_n = jnp.where(c == ksel, mid, jnp.where(ge, hi, mid - 1))
        lo = jnp.where(active, lo_n, lo)
        c_lo = jnp.where(active, c_n, c_lo)
        hi = jnp.where(active, hi_n, hi)
        return lo, hi, c_lo, _open_rows(lo, hi), it + 1

    tau, _, c_tau, _, _ = lax.while_loop(lambda st: st[3] > 0.5, body,
                                         (lo, hi, n_valid, _open_rows(lo, hi), jnp.int32(0)))
    return tau, c_tau


def _tie_cut(count_tie_le, need, n_keys):
    def body(st):
        lo, hi, _ = st
        active = lo < hi
        mid = (lo + hi) >> 1
        ok = count_tie_le(mid) >= need
        hi = jnp.where(active, jnp.where(ok, mid, hi), hi)
        lo = jnp.where(active, jnp.where(ok, lo, mid + 1), lo)
        return lo, hi, _open_rows(lo, hi)

    lo = jnp.zeros(need.shape, jnp.int32)
    hi = jnp.full(need.shape, n_keys - 1, jnp.int32)
    cut, _, _ = lax.while_loop(lambda st: st[2] > 0.5, body, (lo, hi, _open_rows(lo, hi)))
    return cut


def _threshold_and_cut(key_ref, cut_ref, nt, lo, hi, ksel, n_valid):
    _, TQ, TK = key_ref.shape
    nl = TK // LANES
    lane = lax.broadcasted_iota(jnp.int32, (TQ, LANES), 1)

    def count(indicator):
        def body(j, c):
            kk = key_ref[j]
            for l in range(nl):
                c = c + indicator(kk[:, l * LANES:(l + 1) * LANES], j * TK + l * LANES)
            return c
        c = lax.fori_loop(0, nt, body, jnp.zeros((TQ, LANES), F32))
        return jnp.broadcast_to(jnp.sum(c, axis=1, keepdims=True), (TQ, LANES))

    tau, c_tau = _kth_largest_key(
        lambda mid: count(lambda kk, base: jnp.where(kk >= mid, 1.0, 0.0)), lo, hi, ksel, n_valid)
    cut_ref[...] = jnp.full((TQ, LANES), INT_MAX, jnp.int32)

    tied = jnp.max(jnp.where(c_tau > ksel, 1.0, 0.0)) > 0.5

    @pl.when(tied)
    def _():
        need = ksel - count(lambda kk, base: jnp.where(kk > tau, 1.0, 0.0))
        cut_ref[...] = _tie_cut(
            lambda mid: count(lambda kk, base: jnp.where(
                kk == tau, jnp.where(lane + base <= mid, 1.0, 0.0), 0.0)),
            need, nt * TK)

    return tau, cut_ref[...], tied


def _selected(kk, base, tau, cut):
    lane = lax.broadcasted_iota(jnp.int32, kk.shape, 1)
    tie = jnp.where(kk == tau, jnp.where(lane + base <= cut, 1, 0), 0)
    return jnp.where(kk > tau, 1, tie)


def _write_selection(write_tile, nt, tied):
    @pl.when(tied)
    def _():
        lax.fori_loop(0, nt, lambda j, _: write_tile(j, True) or 0, 0)

    @pl.when(jnp.logical_not(tied))
    def _():
        lax.fori_loop(0, nt, lambda j, _: write_tile(j, False) or 0, 0)
def _select_kernel(qi_ref, wi_ref, ki_ref, mask_ref, q2_ref, w2_ref, key_ref, cut_ref, *,
                   TQ, TK, NT, HI, lend_a, lend_b, n_sel, wscale):
    nl = TK // LANES
    lend = lend_a * pl.program_id(0) + lend_b
    nt = (lend + TK - 1) // TK
    ksel = jnp.minimum(n_sel, lend).astype(F32)

    wi = wi_ref[...] * wscale
    for h in range(HI):
        q2_ref[h * TQ:(h + 1) * TQ, :] = qi_ref[:, h * LANES:(h + 1) * LANES]
        w2_ref[h * TQ:(h + 1) * TQ, :] = jnp.broadcast_to(wi[:, h:h + 1], (TQ, LANES))

    lane = lax.broadcasted_iota(jnp.int32, (TQ, TK), 1)
    int_min = jnp.int32(-2 ** 31)

    def score_tile(j, carry):
        smin, smax = carry
        start = pl.multiple_of(j * TK, TK)
        kt = ki_ref[pl.ds(start, TK), :]
        s = lax.dot_general(q2_ref[...], kt, (((1,), (1,)), ((), ())),
                            preferred_element_type=F32)
        cols = []
        for l in range(nl):
            acc = jnp.zeros((TQ, LANES), F32)
            for h in range(HI):
                acc = acc + (jnp.maximum(s[h * TQ:(h + 1) * TQ, l * LANES:(l + 1) * LANES], 0.0)
                             * w2_ref[h * TQ:(h + 1) * TQ, :])
            cols.append(acc)
        sc = jnp.concatenate(cols, axis=1)
        valid = (lane + j * TK) < lend
        key_ref[j] = jnp.where(valid, _to_key(sc), int_min)
        lo_s = jnp.where(valid, sc, jnp.inf)
        hi_s = jnp.where(valid, sc, -jnp.inf)
        for l in range(nl):
            smin = jnp.minimum(smin, lo_s[:, l * LANES:(l + 1) * LANES])
            smax = jnp.maximum(smax, hi_s[:, l * LANES:(l + 1) * LANES])
        return smin, smax

    smin, smax = lax.fori_loop(
        0, nt, score_tile,
        (jnp.full((TQ, LANES), jnp.inf, F32), jnp.full((TQ, LANES), -jnp.inf, F32)))
    lo = _to_key(jnp.broadcast_to(jnp.min(smin, axis=1, keepdims=True), (TQ, LANES)))
    hi = _to_key(jnp.broadcast_to(jnp.max(smax, axis=1, keepdims=True), (TQ, LANES)))

    n_valid = jnp.full((TQ, LANES), lend, jnp.int32).astype(F32)
    tau, cut, tied = _threshold_and_cut(key_ref, cut_ref, nt, lo, hi,
                                        jnp.full((TQ, LANES), ksel, F32), n_valid)

    def write(j, exact_ties):
        kk = key_ref[j]
        sel = []
        for l in range(nl):
            slab = kk[:, l * LANES:(l + 1) * LANES]
            sel.append(_selected(slab, j * TK + l * LANES, tau, cut) if exact_ties
                       else jnp.where(slab >= tau, 1, 0))
        mask_ref[0, j] = jnp.concatenate(sel, axis=1).astype(jnp.int8)

    _write_selection(write, nt, tied)

    def clear(j, _):
        mask_ref[0, j] = jnp.zeros((TQ, TK), jnp.int8)
        return 0

    lax.fori_loop(nt, NT, clear, 0)


def _select(qi, wi, ki, *, TQ, NT, lend_a, lend_b, n_sel, batched):
    TK = KEY_TILE
    Mq = qi.shape[0]
    HI = wi.shape[1]
    steps = Mq // TQ
    if batched:
        ki_spec = pl.BlockSpec((None, NT * TK, LANES), lambda i: (i, 0, 0))
    else:
        ki_spec = pl.BlockSpec((NT * TK, LANES), lambda i: (0, 0))
    kern = functools.partial(_select_kernel, TQ=TQ, TK=TK, NT=NT, HI=HI, lend_a=lend_a,
                             lend_b=lend_b, n_sel=n_sel,
                             wscale=float(HI ** -0.5 * LANES ** -0.5))
    vmem = (2 * (TQ * HI * LANES * 2 + NT * TK * LANES * 2 + NT * TQ * TK)
            + HI * TQ * LANES * 6 + NT * TQ * TK * 4 + 3 * HI * TQ * TK * 4)
    return pl.pallas_call(
        kern,
        grid=(steps,),
        in_specs=[pl.BlockSpec((TQ, HI * LANES), lambda i: (i, 0)),
                  pl.BlockSpec((TQ, HI), lambda i: (i, 0)),
                  ki_spec],
        out_specs=pl.BlockSpec((1, NT, TQ, TK), lambda i: (i, 0, 0, 0)),
        out_shape=jax.ShapeDtypeStruct((steps, NT, TQ, TK), jnp.int8),
        scratch_shapes=[pltpu.VMEM((HI * TQ, LANES), BF16),
                        pltpu.VMEM((HI * TQ, LANES), F32),
                        pltpu.VMEM((NT, TQ, TK), jnp.int32),
                        pltpu.VMEM((TQ, LANES), jnp.int32)],
        compiler_params=_params(("parallel",), vmem),
        name="dsa_select",
    )(qi, wi, ki)


def _select_prompt_kernel(qi_ref, wi_ref, ki_ref, mask_ref, q2_ref, w2_ref, key_ref, acc_ref,
                          cut_ref, *,
                          HI, HG, NT, n_sel, wscale):
    TQ, TK = Q_STEP, FAR_TILE
    nl = TK // LANES
    pad_tiles = FRONT_PAD // LANES
    RB = 64
    c2 = pl.program_id(0)
    row = lax.broadcasted_iota(jnp.int32, (TQ, LANES), 0)
    lend = jnp.where(row < CHUNK, c2 * TQ + CHUNK, c2 * TQ + TQ)
    nt = (c2 * TQ + TQ + TK - 1) // TK
    ksel = jnp.minimum(n_sel, lend).astype(F32)

    wi = wi_ref[...] * wscale
    for h in range(HI):
        q2_ref[h * TQ:(h + 1) * TQ, :] = qi_ref[:, h * LANES:(h + 1) * LANES]
        w2_ref[h * TQ:(h + 1) * TQ, :] = jnp.broadcast_to(wi[:, h:h + 1], (TQ, LANES))

    lane = lax.broadcasted_iota(jnp.int32, (TQ, TK), 1)
    lend_t = jnp.concatenate([lend] * nl, axis=1)
    int_min = jnp.int32(-2 ** 31)

    def score_tile(j, carry):
        smin, smax = carry
        start = pl.multiple_of(j * TK, TK)
        kt = ki_ref[pl.ds(start, TK), :]
        for hg in range(HI // HG):
            s = lax.dot_general(q2_ref[hg * HG * TQ:(hg + 1) * HG * TQ, :], kt,
                                (((1,), (1,)), ((), ())), preferred_element_type=F32)
            for r in range(TQ // RB):
                cs = [None] * nl
                for h in range(HG):
                    r0 = h * TQ + r * RB
                    w = w2_ref[(hg * HG) * TQ + r0:(hg * HG) * TQ + r0 + RB, :]
                    for l in range(nl):
                        term = jnp.maximum(s[r0:r0 + RB, l * LANES:(l + 1) * LANES], 0.0) * w
                        cs[l] = term if cs[l] is None else cs[l] + term
                for l in range(nl):
                    if hg == 0:
                        acc_ref[r * RB:(r + 1) * RB, l * LANES:(l + 1) * LANES] = cs[l]
                    else:
                        acc_ref[r * RB:(r + 1) * RB, l * LANES:(l + 1) * LANES] += cs[l]
        sc = acc_ref[...]
        valid = (lane + j * TK) < lend_t
        key_ref[j] = jnp.where(valid, _to_key(sc), int_min)
        lo_s = jnp.where(valid, sc, jnp.inf)
        hi_s = jnp.where(valid, sc, -jnp.inf)
        for l in range(nl):
            smin = jnp.minimum(smin, lo_s[:, l * LANES:(l + 1) * LANES])
            smax = jnp.maximum(smax, hi_s[:, l * LANES:(l + 1) * LANES])
        return smin, smax

    smin, smax = lax.fori_loop(
        0, nt, score_tile,
        (jnp.full((TQ, LANES), jnp.inf, F32), jnp.full((TQ, LANES), -jnp.inf, F32)))
    lo = _to_key(jnp.broadcast_to(jnp.min(smin, axis=1, keepdims=True), (TQ, LANES)))
    hi = _to_key(jnp.broadcast_to(jnp.max(smax, axis=1, keepdims=True), (TQ, LANES)))

    tau, cut, tied = _threshold_and_cut(key_ref, cut_ref, nt, lo, hi, ksel, lend.astype(F32))

    neg_tile = jnp.full((TQ, LANES), NEG, BF16)
    for i in range(pad_tiles):
        mask_ref[0, i] = neg_tile

    def write(j, exact_ties):
        kk = key_ref[j]
        for i in range(nl):
            slab = kk[:, i * LANES:(i + 1) * LANES]
            keep = (_selected(slab, j * TK + i * LANES, tau, cut) > 0) if exact_ties else slab >= tau
            mask_ref[0, pad_tiles + nl * j + i] = jnp.where(keep, 0.0, NEG).T.astype(BF16)

    _write_selection(write, nt, tied)

    def clear(j, _):
        for i in range(nl):
            mask_ref[0, pad_tiles + nl * j + i] = neg_tile
        return 0

    lax.fori_loop(nt, NT, clear, 0)


def _select_prompt(qi, wi, ki, n_sel, T):
    TQ, TK = Q_STEP, FAR_TILE
    HI = wi.shape[1]
    HG = 4 if HI % 4 == 0 else 1
    NT = ki.shape[0] // TK
    steps = T // TQ
    ntile = (FRONT_PAD + NT * TK) // LANES
    kern = functools.partial(_select_prompt_kernel, HI=HI, HG=HG, NT=NT, n_sel=n_sel,
                             wscale=float(HI ** -0.5 * LANES ** -0.5))
    vmem = (2 * (TQ * HI * LANES * 2 + NT * TK * LANES * 2 + ntile * TQ * LANES * 2)
            + HI * TQ * LANES * 6 + NT * TQ * TK * 4 + TQ * TK * 4 + 4 * HG * TQ * TK * 4)
    return pl.pallas_call(
        kern,
        grid=(steps,),
        in_specs=[pl.BlockSpec((TQ, HI * LANES), lambda i: (i, 0)),
                  pl.BlockSpec((TQ, HI), lambda i: (i, 0)),
                  pl.BlockSpec((NT * TK, LANES), lambda i: (0, 0))],
        out_specs=pl.BlockSpec((1, ntile, TQ, LANES), lambda i: (i, 0, 0, 0)),
        out_shape=jax.ShapeDtypeStruct((steps, ntile, TQ, LANES), BF16),
        scratch_shapes=[pltpu.VMEM((HI * TQ, LANES), BF16),
                        pltpu.VMEM((HI * TQ, LANES), F32),
                        pltpu.VMEM((NT, TQ, TK), jnp.int32),
                        pltpu.VMEM((TQ, TK), F32),
                        pltpu.VMEM((TQ, LANES), jnp.int32)],
        compiler_params=_params(("parallel",), vmem),
        name="dsa_select_prompt",
    )(qi, wi, ki)


def _attn_kernel(q_ref, k_ref, v_ref, mask_ref, nb_ref, o_ref, q2_ref, m_ref, l_ref, acc_ref, *,
                 TQ, TK, G, lend_a, lend_b, step_axis):
    nl = TK // LANES
    R = G * TQ
    lend = lend_a * pl.program_id(step_axis) + lend_b
    jl = (lend - 1) // TK

    for g in range(G):
        q2_ref[g * TQ:(g + 1) * TQ, :] = q_ref[:, g * HEAD_DIM:(g + 1) * HEAD_DIM]
    m_ref[...] = jnp.full((R, LANES), NEG, F32)
    l_ref[...] = jnp.zeros((R, LANES), F32)
    acc_ref[...] = jnp.zeros((R, HEAD_DIM), F32)

    def tile(j, half):
        start = pl.multiple_of(j * TK, TK)
        kt = k_ref[pl.ds(start, TK), :]
        vt = v_ref[pl.ds(start, TK), :]
        s = lax.dot_general(q2_ref[...], kt, (((1,), (1,)), ((), ())),
                            preferred_element_type=F32)
        madd = jnp.where(mask_ref[0, j].astype(jnp.int32) != 0, 0.0, NEG)
        rows = []
        for g in range(G):
            sg = s[g * TQ:(g + 1) * TQ, :]
            if half is not None:
                sg = sg + nb_ref[0, g, :, half * TK:(half + 1) * TK]
            rows.append(sg + madd)
        s = jnp.concatenate(rows, axis=0)
        m_old = m_ref[...]
        m_cur = s[:, :LANES]
        for l in range(1, nl):
            m_cur = jnp.maximum(m_cur, s[:, l * LANES:(l + 1) * LANES])
        m_new = jnp.maximum(m_old, jnp.broadcast_to(jnp.max(m_cur, axis=1, keepdims=True), (R, LANES)))
        alpha = jnp.exp2(m_old - m_new)
        p = jnp.exp2(s - jnp.concatenate([m_new] * nl, axis=1))
        psum = p[:, :LANES]
        for l in range(1, nl):
            psum = psum + p[:, l * LANES:(l + 1) * LANES]
        l_ref[...] = alpha * l_ref[...] + jnp.broadcast_to(
            jnp.sum(psum, axis=1, keepdims=True), (R, LANES))
        acc_ref[...] = alpha * acc_ref[...] + jnp.dot(
            p.astype(BF16), vt, preferred_element_type=F32)
        m_ref[...] = m_new

    def far(j, _):
        tile(j, None)
        return 0

    lax.fori_loop(0, jnp.maximum(jl - 1, 0), far, 0)

    @pl.when(jl >= 1)
    def _():
        tile(jl - 1, 0)

    tile(jl, 1)

    o = acc_ref[...] / l_ref[...]
    for g in range(G):
        o_ref[:, g * HEAD_DIM:(g + 1) * HEAD_DIM] = o[g * TQ:(g + 1) * TQ, :].astype(o_ref.dtype)


def _attention(q, k, v, mask, nb, *, TQ, lend_a, lend_b, batched):
    TK = KEY_TILE
    G = GROUP
    Mq, HD = q.shape
    KV = HD // (G * HEAD_DIM)
    steps = Mq // TQ
    NT = mask.shape[1]
    L = k.shape[-2]
    P = nb.shape[0]
    if batched:
        grid = (steps, KV)
        q_map = lambda b, h: (b, h)
        kv_spec = pl.BlockSpec((None, L, HEAD_DIM), lambda b, h: (b, 0, h))
        mask_map = lambda b, h: (b, 0, 0, 0)
        nb_map = lambda b, h: (0, h, 0, 0)
        step_axis = 0
    else:
        grid = (KV, steps)
        q_map = lambda h, c: (c, h)
        kv_spec = pl.BlockSpec((L, HEAD_DIM), lambda h, c: (0, h))
        mask_map = lambda h, c: (c, 0, 0, 0)
        nb_map = lambda h, c: (c % P, h, 0, 0)
        step_axis = 1
    kern = functools.partial(_attn_kernel, TQ=TQ, TK=TK, G=G, lend_a=lend_a, lend_b=lend_b,
                             step_axis=step_axis)
    R = G * TQ
    vmem = (2 * (2 * TQ * G * HEAD_DIM * 2 + 2 * L * HEAD_DIM * 2 + NT * TQ * TK + G * TQ * 2 * TK * 4)
            + R * LANES * 14 + 6 * R * TK * 4)
    return pl.pallas_call(
        kern,
        grid=grid,
        in_specs=[pl.BlockSpec((TQ, G * HEAD_DIM), q_map),
                  kv_spec, kv_spec,
                  pl.BlockSpec((1, NT, TQ, TK), mask_map),
                  pl.BlockSpec((1, G, TQ, 2 * TK), nb_map)],
        out_specs=pl.BlockSpec((TQ, G * HEAD_DIM), q_map),
        out_shape=jax.ShapeDtypeStruct((Mq, HD), BF16),
        scratch_shapes=[pltpu.VMEM((R, HEAD_DIM), BF16),
                        pltpu.VMEM((R, LANES), F32),
                        pltpu.VMEM((R, LANES), F32),
                        pltpu.VMEM((R, HEAD_DIM), F32)],
        compiler_params=_params(("parallel", "arbitrary"), vmem),
        name="dsa_attention",
    )(q, k, v, mask, nb)


def _attn_prompt_kernel(q_ref, k_ref, v_ref, mask_ref, nb_ref, o_ref,
                        q2_ref, m_ref, acc_ref, sa_ref, sb_ref, p_ref):
    TQ, TK, G = Q_STEP, FAR_TILE, GROUP
    R = G * TQ
    NW = 2 * TQ
    c2 = pl.program_id(1)
    far_len = TQ * (c2 - 1)
    nfar = jnp.maximum((far_len + TK - 1) // TK, 0)
    nt_dims = (((1,), (1,)), ((), ()))

    for g in range(G):
        q2_ref[g * TQ:(g + 1) * TQ, :] = q_ref[:, g * HEAD_DIM:(g + 1) * HEAD_DIM]
    m_ref[...] = jnp.full((1, R), NEG, F32)
    acc_ref[...] = jnp.zeros(acc_ref.shape, F32)

    def far_start(j):
        return pl.multiple_of(jnp.maximum(far_len - TK * j, 0), LANES)

    def logits(start, width):
        return lax.dot_general(k_ref[pl.ds(start, width), :], q2_ref[...], nt_dims,
                               preferred_element_type=F32)

    def fold8(x, op):
        y = x[:SUBLANES]
        for i in range(1, x.shape[0] // SUBLANES):
            y = op(y, x[i * SUBLANES:(i + 1) * SUBLANES])
        return y

    def update(s_ref, start, width, biased):
        t0 = start // LANES
        CH = 32
        mx = jnp.full((SUBLANES, R), NEG, F32)
        for r in range(width // CH):
            rows = slice(r * CH, (r + 1) * CH)
            off = (r * CH) % LANES
            madd = mask_ref[0, t0 + (r * CH) // LANES, off:off + CH, :].astype(F32)
            x = s_ref[rows, :] + jnp.concatenate([madd] * G, axis=1)
            if biased:
                x = x + nb_ref[rows, :]
            s_ref[rows, :] = x
            mx = jnp.maximum(mx, fold8(x, jnp.maximum))
        m_old = m_ref[...]
        m_new = jnp.maximum(m_old, jnp.max(mx, axis=0, keepdims=True))
        alpha = jnp.exp2(m_old - m_new)
        for r in range(width // CH):
            rows = slice(r * CH, (r + 1) * CH)
            p_ref[rows, :] = jnp.exp2((s_ref[rows, :] - m_new).astype(BF16))
        vt = jnp.concatenate([v_ref[t0 + i] for i in range(width // LANES)], axis=1)
        acc_ref[...] = alpha * acc_ref[...] + jnp.dot(
            vt, p_ref[0:width, :], preferred_element_type=F32)
        m_ref[...] = m_new

    sa_ref[...] = logits(far_start(0), TK)
    near = pl.multiple_of(TQ * c2 + FRONT_PAD - TQ, LANES)
    sb_ref[0:NW, :] = logits(near, NW)
    update(sb_ref, near, NW, True)

    def pair(j):
        sb_ref[...] = logits(far_start(j + 1), TK)
        update(sa_ref, far_start(j), TK, False)
        sa_ref[...] = logits(far_start(j + 2), TK)
        update(sb_ref, far_start(j + 1), TK, False)

    def quad(i, _):
        pair(4 * i)
        pair(4 * i + 2)
        return 0

    nquad = nfar // 4
    lax.fori_loop(0, nquad, quad, 0)

    def rest(i, _):
        pair(4 * nquad + 2 * i)
        return 0

    lax.fori_loop(0, (nfar - 4 * nquad + 1) // 2, rest, 0)

    o = acc_ref[0:HEAD_DIM, :] / acc_ref[HEAD_DIM:HEAD_DIM + 1, :]
    for g in range(G):
        o_ref[:, g * HEAD_DIM:(g + 1) * HEAD_DIM] = o[:, g * TQ:(g + 1) * TQ].T.astype(o_ref.dtype)


def _attention_prompt(q, k, v, mask, nb):
    TQ, TK, G = Q_STEP, FAR_TILE, GROUP
    HD = q.shape[1]
    KV = HD // (G * HEAD_DIM)
    steps = mask.shape[0]
    T = steps * TQ
    ntile = mask.shape[1]
    Lp = k.shape[0]
    assert Lp == ntile * LANES and Lp >= FRONT_PAD + T and TQ == LANES
    R = G * TQ
    VR = HEAD_DIM + ONES_ROWS
    vt = v.reshape(ntile, LANES, KV, HEAD_DIM).transpose(2, 0, 3, 1)
    vt = jnp.concatenate([vt, jnp.ones((KV, ntile, ONES_ROWS, LANES), BF16)], axis=2)
    nbt = nb.reshape(KV, G, TQ, 2 * TQ).transpose(0, 3, 1, 2).reshape(KV, 2 * TQ, R)
    vmem = (2 * (2 * TQ * G * HEAD_DIM * 2 + 2 * Lp * HEAD_DIM * 2 + ntile * TQ * LANES * 2
                 + 2 * TQ * R * 4)
            + R * LANES * 6 + 2 * R * TK * 4 + 5 * R * TK * 4)
    q_map = lambda h, c: (c, h)
    return pl.pallas_call(
        _attn_prompt_kernel,
        grid=(KV, steps),
        in_specs=[pl.BlockSpec((TQ, G * HEAD_DIM), q_map),
                  pl.BlockSpec((Lp, HEAD_DIM), lambda h, c: (0, h)),
                  pl.BlockSpec((None, ntile, VR, LANES), lambda h, c: (h, 0, 0, 0)),
                  pl.BlockSpec((1, ntile, LANES, TQ), lambda h, c: (c, 0, 0, 0)),
                  pl.BlockSpec((None, 2 * TQ, R), lambda h, c: (h, 0, 0))],
        out_specs=pl.BlockSpec((TQ, G * HEAD_DIM), q_map),
        out_shape=jax.ShapeDtypeStruct((T, HD), BF16),
        scratch_shapes=[pltpu.VMEM((R, HEAD_DIM), BF16),
                        pltpu.VMEM((1, R), F32),
                        pltpu.VMEM((VR, R), F32),
                        pltpu.VMEM((TK, R), F32),
                        pltpu.VMEM((TK, R), F32),
                        pltpu.VMEM((TK, R), BF16)],
        compiler_params=_params(("parallel", "arbitrary"), vmem),
        name="dsa_attention_prompt",
    )(q, k, vt, mask, nbt)


def _t5_bucket_np(rel):
    nb = N_BUCKETS // 2
    max_exact = nb // 2
    side = np.where(rel > 0, nb, 0)
    n = np.abs(rel)
    nf = np.maximum(n, 1).astype(np.float32)
    large = max_exact + (np.log(nf / np.float32(max_exact))
                         / np.float32(math.log(REL_MAX_DIST / max_exact))
                         * np.float32(nb - max_exact)).astype(np.int32)
    large = np.minimum(large, nb - 1)
    return side + np.where(n < max_exact, n, large)


def _near_bias(rel_bias, rel, far_rel):
    far = int(_t5_bucket_np(np.array(far_rel)))
    assert far_rel < 0 and far == int(_t5_bucket_np(np.array(-10 ** 6)))
    rb = rel_bias.astype(F32) * LOG2E
    tab = rb[_t5_bucket_np(rel)] - rb[far][None, None, None, :]
    return tab.transpose(0, 3, 1, 2)


def _near_bias_stream(rel_bias, TQ, TK, phases):
    t = np.arange(TQ)[:, None]
    j = np.arange(2 * TK)[None, :]
    rel = np.stack([j - TK - ph + TQ - t for ph in phases])
    return _near_bias(rel_bias, rel, TQ - TK - 2)


def _near_bias_prompt(rel_bias):
    TQ = Q_STEP
    rel = np.arange(2 * TQ)[None, :] - TQ - np.arange(TQ)[:, None]
    return _near_bias(rel_bias, rel[None], -TQ - 1)[0]


def _conv_kernel(x_ref, halo_ref, buf_ref, w_ref, b_ref, xc_ref, xcb_ref, ext_ref, *, tt):
    first = pl.program_id(1) == 0
    ext_ref[0:SUBLANES, :] = jnp.where(first, buf_ref[...], halo_ref[...])
    ext_ref[SUBLANES:, :] = x_ref[...]
    y = jnp.broadcast_to(b_ref[...], x_ref.shape)
    for j in range(CONV_W):
        off = SUBLANES - (CONV_W - 1) + j
        y = y + ext_ref[off:off + tt, :] * w_ref[j:j + 1, :]
    xc_ref[...] = y
    xcb_ref[...] = y.astype(BF16)


def _conv(x, buf8, w, b, *, row0, B, T, tt):
    C = x.shape[1]
    assert row0 % tt == 0 and T % tt == 0 and tt % SUBLANES == 0
    nt = T // tt
    hb = tt // SUBLANES
    blk0 = row0 // tt
    main_map = lambda b, i: (blk0 + b * nt + i, 0)
    halo_map = lambda b, i: (jnp.maximum((blk0 + b * nt + i) * hb - 1, 0), 0)
    out_map = lambda b, i: (b * nt + i, 0)
    return pl.pallas_call(
        functools.partial(_conv_kernel, tt=tt),
        grid=(B, nt),
        in_specs=[pl.BlockSpec((tt, C), main_map),
                  pl.BlockSpec((SUBLANES, C), halo_map),
                  pl.BlockSpec((None, SUBLANES, C), lambda b, i: (b, 0, 0)),
                  pl.BlockSpec((CONV_W, C), lambda b, i: (0, 0)),
                  pl.BlockSpec((1, C), lambda b, i: (0, 0))],
        out_specs=[pl.BlockSpec((tt, C), out_map), pl.BlockSpec((tt, C), out_map)],
        out_shape=[jax.ShapeDtypeStruct((B * T, C), F32),
                   jax.ShapeDtypeStruct((B * T, C), BF16)],
        scratch_shapes=[pltpu.VMEM((tt + SUBLANES, C), F32)],
        compiler_params=_params(("parallel", "parallel"), 2 * tt * C * 10 + tt * C * 12),
        name="causal_conv",
    )(x, x, buf8, w, b.reshape(1, C))


def _scan_kernel(a_ref, u_ref, g_ref, h0_ref, y_ref, hlast_ref, h_ref, hs_ref, *, tt):
    i = pl.program_id(1)

    @pl.when(i == 0)
    def _():
        h_ref[...] = h0_ref[...]

    def step(t, h):
        h = a_ref[pl.ds(t, 1), :] * h + u_ref[pl.ds(t, 1), :]
        hs_ref[pl.ds(t, 1), :] = h
        return h

    h = lax.fori_loop(0, tt, step, h_ref[...])
    h_ref[...] = h
    hlast_ref[...] = h
    y_ref[...] = (hs_ref[...] * jax.nn.gelu(g_ref[...])).astype(y_ref.dtype)


def _scan(a, u, g, h0, *, row0, B, T, tt):
    C = a.shape[1]
    assert row0 % tt == 0 and T % tt == 0
    nt = T // tt
    blk0 = row0 // tt
    blk = pl.BlockSpec((tt, C), lambda b, i: (b * nt + i, 0))
    g_blk = pl.BlockSpec((tt, C), lambda b, i: (blk0 + b * nt + i, 0))
    vec = pl.BlockSpec((None, 1, C), lambda b, i: (b, 0, 0))
    return pl.pallas_call(
        functools.partial(_scan_kernel, tt=tt),
        grid=(B, nt),
        in_specs=[blk, blk, g_blk, vec],
        out_specs=[blk, vec],
        out_shape=[jax.ShapeDtypeStruct((B * T, C), BF16),
                   jax.ShapeDtypeStruct((B, 1, C), F32)],
        scratch_shapes=[pltpu.VMEM((1, C), F32), pltpu.VMEM((tt, C), F32)],
        compiler_params=_params(("parallel", "arbitrary"), 2 * tt * C * 14 + tt * C * 12),
        name="rglru_scan",
    )(a, u, g, h0)


GATE_TN = 256


def _gate_window(rb, C):
    raw = [((j * GATE_TN) // rb * rb) // LANES * LANES for j in range(C // GATE_TN)]
    ends = [((j * GATE_TN + GATE_TN - 1) // rb + 1) * rb for j in range(C // GATE_TN)]
    kw = -(-max(e - s for s, e in zip(raw, ends)) // LANES) * LANES
    kw = min(kw, C)
    starts = [min(s, C - kw) for s in raw]
    assert all(s + kw >= e for s, e in zip(starts, ends))
    return starts, kw


def _blockdiag_tiles(w):
    nblk, rb, _ = w.shape
    C = nblk * rb
    starts, kw = _gate_window(rb, C)
    dense = jax.scipy.linalg.block_diag(*[w[i] for i in range(nblk)])
    tiles = [dense[s:s + kw, j * GATE_TN:(j + 1) * GATE_TN] for j, s in enumerate(starts)]
    return jnp.stack(tiles).astype(BF16)


def _gates_kernel(x_ref, wa_ref, wx_ref, xc_ref, ba_ref, bx_ref, lam_ref, a_ref, u_ref, *, rb, kw):
    C = x_ref.shape[1]
    j = pl.program_id(1)
    start = pl.multiple_of(jnp.minimum(((j * GATE_TN) // rb * rb) // LANES * LANES, C - kw), LANES)
    vecs = [ba_ref[...], bx_ref[...], lam_ref[...]]
    sub = math.gcd(x_ref.shape[0], 256)
    for r in range(x_ref.shape[0] // sub):
        rows = slice(r * sub, (r + 1) * sub)
        x = x_ref[rows, pl.ds(start, kw)]
        accs = [jnp.dot(x, w[0], preferred_element_type=F32) for w in (wa_ref, wx_ref)]
        a, u = _ep_rglru_gates(accs, [xc_ref[rows, :]], vecs)
        a_ref[rows, :] = a
        u_ref[rows, :] = u


def _rglru_gates(xcb, xc, wa_t, wx_t, ba, bx, lam, *, rb, tm):
    rows, C = xc.shape
    nt, kw, _ = wa_t.shape
    tile = lambda: pl.BlockSpec((tm, GATE_TN), lambda i, j: (i, j))
    vec = lambda: pl.BlockSpec((1, GATE_TN), lambda i, j: (0, j))
    wspec = lambda: pl.BlockSpec((1, kw, GATE_TN), lambda i, j: (j, 0, 0))
    vmem = 2 * (tm * C * 2 + 2 * kw * GATE_TN * 2 + 3 * tm * GATE_TN * 4) + 8 * tm * GATE_TN * 4
    return pl.pallas_call(
        functools.partial(_gates_kernel, rb=rb, kw=kw),
        grid=(rows // tm, nt),
        in_specs=[pl.BlockSpec((tm, C), lambda i, j: (i, 0)), wspec(), wspec(), tile(),
                  vec(), vec(), vec()],
        out_specs=[tile(), tile()],
        out_shape=[jax.ShapeDtypeStruct((rows, C), F32)] * 2,
        compiler_params=_params(("parallel", "arbitrary"), vmem),
        name="rglru_gates",
    )(xcb, wa_t, wx_t, xc, ba.reshape(1, C), bx.reshape(1, C), lam.reshape(1, C))


def _pick(n, cands):
    for c in cands:
        if n % c == 0:
            return c
    raise ValueError(f"no tile for {n}")


def _layer(x, hist, p, rel_bias, dims):
    Tp, Bs, Ts, past = dims
    M, D = x.shape
    Ms = Bs * Ts
    k_past, v_past, ki_past, conv_buf, h0 = hist
    KV = k_past.shape[2]
    HQ = KV * GROUP * HEAD_DIM
    KVD = KV * HEAD_DIM
    DI = ki_past.shape[-1]
    C = p["conv_w"].shape[-1]
    n_in = p["w_in"].shape[1]
    HI = (n_in - HQ - 2 * KVD - DI - 2 * C - 2 * D) // (DI + 1)
    assert DI == LANES
    sizes = (HQ, KVD, KVD, HI * DI, DI, HI, C, C, D, D)
    offs = np.concatenate([[0], np.cumsum(sizes)])
    assert offs[-1] == n_in
    w_in = p["w_in"]

    def wslice(i, j=None):
        j = i if j is None else j
        return w_in[:, offs[i]:offs[j + 1]].astype(BF16)

    tm = _pick(M, (768, 512, 256, 128, 64))
    TK = KEY_TILE

    h = _rmsnorm(x, p["norm_mix"], tm)

    def proj(w, ep, dt, tn, vecs=(), name="in_proj"):
        return _matmul(h, [w], ep, [dt], tm=tm, tn=tn, vecs=vecs, name=name)[0]

    wide = (1024, 512, 256, 128)
    q = proj(wslice(0), _ep_headnorm(HEAD_DIM ** -0.5 * LOG2E), BF16, _pick(HQ, wide),
             vecs=[jnp.tile(p["q_norm"], HQ // HEAD_DIM)], name="in_proj_q")
    k = proj(wslice(1), _ep_headnorm(1.0), F32, _pick(KVD, wide),
             vecs=[jnp.tile(p["k_norm"], KV)], name="in_proj_k")
    v = proj(wslice(2), _ep_identity, F32, _pick(KVD, wide), name="in_proj_v")
    qi = proj(wslice(3), _ep_identity, BF16, _pick(HI * DI, wide), name="in_proj_qi")
    kw_w = jnp.pad(wslice(4, 5), ((0, 0), (0, 2 * LANES - DI - HI)))
    kiwi = proj(kw_w, _ep_identity, F32, 2 * LANES, name="in_proj_ki")
    ki, wi = kiwi[:, :DI], kiwi[:, DI:DI + HI]
    tn_c = _pick(C, (768, 384, 128))
    xr = proj(wslice(6), _ep_identity, F32, tn_c, name="in_proj_xr")
    gr = proj(wslice(7), _ep_identity, F32, tn_c, name="in_proj_gr")
    tn_d = _pick(D, (512, 256, 128))
    tn_w = _pick(D, wide)
    ga = proj(wslice(8), _ep_identity, F32, tn_w, name="in_proj_ga")
    gb = proj(wslice(9), _ep_identity, F32, tn_w, name="in_proj_gb")

    k_bf, v_bf, ki_bf = k.astype(BF16), v.astype(BF16), ki.astype(BF16)

    assert Tp % Q_STEP == 0
    back_p = -Tp % FAR_TILE
    n_sel_p = min(TOPK_MAX, Tp // 4)
    mask_p = _select_prompt(qi, wi, jnp.pad(ki_bf[:Tp], ((0, back_p), (0, 0))), n_sel_p, Tp)
    kv_pad = lambda a: jnp.pad(a[:Tp], ((FRONT_PAD, back_p), (0, 0)))
    o_p = _attention_prompt(q, kv_pad(k_bf), kv_pad(v_bf), mask_p, _near_bias_prompt(rel_bias))

    Ls = past + Ts
    nt_s = -(-Ls // TK)
    pad_s = nt_s * TK - Ls

    def with_cache(cache, new):
        new = new[Tp:].reshape(Bs, Ts, -1)
        parts = [cache.reshape(Bs, past, -1).astype(BF16), new]
        if pad_s:
            parts.append(jnp.zeros((Bs, pad_s, new.shape[-1]), BF16))
        return jnp.concatenate(parts, axis=1)

    n_sel_s = min(TOPK_MAX, Ls // 4)
    mask_s = _select(qi[Tp:], wi[Tp:], with_cache(ki_past, ki_bf), TQ=Ts, NT=nt_s,
                     lend_a=0, lend_b=Ls, n_sel=n_sel_s, batched=True)
    nb_s = _near_bias_stream(rel_bias, Ts, TK, [Ls - ((Ls - 1) // TK) * TK])
    o_s = _attention(q[Tp:], with_cache(k_past, k_bf), with_cache(v_past, v_bf), mask_s, nb_s,
                     TQ=Ts, lend_a=0, lend_b=Ls, batched=True)
    o_a = jnp.concatenate([o_p, o_s], axis=0)

    assert C % GATE_TN == 0
    wa_t = _blockdiag_tiles(p["rg_wa"])
    wx_t = _blockdiag_tiles(p["rg_wx"])

    def griffin(row0, B, T, buf, h_init):
        tt = _pick(math.gcd(T, row0) if row0 else T, (128, 64, 32, 16, 8))
        buf8 = jnp.pad(buf.astype(F32), ((0, 0), (SUBLANES - (CONV_W - 1), 0), (0, 0)))
        xc, xcb = _conv(xr, buf8, p["conv_w"], p["conv_b"], row0=row0, B=B, T=T, tt=tt)
        rows = B * T
        tmr = _pick(rows, (1024, 512, 256, 128, 64, 32))
        a, u = _rglru_gates(xcb, xc, wa_t, wx_t, p["rg_ba"], p["rg_bx"], p["rg_lambda"],
                            rb=p["rg_wa"].shape[1], tm=tmr)
        y, h_last = _scan(a, u, gr, h_init.reshape(B, 1, C), row0=row0, B=B, T=T, tt=tt)
        tail = xr[row0:row0 + rows].reshape(B, T, C)[:, -(CONV_W - 1):]
        conv_new = jnp.concatenate([buf.astype(F32), tail], axis=1)[:, -(CONV_W - 1):]
        return y, conv_new, h_last.reshape(B, C)

    y_p, conv_p, h_p = griffin(0, 1, Tp, jnp.zeros((1, CONV_W - 1, C), F32), jnp.zeros((1, C), F32))
    y_s, conv_s, h_s = griffin(Tp, Bs, Ts, conv_buf, h0)
    o_b = jnp.concatenate([y_p, y_s], axis=0)

    part = _matmul(o_a, [p["w_out_attn"].astype(BF16)], _ep_gate, [F32], tm=tm, tn=tn_w,
                   exts=[ga], name="out_attn")[0]
    merged = _matmul(o_b, [p["w_out_rg"].astype(BF16)], _ep_gate_add, [BF16],
                     tm=_pick(M, (512, 256, 128, 64)), tn=tn_w, exts=[gb, part], name="out_rg")[0]
    x1 = _matmul(merged, [p["w_o"].astype(BF16)], _ep_residual, [F32], tm=tm, tn=tn_w,
                 exts=[x], name="w_o")[0]
    hf = _rmsnorm(x1, p["norm_ffn"], tm)
    FF = p["ffn_w1"].shape[1]
    tn_f = _pick(FF, (512, 256, 128))
    act = _matmul(hf, [p["ffn_w1"].astype(BF16), p["ffn_w3"].astype(BF16)], _ep_swiglu, [BF16],
                  tm=tm, tn=tn_f, name="ffn_up")[0]
    x2 = _matmul(act, [p["ffn_w2"].astype(BF16)], _ep_residual, [F32],
                 tm=_pick(M, (384, 256, 128, 64)), tn=tn_d, exts=[x1], name="ffn_down")[0]

    new_p = (k[:Tp].reshape(1, Tp, KV, HEAD_DIM), v[:Tp].reshape(1, Tp, KV, HEAD_DIM),
             ki[:Tp].reshape(1, Tp, DI), conv_p, h_p)
    new_s = (k[Tp:].reshape(Bs, Ts, KV, HEAD_DIM), v[Tp:].reshape(Bs, Ts, KV, HEAD_DIM),
             ki[Tp:].reshape(Bs, Ts, DI), conv_s, h_s)
    return x2, new_p, new_s


def kernel(x_prompt, x_sample, cache_k, cache_v, cache_kidx, state_conv, state_rglru, norm_mix, w_in, q_norm, k_norm, rel_bias, conv_w, conv_b, rg_wa, rg_ba, rg_wx, rg_bx, rg_lambda, w_out_attn, w_out_rg, w_o, norm_ffn, ffn_w1, ffn_w3, ffn_w2):
    Bp, Tp, D = x_prompt.shape
    Bs, Ts, _ = x_sample.shape
    assert Bp == 1 and Tp % CHUNK == 0
    depth = w_in.shape[0]
    past = cache_k.shape[2]
    x = jnp.concatenate([x_prompt.reshape(Tp, D), x_sample.reshape(Bs * Ts, D)], axis=0)
    outs_p, outs_s = [], []
    for l in range(depth):
        p = dict(norm_mix=norm_mix[l], w_in=w_in[l], q_norm=q_norm[l], k_norm=k_norm[l],
                 conv_w=conv_w[l], conv_b=conv_b[l], rg_wa=rg_wa[l], rg_ba=rg_ba[l],
                 rg_wx=rg_wx[l], rg_bx=rg_bx[l], rg_lambda=rg_lambda[l],
                 w_out_attn=w_out_attn[l], w_out_rg=w_out_rg[l], w_o=w_o[l],
                 norm_ffn=norm_ffn[l], ffn_w1=ffn_w1[l], ffn_w3=ffn_w3[l], ffn_w2=ffn_w2[l])
        hist = (cache_k[l], cache_v[l], cache_kidx[l], state_conv[l], state_rglru[l])
        x, new_p, new_s = _layer(x, hist, p, rel_bias, (Tp, Bs, Ts, past))
        outs_p.append(new_p)
        outs_s.append(new_s)
    stack = lambda outs, i: jnp.stack([o[i] for o in outs])
    return (x[:Tp].reshape(1, Tp, D), x[Tp:].reshape(Bs, Ts, D),
            *[stack(outs_p, i) for i in range(5)],
            *[stack(outs_s, i) for i in range(5)])
```

```python
import functools
import math

import numpy as np
import jax
import jax.numpy as jnp
from jax import lax
from jax.experimental import pallas as pl
from jax.experimental.pallas import tpu as pltpu

F32 = jnp.float32
BF16 = jnp.bfloat16

CHUNK = 64
HEAD_DIM = 128
GROUP = 4
TOPK_MAX = 256
N_BUCKETS = 32
REL_MAX_DIST = 128
RG_C = 8.0
CONV_W = 4
EPS = 1e-6

LANES = 128
SUBLANES = 8
KEY_TILE = 256
Q_STEP = 2 * CHUNK
FAR_TILE = 512
FRONT_PAD = FAR_TILE
ONES_ROWS = 16
LOG2E = math.log2(math.e)
VMEM_CAP = 56 << 20
NEG = -1e30


def _params(sem, vmem_bytes):
    limit = min(max(int(vmem_bytes) + (6 << 20), 24 << 20), VMEM_CAP)
    return pltpu.CompilerParams(dimension_semantics=sem, vmem_limit_bytes=limit)


def _rmsnorm_kernel(x_ref, g_ref, o_ref):
    x = x_ref[...]
    ms = jnp.mean(x * x, axis=-1, keepdims=True)
    o_ref[...] = (x * lax.rsqrt(ms + EPS) * g_ref[...]).astype(o_ref.dtype)


def _rmsnorm(x, g, tm):
    M, D = x.shape
    tm = min(tm, 256)
    return pl.pallas_call(
        _rmsnorm_kernel,
        grid=(M // tm,),
        in_specs=[pl.BlockSpec((tm, D), lambda i: (i, 0)),
                  pl.BlockSpec((1, D), lambda i: (0, 0))],
        out_specs=pl.BlockSpec((tm, D), lambda i: (i, 0)),
        out_shape=jax.ShapeDtypeStruct((M, D), BF16),
        compiler_params=_params(("parallel",), 2 * tm * D * 6 + 2 * tm * D * 4),
        name="rmsnorm",
    )(x, g.reshape(1, D))


def _mm_kernel(*refs, n_w, n_ext, n_vec, n_out, nk, epilogue):
    x_ref = refs[0]
    w_refs = refs[1:1 + n_w]
    p = 1 + n_w
    ext_refs = refs[p:p + n_ext]
    p += n_ext
    vec_refs = refs[p:p + n_vec]
    p += n_vec
    out_refs = refs[p:p + n_out]
    acc_refs = refs[p + n_out:]

    def finish(accs):
        res = epilogue(accs, [e[...] for e in ext_refs], [v[...] for v in vec_refs])
        for o, r in zip(out_refs, res):
            o[...] = r.astype(o.dtype)

    if nk == 1:
        finish([jnp.dot(x_ref[...], w[...], preferred_element_type=F32) for w in w_refs])
        return

    k = pl.program_id(2)

    @pl.when(k == 0)
    def _():
        for a in acc_refs:
            a[...] = jnp.zeros_like(a)

    for a, w in zip(acc_refs, w_refs):
        a[...] += jnp.dot(x_ref[...], w[...], preferred_element_type=F32)

    @pl.when(k == nk - 1)
    def _():
        finish([a[...] for a in acc_refs])


def _matmul(x, ws, epilogue, out_dtypes, *, tm, tn, tk=None, exts=(), vecs=(), name="matmul",
            single_buffer_w=False):
    M, K = x.shape
    n = ws[0].shape[1]
    tk = K if tk is None else tk
    nk = K // tk
    assert M % tm == 0 and n % tn == 0 and K % tk == 0
    x_map = lambda j, i, k: (i, k)
    w_map = lambda j, i, k: (k, j)
    mn_map = lambda j, i, k: (i, j)
    in_specs = [pl.BlockSpec((tm, tk), x_map)]
    w_mode = dict(pipeline_mode=pl.Buffered(1)) if single_buffer_w and nk == 1 else {}
    in_specs += [pl.BlockSpec((tk, tn), w_map, **w_mode) for _ in ws]
    in_specs += [pl.BlockSpec((tm, tn), mn_map) for _ in exts]
    in_specs += [pl.BlockSpec((1, tn), lambda j, i, k: (0, j)) for _ in vecs]
    out_specs = [pl.BlockSpec((tm, tn), mn_map) for _ in out_dtypes]
    out_shape = [jax.ShapeDtypeStruct((M, n), dt) for dt in out_dtypes]
    scratch = [pltpu.VMEM((tm, tn), F32) for _ in ws] if nk > 1 else []
    vmem = (2 * (tm * tk * 2 + len(ws) * tk * tn * 2 + (len(exts) + len(out_dtypes)) * tm * tn * 4)
            + 3 * len(ws) * tm * tn * 4)
    kern = functools.partial(_mm_kernel, n_w=len(ws), n_ext=len(exts), n_vec=len(vecs),
                             n_out=len(out_dtypes), nk=nk, epilogue=epilogue)
    outs = pl.pallas_call(
        kern,
        grid=(n // tn, M // tm, nk),
        in_specs=in_specs,
        out_specs=out_specs,
        out_shape=out_shape,
        scratch_shapes=scratch,
        compiler_params=_params(("parallel", "parallel", "arbitrary"), vmem),
        name=name,
    )(x, *ws, *exts, *[v.reshape(1, -1) for v in vecs])
    return outs


def _ep_identity(accs, exts, vecs):
    return (accs[0],)


def _ep_headnorm(scale):
    def ep(accs, exts, vecs):
        a, g = accs[0], vecs[0]
        outs = []
        for j in range(a.shape[1] // HEAD_DIM):
            aj = a[:, j * HEAD_DIM:(j + 1) * HEAD_DIM]
            ms = jnp.mean(aj * aj, axis=-1, keepdims=True)
            outs.append(aj * lax.rsqrt(ms + EPS) * g[:, j * HEAD_DIM:(j + 1) * HEAD_DIM])
        y = jnp.concatenate(outs, axis=1)
        return (y * scale if scale != 1.0 else y,)
    return ep


def _ep_gate(accs, exts, vecs):
    return (jax.nn.sigmoid(exts[0]) * accs[0],)


def _ep_gate_add(accs, exts, vecs):
    return (exts[1] + jax.nn.sigmoid(exts[0]) * accs[0],)


def _ep_residual(accs, exts, vecs):
    return (exts[0] + accs[0],)


def _ep_swiglu(accs, exts, vecs):
    return (jax.nn.silu(accs[0]) * accs[1],)


def _softplus(x):
    return jnp.maximum(x, 0.0) + jnp.log1p(jnp.exp(-jnp.abs(x)))


def _ep_rglru_gates(accs, exts, vecs):
    xc = exts[0]
    r = jax.nn.sigmoid(accs[0] + vecs[0])
    i = jax.nn.sigmoid(accs[1] + vecs[1])
    log_a = -RG_C * r * _softplus(-vecs[2])
    a = jnp.exp(log_a)
    u = jnp.sqrt(1.0 - a * a) * (i * xc)
    return (a, u)


INT_MAX = 2 ** 31 - 1


def _open_rows(lo, hi):
    return jnp.max(jnp.where(lo < hi, 1.0, 0.0))


def _to_key(s):
    b = pltpu.bitcast(s, jnp.int32)
    return b ^ ((b >> 31) & jnp.int32(0x7FFFFFFF))


def _from_key(k):
    return pltpu.bitcast(k ^ ((k >> 31) & jnp.int32(0x7FFFFFFF)), F32)


def _kth_largest_key(count_ge, lo, hi, ksel, n_valid):
    def body(st):
        lo, hi, c_lo, _, it = st
        active = lo < hi
        mid_k = (lo >> 1) + (hi >> 1) + ((lo | hi) & 1)
        mid_f = _to_key(0.5 * _from_key(lo) + 0.5 * _from_key(hi))
        mid_f = jnp.minimum(jnp.maximum(mid_f, lo + 1), hi)
        mid = jnp.where((it & 1) == 0, mid_f, mid_k)
        c = count_ge(mid)
        ge = c >= ksel
        lo_n = jnp.where(ge, mid, lo)
        c_n = jnp.where(ge, c, c_lo)
        hi_n = jnp.where(c == ksel, mid, jnp.where(ge, hi, mid - 1))
        lo = jnp.where(active, lo_n, lo)
        c_lo = jnp.where(active, c_n, c_lo)
        hi = jnp.where(active, hi_n, hi)
        return lo, hi, c_lo, _open_rows(lo, hi), it + 1

    tau, _, c_tau, _, _ = lax.while_loop(lambda st: st[3] > 0.5, body,
                                         (lo, hi, n_valid, _open_rows(lo, hi), jnp.int32(0)))
    return tau, c_tau


def _tie_cut(count_tie_le, need, n_keys):
    def body(st):
        lo, hi, _ = st
        active = lo < hi
        mid = (lo + hi) >> 1
        ok = count_tie_le(mid) >= need
        hi = jnp.where(active, jnp.where(ok, mid, hi), hi)
        lo = jnp.where(active, jnp.where(ok, lo, mid + 1), lo)
        return lo, hi, _open_rows(lo, hi)

    lo = jnp.zeros(need.shape, jnp.int32)
    hi = jnp.full(need.shape, n_keys - 1, jnp.int32)
    cut, _, _ = lax.while_loop(lambda st: st[2] > 0.5, body, (lo, hi, _open_rows(lo, hi)))
    return cut


def _threshold_and_cut(key_ref, cut_ref, nt, lo, hi, ksel, n_valid):
    _, TQ, TK = key_ref.shape
    nl = TK // LANES
    lane = lax.broadcasted_iota(jnp.int32, (TQ, LANES), 1)

    def count(indicator):
        def body(j, c):
            kk = key_ref[j]
            for l in range(nl):
                c = c + indicator(kk[:, l * LANES:(l + 1) * LANES], j * TK + l * LANES)
            return c
        c = lax.fori_loop(0, nt, body, jnp.zeros((TQ, LANES), F32))
        return jnp.broadcast_to(jnp.sum(c, axis=1, keepdims=True), (TQ, LANES))

    tau, c_tau = _kth_largest_key(
        lambda mid: count(lambda kk, base: jnp.where(kk >= mid, 1.0, 0.0)), lo, hi, ksel, n_valid)
    cut_ref[...] = jnp.full((TQ, LANES), INT_MAX, jnp.int32)

    tied = jnp.max(jnp.where(c_tau > ksel, 1.0, 0.0)) > 0.5

    @pl.when(tied)
    def _():
        need = ksel - count(lambda kk, base: jnp.where(kk > tau, 1.0, 0.0))
        cut_ref[...] = _tie_cut(
            lambda mid: count(lambda kk, base: jnp.where(
                kk == tau, jnp.where(lane + base <= mid, 1.0, 0.0), 0.0)),
            need, nt * TK)

    return tau, cut_ref[...], tied


def _selected(kk, base, tau, cut):
    lane = lax.broadcasted_iota(jnp.int32, kk.shape, 1)
    tie = jnp.where(kk == tau, jnp.where(lane + base <= cut, 1, 0), 0)
    return jnp.where(kk > tau, 1, tie)


def _write_selection(write_tile, nt, tied):
    @pl.when(tied)
    def _():
        lax.fori_loop(0, nt, lambda j, _: write_tile(j, True) or 0, 0)

    @pl.when(jnp.logical_not(tied))
    def _():
        lax.fori_loop(0, nt, lambda j, _: write_tile(j, False) or 0, 0)
def _select_kernel(qi_ref, wi_ref, ki_ref, mask_ref, q2_ref, w2_ref, key_ref, cut_ref, *,
                   TQ, TK, NT, HI, lend_a, lend_b, n_sel, wscale):
    nl = TK // LANES
    lend = lend_a * pl.program_id(0) + lend_b
    nt = (lend + TK - 1) // TK
    ksel = jnp.minimum(n_sel, lend).astype(F32)

    wi = wi_ref[...] * wscale
    for h in range(HI):
        q2_ref[h * TQ:(h + 1) * TQ, :] = qi_ref[:, h * LANES:(h + 1) * LANES]
        w2_ref[h * TQ:(h + 1) * TQ, :] = jnp.broadcast_to(wi[:, h:h + 1], (TQ, LANES))

    lane = lax.broadcasted_iota(jnp.int32, (TQ, TK), 1)
    int_min = jnp.int32(-2 ** 31)

    def score_tile(j, carry):
        smin, smax = carry
        start = pl.multiple_of(j * TK, TK)
        kt = ki_ref[pl.ds(start, TK), :]
        s = lax.dot_general(q2_ref[...], kt, (((1,), (1,)), ((), ())),
                            preferred_element_type=F32)
        cols = []
        for l in range(nl):
            acc = jnp.zeros((TQ, LANES), F32)
            for h in range(HI):
                acc = acc + (jnp.maximum(s[h * TQ:(h + 1) * TQ, l * LANES:(l + 1) * LANES], 0.0)
                             * w2_ref[h * TQ:(h + 1) * TQ, :])
            cols.append(acc)
        sc = jnp.concatenate(cols, axis=1)
        valid = (lane + j * TK) < lend
        key_ref[j] = jnp.where(valid, _to_key(sc), int_min)
        lo_s = jnp.where(valid, sc, jnp.inf)
        hi_s = jnp.where(valid, sc, -jnp.inf)
        for l in range(nl):
            smin = jnp.minimum(smin, lo_s[:, l * LANES:(l + 1) * LANES])
            smax = jnp.maximum(smax, hi_s[:, l * LANES:(l + 1) * LANES])
        return smin, smax

    smin, smax = lax.fori_loop(
        0, nt, score_tile,
        (jnp.full((TQ, LANES), jnp.inf, F32), jnp.full((TQ, LANES), -jnp.inf, F32)))
    lo = _to_key(jnp.broadcast_to(jnp.min(smin, axis=1, keepdims=True), (TQ, LANES)))
    hi = _to_key(jnp.broadcast_to(jnp.max(smax, axis=1, keepdims=True), (TQ, LANES)))

    n_valid = jnp.full((TQ, LANES), lend, jnp.int32).astype(F32)
    tau, cut, tied = _threshold_and_cut(key_ref, cut_ref, nt, lo, hi,
                                        jnp.full((TQ, LANES), ksel, F32), n_valid)

    def write(j, exact_ties):
        kk = key_ref[j]
        sel = []
        for l in range(nl):
            slab = kk[:, l * LANES:(l + 1) * LANES]
            sel.append(_selected(slab, j * TK + l * LANES, tau, cut) if exact_ties
                       else jnp.where(slab >= tau, 1, 0))
        mask_ref[0, j] = jnp.concatenate(sel, axis=1).astype(jnp.int8)

    _write_selection(write, nt, tied)

    def clear(j, _):
        mask_ref[0, j] = jnp.zeros((TQ, TK), jnp.int8)
        return 0

    lax.fori_loop(nt, NT, clear, 0)


def _select(qi, wi, ki, *, TQ, NT, lend_a, lend_b, n_sel, batched):
    TK = KEY_TILE
    Mq = qi.shape[0]
    HI = wi.shape[1]
    steps = Mq // TQ
    if batched:
        ki_spec = pl.BlockSpec((None, NT * TK, LANES), lambda i: (i, 0, 0))
    else:
        ki_spec = pl.BlockSpec((NT * TK, LANES), lambda i: (0, 0))
    kern = functools.partial(_select_kernel, TQ=TQ, TK=TK, NT=NT, HI=HI, lend_a=lend_a,
                             lend_b=lend_b, n_sel=n_sel,
                             wscale=float(HI ** -0.5 * LANES ** -0.5))
    vmem = (2 * (TQ * HI * LANES * 2 + NT * TK * LANES * 2 + NT * TQ * TK)
            + HI * TQ * LANES * 6 + NT * TQ * TK * 4 + 3 * HI * TQ * TK * 4)
    return pl.pallas_call(
        kern,
        grid=(steps,),
        in_specs=[pl.BlockSpec((TQ, HI * LANES), lambda i: (i, 0)),
                  pl.BlockSpec((TQ, HI), lambda i: (i, 0)),
                  ki_spec],
        out_specs=pl.BlockSpec((1, NT, TQ, TK), lambda i: (i, 0, 0, 0)),
        out_shape=jax.ShapeDtypeStruct((steps, NT, TQ, TK), jnp.int8),
        scratch_shapes=[pltpu.VMEM((HI * TQ, LANES), BF16),
                        pltpu.VMEM((HI * TQ, LANES), F32),
                        pltpu.VMEM((NT, TQ, TK), jnp.int32),
                        pltpu.VMEM((TQ, LANES), jnp.int32)],
        compiler_params=_params(("parallel",), vmem),
        name="dsa_select",
    )(qi, wi, ki)


def _select_prompt_kernel(qi_ref, wi_ref, ki_ref, mask_ref, q2_ref, w2_ref, key_ref, acc_ref,
                          cut_ref, *,
                          HI, HG, NT, n_sel, wscale):
    TQ, TK = Q_STEP, FAR_TILE
    nl = TK // LANES
    pad_tiles = FRONT_PAD // LANES
    RB = 64
    c2 = pl.program_id(0)
    row = lax.broadcasted_iota(jnp.int32, (TQ, LANES), 0)
    lend = jnp.where(row < CHUNK, c2 * TQ + CHUNK, c2 * TQ + TQ)
    nt = (c2 * TQ + TQ + TK - 1) // TK
    ksel = jnp.minimum(n_sel, lend).astype(F32)

    wi = wi_ref[...] * wscale
    for h in range(HI):
        q2_ref[h * TQ:(h + 1) * TQ, :] = qi_ref[:, h * LANES:(h + 1) * LANES]
        w2_ref[h * TQ:(h + 1) * TQ, :] = jnp.broadcast_to(wi[:, h:h + 1], (TQ, LANES))

    lane = lax.broadcasted_iota(jnp.int32, (TQ, TK), 1)
    lend_t = jnp.concatenate([lend] * nl, axis=1)
    int_min = jnp.int32(-2 ** 31)

    def score_tile(j, carry):
        smin, smax = carry
        start = pl.multiple_of(j * TK, TK)
        kt = ki_ref[pl.ds(start, TK), :]
        for hg in range(HI // HG):
            s = lax.dot_general(q2_ref[hg * HG * TQ:(hg + 1) * HG * TQ, :], kt,
                                (((1,), (1,)), ((), ())), preferred_element_type=F32)
            for r in range(TQ // RB):
                cs = [None] * nl
                for h in range(HG):
                    r0 = h * TQ + r * RB
                    w = w2_ref[(hg * HG) * TQ + r0:(hg * HG) * TQ + r0 + RB, :]
                    for l in range(nl):
                        term = jnp.maximum(s[r0:r0 + RB, l * LANES:(l + 1) * LANES], 0.0) * w
                        cs[l] = term if cs[l] is None else cs[l] + term
                for l in range(nl):
                    if hg == 0:
                        acc_ref[r * RB:(r + 1) * RB, l * LANES:(l + 1) * LANES] = cs[l]
                    else:
                        acc_ref[r * RB:(r + 1) * RB, l * LANES:(l + 1) * LANES] += cs[l]
        sc = acc_ref[...]
        valid = (lane + j * TK) < lend_t
        key_ref[j] = jnp.where(valid, _to_key(sc), int_min)
        lo_s = jnp.where(valid, sc, jnp.inf)
        hi_s = jnp.where(valid, sc, -jnp.inf)
        for l in range(nl):
            smin = jnp.minimum(smin, lo_s[:, l * LANES:(l + 1) * LANES])
            smax = jnp.maximum(smax, hi_s[:, l * LANES:(l + 1) * LANES])
        return smin, smax

    smin, smax = lax.fori_loop(
        0, nt, score_tile,
        (jnp.full((TQ, LANES), jnp.inf, F32), jnp.full((TQ, LANES), -jnp.inf, F32)))
    lo = _to_key(jnp.broadcast_to(jnp.min(smin, axis=1, keepdims=True), (TQ, LANES)))
    hi = _to_key(jnp.broadcast_to(jnp.max(smax, axis=1, keepdims=True), (TQ, LANES)))

    tau, cut, tied = _threshold_and_cut(key_ref, cut_ref, nt, lo, hi, ksel, lend.astype(F32))

    neg_tile = jnp.full((TQ, LANES), NEG, BF16)
    for i in range(pad_tiles):
        mask_ref[0, i] = neg_tile

    def write(j, exact_ties):
        kk = key_ref[j]
        for i in range(nl):
            slab = kk[:, i * LANES:(i + 1) * LANES]
            keep = (_selected(slab, j * TK + i * LANES, tau, cut) > 0) if exact_ties else slab >= tau
            mask_ref[0, pad_tiles + nl * j + i] = jnp.where(keep, 0.0, NEG).T.astype(BF16)

    _write_selection(write, nt, tied)

    def clear(j, _):
        for i in range(nl):
            mask_ref[0, pad_tiles + nl * j + i] = neg_tile
        return 0

    lax.fori_loop(nt, NT, clear, 0)


def _select_prompt(qi, wi, ki, n_sel, T):
    TQ, TK = Q_STEP, FAR_TILE
    HI = wi.shape[1]
    HG = 4 if HI % 4 == 0 else 1
    NT = ki.shape[0] // TK
    steps = T // TQ
    ntile = (FRONT_PAD + NT * TK) // LANES
    kern = functools.partial(_select_prompt_kernel, HI=HI, HG=HG, NT=NT, n_sel=n_sel,
                             wscale=float(HI ** -0.5 * LANES ** -0.5))
    vmem = (2 * (TQ * HI * LANES * 2 + NT * TK * LANES * 2 + ntile * TQ * LANES * 2)
            + HI * TQ * LANES * 6 + NT * TQ * TK * 4 + TQ * TK * 4 + 4 * HG * TQ * TK * 4)
    return pl.pallas_call(
        kern,
        grid=(steps,),
        in_specs=[pl.BlockSpec((TQ, HI * LANES), lambda i: (i, 0)),
                  pl.BlockSpec((TQ, HI), lambda i: (i, 0)),
                  pl.BlockSpec((NT * TK, LANES), lambda i: (0, 0))],
        out_specs=pl.BlockSpec((1, ntile, TQ, LANES), lambda i: (i, 0, 0, 0)),
        out_shape=jax.ShapeDtypeStruct((steps, ntile, TQ, LANES), BF16),
        scratch_shapes=[pltpu.VMEM((HI * TQ, LANES), BF16),
                        pltpu.VMEM((HI * TQ, LANES), F32),
                        pltpu.VMEM((NT, TQ, TK), jnp.int32),
                        pltpu.VMEM((TQ, TK), F32),
                        pltpu.VMEM((TQ, LANES), jnp.int32)],
        compiler_params=_params(("parallel",), vmem),
        name="dsa_select_prompt",
    )(qi, wi, ki)


def _attn_kernel(q_ref, k_ref, v_ref, mask_ref, nb_ref, o_ref, q2_ref, m_ref, l_ref, acc_ref, *,
                 TQ, TK, G, lend_a, lend_b, step_axis):
    nl = TK // LANES
    R = G * TQ
    lend = lend_a * pl.program_id(step_axis) + lend_b
    jl = (lend - 1) // TK

    for g in range(G):
        q2_ref[g * TQ:(g + 1) * TQ, :] = q_ref[:, g * HEAD_DIM:(g + 1) * HEAD_DIM]
    m_ref[...] = jnp.full((R, LANES), NEG, F32)
    l_ref[...] = jnp.zeros((R, LANES), F32)
    acc_ref[...] = jnp.zeros((R, HEAD_DIM), F32)

    def tile(j, half):
        start = pl.multiple_of(j * TK, TK)
        kt = k_ref[pl.ds(start, TK), :]
        vt = v_ref[pl.ds(start, TK), :]
        s = lax.dot_general(q2_ref[...], kt, (((1,), (1,)), ((), ())),
                            preferred_element_type=F32)
        madd = jnp.where(mask_ref[0, j].astype(jnp.int32) != 0, 0.0, NEG)
        rows = []
        for g in range(G):
            sg = s[g * TQ:(g + 1) * TQ, :]
            if half is not None:
                sg = sg + nb_ref[0, g, :, half * TK:(half + 1) * TK]
            rows.append(sg + madd)
        s = jnp.concatenate(rows, axis=0)
        m_old = m_ref[...]
        m_cur = s[:, :LANES]
        for l in range(1, nl):
            m_cur = jnp.maximum(m_cur, s[:, l * LANES:(l + 1) * LANES])
        m_new = jnp.maximum(m_old, jnp.broadcast_to(jnp.max(m_cur, axis=1, keepdims=True), (R, LANES)))
        alpha = jnp.exp2(m_old - m_new)
        p = jnp.exp2(s - jnp.concatenate([m_new] * nl, axis=1))
        psum = p[:, :LANES]
        for l in range(1, nl):
            psum = psum + p[:, l * LANES:(l + 1) * LANES]
        l_ref[...] = alpha * l_ref[...] + jnp.broadcast_to(
            jnp.sum(psum, axis=1, keepdims=True), (R, LANES))
        acc_ref[...] = alpha * acc_ref[...] + jnp.dot(
            p.astype(BF16), vt, preferred_element_type=F32)
        m_ref[...] = m_new

    def far(j, _):
        tile(j, None)
        return 0

    lax.fori_loop(0, jnp.maximum(jl - 1, 0), far, 0)

    @pl.when(jl >= 1)
    def _():
        tile(jl - 1, 0)

    tile(jl, 1)

    o = acc_ref[...] / l_ref[...]
    for g in range(G):
        o_ref[:, g * HEAD_DIM:(g + 1) * HEAD_DIM] = o[g * TQ:(g + 1) * TQ, :].astype(o_ref.dtype)


def _attention(q, k, v, mask, nb, *, TQ, lend_a, lend_b, batched):
    TK = KEY_TILE
    G = GROUP
    Mq, HD = q.shape
    KV = HD // (G * HEAD_DIM)
    steps = Mq // TQ
    NT = mask.shape[1]
    L = k.shape[-2]
    P = nb.shape[0]
    if batched:
        grid = (steps, KV)
        q_map = lambda b, h: (b, h)
        kv_spec = pl.BlockSpec((None, L, HEAD_DIM), lambda b, h: (b, 0, h))
        mask_map = lambda b, h: (b, 0, 0, 0)
        nb_map = lambda b, h: (0, h, 0, 0)
        step_axis = 0
    else:
        grid = (KV, steps)
        q_map = lambda h, c: (c, h)
        kv_spec = pl.BlockSpec((L, HEAD_DIM), lambda h, c: (0, h))
        mask_map = lambda h, c: (c, 0, 0, 0)
        nb_map = lambda h, c: (c % P, h, 0, 0)
        step_axis = 1
    kern = functools.partial(_attn_kernel, TQ=TQ, TK=TK, G=G, lend_a=lend_a, lend_b=lend_b,
                             step_axis=step_axis)
    R = G * TQ
    vmem = (2 * (2 * TQ * G * HEAD_DIM * 2 + 2 * L * HEAD_DIM * 2 + NT * TQ * TK + G * TQ * 2 * TK * 4)
            + R * LANES * 14 + 6 * R * TK * 4)
    return pl.pallas_call(
        kern,
        grid=grid,
        in_specs=[pl.BlockSpec((TQ, G * HEAD_DIM), q_map),
                  kv_spec, kv_spec,
                  pl.BlockSpec((1, NT, TQ, TK), mask_map),
                  pl.BlockSpec((1, G, TQ, 2 * TK), nb_map)],
        out_specs=pl.BlockSpec((TQ, G * HEAD_DIM), q_map),
        out_shape=jax.ShapeDtypeStruct((Mq, HD), BF16),
        scratch_shapes=[pltpu.VMEM((R, HEAD_DIM), BF16),
                        pltpu.VMEM((R, LANES), F32),
                        pltpu.VMEM((R, LANES), F32),
                        pltpu.VMEM((R, HEAD_DIM), F32)],
        compiler_params=_params(("parallel", "arbitrary"), vmem),
        name="dsa_attention",
    )(q, k, v, mask, nb)


def _attn_prompt_kernel(q_ref, k_ref, v_ref, mask_ref, nb_ref, o_ref,
                        q2_ref, m_ref, acc_ref, sa_ref, sb_ref, p_ref):
    TQ, TK, G = Q_STEP, FAR_TILE, GROUP
    R = G * TQ
    NW = 2 * TQ
    c2 = pl.program_id(1)
    far_len = TQ * (c2 - 1)
    nfar = jnp.maximum((far_len + TK - 1) // TK, 0)
    nt_dims = (((1,), (1,)), ((), ()))

    for g in range(G):
        q2_ref[g * TQ:(g + 1) * TQ, :] = q_ref[:, g * HEAD_DIM:(g + 1) * HEAD_DIM]
    m_ref[...] = jnp.full((1, R), NEG, F32)
    acc_ref[...] = jnp.zeros(acc_ref.shape, F32)

    def far_start(j):
        return pl.multiple_of(jnp.maximum(far_len - TK * j, 0), LANES)

    def logits(start, width):
        return lax.dot_general(k_ref[pl.ds(start, width), :], q2_ref[...], nt_dims,
                               preferred_element_type=F32)

    def fold8(x, op):
        y = x[:SUBLANES]
        for i in range(1, x.shape[0] // SUBLANES):
            y = op(y, x[i * SUBLANES:(i + 1) * SUBLANES])
        return y

    def update(s_ref, start, width, biased):
        t0 = start // LANES
        CH = 32
        mx = jnp.full((SUBLANES, R), NEG, F32)
        for r in range(width // CH):
            rows = slice(r * CH, (r + 1) * CH)
            off = (r * CH) % LANES
            madd = mask_ref[0, t0 + (r * CH) // LANES, off:off + CH, :].astype(F32)
            x = s_ref[rows, :] + jnp.concatenate([madd] * G, axis=1)
            if biased:
                x = x + nb_ref[rows, :]
            s_ref[rows, :] = x
            mx = jnp.maximum(mx, fold8(x, jnp.maximum))
        m_old = m_ref[...]
        m_new = jnp.maximum(m_old, jnp.max(mx, axis=0, keepdims=True))
        alpha = jnp.exp2(m_old - m_new)
        for r in range(width // CH):
            rows = slice(r * CH, (r + 1) * CH)
            p_ref[rows, :] = jnp.exp2((s_ref[rows, :] - m_new).astype(BF16))
        vt = jnp.concatenate([v_ref[t0 + i] for i in range(width // LANES)], axis=1)
        acc_ref[...] = alpha * acc_ref[...] + jnp.dot(
            vt, p_ref[0:width, :], preferred_element_type=F32)
        m_ref[...] = m_new

    sa_ref[...] = logits(far_start(0), TK)
    near = pl.multiple_of(TQ * c2 + FRONT_PAD - TQ, LANES)
    sb_ref[0:NW, :] = logits(near, NW)
    update(sb_ref, near, NW, True)

    def pair(j):
        sb_ref[...] = logits(far_start(j + 1), TK)
        update(sa_ref, far_start(j), TK, False)
        sa_ref[...] = logits(far_start(j + 2), TK)
        update(sb_ref, far_start(j + 1), TK, False)

    def quad(i, _):
        pair(4 * i)
        pair(4 * i + 2)
        return 0

    nquad = nfar // 4
    lax.fori_loop(0, nquad, quad, 0)

    def rest(i, _):
        pair(4 * nquad + 2 * i)
        return 0

    lax.fori_loop(0, (nfar - 4 * nquad + 1) // 2, rest, 0)

    o = acc_ref[0:HEAD_DIM, :] / acc_ref[HEAD_DIM:HEAD_DIM + 1, :]
    for g in range(G):
        o_ref[:, g * HEAD_DIM:(g + 1) * HEAD_DIM] = o[:, g * TQ:(g + 1) * TQ].T.astype(o_ref.dtype)


def _attention_prompt(q, k, v, mask, nb):
    TQ, TK, G = Q_STEP, FAR_TILE, GROUP
    HD = q.shape[1]
    KV = HD // (G * HEAD_DIM)
    steps = mask.shape[0]
    T = steps * TQ
    ntile = mask.shape[1]
    Lp = k.shape[0]
    assert Lp == ntile * LANES and Lp >= FRONT_PAD + T and TQ == LANES
    R = G * TQ
    VR = HEAD_DIM + ONES_ROWS
    vt = v.reshape(ntile, LANES, KV, HEAD_DIM).transpose(2, 0, 3, 1)
    vt = jnp.concatenate([vt, jnp.ones((KV, ntile, ONES_ROWS, LANES), BF16)], axis=2)
    nbt = nb.reshape(KV, G, TQ, 2 * TQ).transpose(0, 3, 1, 2).reshape(KV, 2 * TQ, R)
    vmem = (2 * (2 * TQ * G * HEAD_DIM * 2 + 2 * Lp * HEAD_DIM * 2 + ntile * TQ * LANES * 2
                 + 2 * TQ * R * 4)
            + R * LANES * 6 + 2 * R * TK * 4 + 5 * R * TK * 4)
    q_map = lambda h, c: (c, h)
    return pl.pallas_call(
        _attn_prompt_kernel,
        grid=(KV, steps),
        in_specs=[pl.BlockSpec((TQ, G * HEAD_DIM), q_map),
                  pl.BlockSpec((Lp, HEAD_DIM), lambda h, c: (0, h)),
                  pl.BlockSpec((None, ntile, VR, LANES), lambda h, c: (h, 0, 0, 0)),
                  pl.BlockSpec((1, ntile, LANES, TQ), lambda h, c: (c, 0, 0, 0)),
                  pl.BlockSpec((None, 2 * TQ, R), lambda h, c: (h, 0, 0))],
        out_specs=pl.BlockSpec((TQ, G * HEAD_DIM), q_map),
        out_shape=jax.ShapeDtypeStruct((T, HD), BF16),
        scratch_shapes=[pltpu.VMEM((R, HEAD_DIM), BF16),
                        pltpu.VMEM((1, R), F32),
                        pltpu.VMEM((VR, R), F32),
                        pltpu.VMEM((TK, R), F32),
                        pltpu.VMEM((TK, R), F32),
                        pltpu.VMEM((TK, R), BF16)],
        compiler_params=_params(("parallel", "arbitrary"), vmem),
        name="dsa_attention_prompt",
    )(q, k, vt, mask, nbt)


def _t5_bucket_np(rel):
    nb = N_BUCKETS // 2
    max_exact = nb // 2
    side = np.where(rel > 0, nb, 0)
    n = np.abs(rel)
    nf = np.maximum(n, 1).astype(np.float32)
    large = max_exact + (np.log(nf / np.float32(max_exact))
                         / np.float32(math.log(REL_MAX_DIST / max_exact))
                         * np.float32(nb - max_exact)).astype(np.int32)
    large = np.minimum(large, nb - 1)
    return side + np.where(n < max_exact, n, large)


def _near_bias(rel_bias, rel, far_rel):
    far = int(_t5_bucket_np(np.array(far_rel)))
    assert far_rel < 0 and far == int(_t5_bucket_np(np.array(-10 ** 6)))
    rb = rel_bias.astype(F32) * LOG2E
    tab = rb[_t5_bucket_np(rel)] - rb[far][None, None, None, :]
    return tab.transpose(0, 3, 1, 2)


def _near_bias_stream(rel_bias, TQ, TK, phases):
    t = np.arange(TQ)[:, None]
    j = np.arange(2 * TK)[None, :]
    rel = np.stack([j - TK - ph + TQ - t for ph in phases])
    return _near_bias(rel_bias, rel, TQ - TK - 2)


def _near_bias_prompt(rel_bias):
    TQ = Q_STEP
    rel = np.arange(2 * TQ)[None, :] - TQ - np.arange(TQ)[:, None]
    return _near_bias(rel_bias, rel[None], -TQ - 1)[0]


def _conv_kernel(x_ref, halo_ref, buf_ref, w_ref, b_ref, xc_ref, xcb_ref, ext_ref, *, tt):
    first = pl.program_id(1) == 0
    ext_ref[0:SUBLANES, :] = jnp.where(first, buf_ref[...], halo_ref[...])
    ext_ref[SUBLANES:, :] = x_ref[...]
    y = jnp.broadcast_to(b_ref[...], x_ref.shape)
    for j in range(CONV_W):
        off = SUBLANES - (CONV_W - 1) + j
        y = y + ext_ref[off:off + tt, :] * w_ref[j:j + 1, :]
    xc_ref[...] = y
    xcb_ref[...] = y.astype(BF16)


def _conv(x, buf8, w, b, *, row0, B, T, tt):
    C = x.shape[1]
    assert row0 % tt == 0 and T % tt == 0 and tt % SUBLANES == 0
    nt = T // tt
    hb = tt // SUBLANES
    blk0 = row0 // tt
    main_map = lambda b, i: (blk0 + b * nt + i, 0)
    halo_map = lambda b, i: (jnp.maximum((blk0 + b * nt + i) * hb - 1, 0), 0)
    out_map = lambda b, i: (b * nt + i, 0)
    return pl.pallas_call(
        functools.partial(_conv_kernel, tt=tt),
        grid=(B, nt),
        in_specs=[pl.BlockSpec((tt, C), main_map),
                  pl.BlockSpec((SUBLANES, C), halo_map),
                  pl.BlockSpec((None, SUBLANES, C), lambda b, i: (b, 0, 0)),
                  pl.BlockSpec((CONV_W, C), lambda b, i: (0, 0)),
                  pl.BlockSpec((1, C), lambda b, i: (0, 0))],
        out_specs=[pl.BlockSpec((tt, C), out_map), pl.BlockSpec((tt, C), out_map)],
        out_shape=[jax.ShapeDtypeStruct((B * T, C), F32),
                   jax.ShapeDtypeStruct((B * T, C), BF16)],
        scratch_shapes=[pltpu.VMEM((tt + SUBLANES, C), F32)],
        compiler_params=_params(("parallel", "parallel"), 2 * tt * C * 10 + tt * C * 12),
        name="causal_conv",
    )(x, x, buf8, w, b.reshape(1, C))


def _scan_kernel(a_ref, u_ref, g_ref, h0_ref, y_ref, hlast_ref, h_ref, hs_ref, *, tt):
    i = pl.program_id(1)

    @pl.when(i == 0)
    def _():
        h_ref[...] = h0_ref[...]

    def step(t, h):
        h = a_ref[pl.ds(t, 1), :] * h + u_ref[pl.ds(t, 1), :]
        hs_ref[pl.ds(t, 1), :] = h
        return h

    h = lax.fori_loop(0, tt, step, h_ref[...])
    h_ref[...] = h
    hlast_ref[...] = h
    y_ref[...] = (hs_ref[...] * jax.nn.gelu(g_ref[...])).astype(y_ref.dtype)


def _scan(a, u, g, h0, *, row0, B, T, tt):
    C = a.shape[1]
    assert row0 % tt == 0 and T % tt == 0
    nt = T // tt
    blk0 = row0 // tt
    blk = pl.BlockSpec((tt, C), lambda b, i: (b * nt + i, 0))
    g_blk = pl.BlockSpec((tt, C), lambda b, i: (blk0 + b * nt + i, 0))
    vec = pl.BlockSpec((None, 1, C), lambda b, i: (b, 0, 0))
    return pl.pallas_call(
        functools.partial(_scan_kernel, tt=tt),
        grid=(B, nt),
        in_specs=[blk, blk, g_blk, vec],
        out_specs=[blk, vec],
        out_shape=[jax.ShapeDtypeStruct((B * T, C), BF16),
                   jax.ShapeDtypeStruct((B, 1, C), F32)],
        scratch_shapes=[pltpu.VMEM((1, C), F32), pltpu.VMEM((tt, C), F32)],
        compiler_params=_params(("parallel", "arbitrary"), 2 * tt * C * 14 + tt * C * 12),
        name="rglru_scan",
    )(a, u, g, h0)


GATE_TN = 256


def _gate_window(rb, C):
    raw = [((j * GATE_TN) // rb * rb) // LANES * LANES for j in range(C // GATE_TN)]
    ends = [((j * GATE_TN + GATE_TN - 1) // rb + 1) * rb for j in range(C // GATE_TN)]
    kw = -(-max(e - s for s, e in zip(raw, ends)) // LANES) * LANES
    kw = min(kw, C)
    starts = [min(s, C - kw) for s in raw]
    assert all(s + kw >= e for s, e in zip(starts, ends))
    return starts, kw


def _blockdiag_tiles(w):
    nblk, rb, _ = w.shape
    C = nblk * rb
    starts, kw = _gate_window(rb, C)
    dense = jax.scipy.linalg.block_diag(*[w[i] for i in range(nblk)])
    tiles = [dense[s:s + kw, j * GATE_TN:(j + 1) * GATE_TN] for j, s in enumerate(starts)]
    return jnp.stack(tiles).astype(BF16)


def _gates_kernel(x_ref, wa_ref, wx_ref, xc_ref, ba_ref, bx_ref, lam_ref, a_ref, u_ref, *, rb, kw):
    C = x_ref.shape[1]
    j = pl.program_id(1)
    start = pl.multiple_of(jnp.minimum(((j * GATE_TN) // rb * rb) // LANES * LANES, C - kw), LANES)
    vecs = [ba_ref[...], bx_ref[...], lam_ref[...]]
    sub = math.gcd(x_ref.shape[0], 256)
    for r in range(x_ref.shape[0] // sub):
        rows = slice(r * sub, (r + 1) * sub)
        x = x_ref[rows, pl.ds(start, kw)]
        accs = [jnp.dot(x, w[0], preferred_element_type=F32) for w in (wa_ref, wx_ref)]
        a, u = _ep_rglru_gates(accs, [xc_ref[rows, :]], vecs)
        a_ref[rows, :] = a
        u_ref[rows, :] = u


def _rglru_gates(xcb, xc, wa_t, wx_t, ba, bx, lam, *, rb, tm):
    rows, C = xc.shape
    nt, kw, _ = wa_t.shape
    tile = lambda: pl.BlockSpec((tm, GATE_TN), lambda i, j: (i, j))
    vec = lambda: pl.BlockSpec((1, GATE_TN), lambda i, j: (0, j))
    wspec = lambda: pl.BlockSpec((1, kw, GATE_TN), lambda i, j: (j, 0, 0))
    vmem = 2 * (tm * C * 2 + 2 * kw * GATE_TN * 2 + 3 * tm * GATE_TN * 4) + 8 * tm * GATE_TN * 4
    return pl.pallas_call(
        functools.partial(_gates_kernel, rb=rb, kw=kw),
        grid=(rows // tm, nt),
        in_specs=[pl.BlockSpec((tm, C), lambda i, j: (i, 0)), wspec(), wspec(), tile(),
                  vec(), vec(), vec()],
        out_specs=[tile(), tile()],
        out_shape=[jax.ShapeDtypeStruct((rows, C), F32)] * 2,
        compiler_params=_params(("parallel", "arbitrary"), vmem),
        name="rglru_gates",
    )(xcb, wa_t, wx_t, xc, ba.reshape(1, C), bx.reshape(1, C), lam.reshape(1, C))


def _pick(n, cands):
    for c in cands:
        if n % c == 0:
            return c
    raise ValueError(f"no tile for {n}")


def _layer(x, hist, p, rel_bias, dims):
    Tp, Bs, Ts, past = dims
    M, D = x.shape
    Ms = Bs * Ts
    k_past, v_past, ki_past, conv_buf, h0 = hist
    KV = k_past.shape[2]
    HQ = KV * GROUP * HEAD_DIM
    KVD = KV * HEAD_DIM
    DI = ki_past.shape[-1]
    C = p["conv_w"].shape[-1]
    n_in = p["w_in"].shape[1]
    HI = (n_in - HQ - 2 * KVD - DI - 2 * C - 2 * D) // (DI + 1)
    assert DI == LANES
    sizes = (HQ, KVD, KVD, HI * DI, DI, HI, C, C, D, D)
    offs = np.concatenate([[0], np.cumsum(sizes)])
    assert offs[-1] == n_in
    w_in = p["w_in"]

    def wslice(i, j=None):
        j = i if j is None else j
        return w_in[:, offs[i]:offs[j + 1]].astype(BF16)

    tm = _pick(M, (768, 512, 256, 128, 64))
    TK = KEY_TILE

    h = _rmsnorm(x, p["norm_mix"], tm)

    def proj(w, ep, dt, tn, vecs=(), name="in_proj"):
        return _matmul(h, [w], ep, [dt], tm=tm, tn=tn, vecs=vecs, name=name)[0]

    wide = (1024, 512, 256, 128)
    q = proj(wslice(0), _ep_headnorm(HEAD_DIM ** -0.5 * LOG2E), BF16, _pick(HQ, wide),
             vecs=[jnp.tile(p["q_norm"], HQ // HEAD_DIM)], name="in_proj_q")
    k = proj(wslice(1), _ep_headnorm(1.0), F32, _pick(KVD, wide),
             vecs=[jnp.tile(p["k_norm"], KV)], name="in_proj_k")
    v = proj(wslice(2), _ep_identity, F32, _pick(KVD, wide), name="in_proj_v")
    qi = proj(wslice(3), _ep_identity, BF16, _pick(HI * DI, wide), name="in_proj_qi")
    kw_w = jnp.pad(wslice(4, 5), ((0, 0), (0, 2 * LANES - DI - HI)))
    kiwi = proj(kw_w, _ep_identity, F32, 2 * LANES, name="in_proj_ki")
    ki, wi = kiwi[:, :DI], kiwi[:, DI:DI + HI]
    tn_c = _pick(C, (768, 384, 128))
    xr = proj(wslice(6), _ep_identity, F32, tn_c, name="in_proj_xr")
    gr = proj(wslice(7), _ep_identity, F32, tn_c, name="in_proj_gr")
    tn_d = _pick(D, (512, 256, 128))
    tn_w = _pick(D, wide)
    ga = proj(wslice(8), _ep_identity, F32, tn_w, name="in_proj_ga")
    gb = proj(wslice(9), _ep_identity, F32, tn_w, name="in_proj_gb")

    k_bf, v_bf, ki_bf = k.astype(BF16), v.astype(BF16), ki.astype(BF16)

    assert Tp % Q_STEP == 0
    back_p = -Tp % FAR_TILE
    n_sel_p = min(TOPK_MAX, Tp // 4)
    mask_p = _select_prompt(qi, wi, jnp.pad(ki_bf[:Tp], ((0, back_p), (0, 0))), n_sel_p, Tp)
    kv_pad = lambda a: jnp.pad(a[:Tp], ((FRONT_PAD, back_p), (0, 0)))
    o_p = _attention_prompt(q, kv_pad(k_bf), kv_pad(v_bf), mask_p, _near_bias_prompt(rel_bias))

    Ls = past + Ts
    nt_s = -(-Ls // TK)
    pad_s = nt_s * TK - Ls

    def with_cache(cache, new):
        new = new[Tp:].reshape(Bs, Ts, -1)
        parts = [cache.reshape(Bs, past, -1).astype(BF16), new]
        if pad_s:
            parts.append(jnp.zeros((Bs, pad_s, new.shape[-1]), BF16))
        return jnp.concatenate(parts, axis=1)

    n_sel_s = min(TOPK_MAX, Ls // 4)
    mask_s = _select(qi[Tp:], wi[Tp:], with_cache(ki_past, ki_bf), TQ=Ts, NT=nt_s,
                     lend_a=0, lend_b=Ls, n_sel=n_sel_s, batched=True)
    nb_s = _near_bias_stream(rel_bias, Ts, TK, [Ls - ((Ls - 1) // TK) * TK])
    o_s = _attention(q[Tp:], with_cache(k_past, k_bf), with_cache(v_past, v_bf), mask_s, nb_s,
                     TQ=Ts, lend_a=0, lend_b=Ls, batched=True)
    o_a = jnp.concatenate([o_p, o_s], axis=0)

    assert C % GATE_TN == 0
    wa_t = _blockdiag_tiles(p["rg_wa"])
    wx_t = _blockdiag_tiles(p["rg_wx"])

    def griffin(row0, B, T, buf, h_init):
        tt = _pick(math.gcd(T, row0) if row0 else T, (128, 64, 32, 16, 8))
        buf8 = jnp.pad(buf.astype(F32), ((0, 0), (SUBLANES - (CONV_W - 1), 0), (0, 0)))
        xc, xcb = _conv(xr, buf8, p["conv_w"], p["conv_b"], row0=row0, B=B, T=T, tt=tt)
        rows = B * T
        tmr = _pick(rows, (1024, 512, 256, 128, 64, 32))
        a, u = _rglru_gates(xcb, xc, wa_t, wx_t, p["rg_ba"], p["rg_bx"], p["rg_lambda"],
                            rb=p["rg_wa"].shape[1], tm=tmr)
        y, h_last = _scan(a, u, gr, h_init.reshape(B, 1, C), row0=row0, B=B, T=T, tt=tt)
        tail = xr[row0:row0 + rows].reshape(B, T, C)[:, -(CONV_W - 1):]
        conv_new = jnp.concatenate([buf.astype(F32), tail], axis=1)[:, -(CONV_W - 1):]
        return y, conv_new, h_last.reshape(B, C)

    y_p, conv_p, h_p = griffin(0, 1, Tp, jnp.zeros((1, CONV_W - 1, C), F32), jnp.zeros((1, C), F32))
    y_s, conv_s, h_s = griffin(Tp, Bs, Ts, conv_buf, h0)
    o_b = jnp.concatenate([y_p, y_s], axis=0)

    part = _matmul(o_a, [p["w_out_attn"].astype(BF16)], _ep_gate, [F32], tm=tm, tn=tn_w,
                   exts=[ga], name="out_attn")[0]
    merged = _matmul(o_b, [p["w_out_rg"].astype(BF16)], _ep_gate_add, [BF16],
                     tm=_pick(M, (512, 256, 128, 64)), tn=tn_w, exts=[gb, part], name="out_rg")[0]
    x1 = _matmul(merged, [p["w_o"].astype(BF16)], _ep_residual, [F32], tm=tm, tn=tn_w,
                 exts=[x], name="w_o")[0]
    hf = _rmsnorm(x1, p["norm_ffn"], tm)
    FF = p["ffn_w1"].shape[1]
    tn_f = _pick(FF, (512, 256, 128))
    act = _matmul(hf, [p["ffn_w1"].astype(BF16), p["ffn_w3"].astype(BF16)], _ep_swiglu, [BF16],
                  tm=_pick(M, (1536, 768, 512, 256, 128, 64)), tn=tn_f, name="ffn_up",
                  single_buffer_w=True)[0]
    x2 = _matmul(act, [p["ffn_w2"].astype(BF16)], _ep_residual, [F32],
                 tm=_pick(M, (384, 256, 128, 64)), tn=tn_d, exts=[x1], name="ffn_down")[0]

    new_p = (k[:Tp].reshape(1, Tp, KV, HEAD_DIM), v[:Tp].reshape(1, Tp, KV, HEAD_DIM),
             ki[:Tp].reshape(1, Tp, DI), conv_p, h_p)
    new_s = (k[Tp:].reshape(Bs, Ts, KV, HEAD_DIM), v[Tp:].reshape(Bs, Ts, KV, HEAD_DIM),
             ki[Tp:].reshape(Bs, Ts, DI), conv_s, h_s)
    return x2, new_p, new_s


def kernel(x_prompt, x_sample, cache_k, cache_v, cache_kidx, state_conv, state_rglru, norm_mix, w_in, q_norm, k_norm, rel_bias, conv_w, conv_b, rg_wa, rg_ba, rg_wx, rg_bx, rg_lambda, w_out_attn, w_out_rg, w_o, norm_ffn, ffn_w1, ffn_w3, ffn_w2):
    Bp, Tp, D = x_prompt.shape
    Bs, Ts, _ = x_sample.shape
    assert Bp == 1 and Tp % CHUNK == 0
    depth = w_in.shape[0]
    past = cache_k.shape[2]
    x = jnp.concatenate([x_prompt.reshape(Tp, D), x_sample.reshape(Bs * Ts, D)], axis=0)
    outs_p, outs_s = [], []
    for l in range(depth):
        p = dict(norm_mix=norm_mix[l], w_in=w_in[l], q_norm=q_norm[l], k_norm=k_norm[l],
                 conv_w=conv_w[l], conv_b=conv_b[l], rg_wa=rg_wa[l], rg_ba=rg_ba[l],
                 rg_wx=rg_wx[l], rg_bx=rg_bx[l], rg_lambda=rg_lambda[l],
                 w_out_attn=w_out_attn[l], w_out_rg=w_out_rg[l], w_o=w_o[l],
                 norm_ffn=norm_ffn[l], ffn_w1=ffn_w1[l], ffn_w3=ffn_w3[l], ffn_w2=ffn_w2[l])
        hist = (cache_k[l], cache_v[l], cache_kidx[l], state_conv[l], state_rglru[l])
        x, new_p, new_s = _layer(x, hist, p, rel_bias, (Tp, Bs, Ts, past))
        outs_p.append(new_p)
        outs_s.append(new_s)
    stack = lambda outs, i: jnp.stack([o[i] for o in outs])
    return (x[:Tp].reshape(1, Tp, D), x[Tp:].reshape(Bs, Ts, D),
            *[stack(outs_p, i) for i in range(5)],
            *[stack(outs_s, i) for i in range(5)])
```
